```python
import jax, jax.numpy as jnp
from jax import lax
import numpy as np

D_MODEL = 1024
BATCH = 2
SEQ = 8192
DEPTH = 4
DEC_BATCH = 128
DEC_SEQ = 8
PAST_LEN = 8192
PAGE_SIZE = 128

M_HEADS = 4
M_DQK = 128
M_DV = D_MODEL // M_HEADS
M_CHUNK = 64
A_HEADS = 16
A_KV = 4
A_HD = D_MODEL // A_HEADS
A_GROUP = A_HEADS // A_KV
WINDOW = 128
WIN_CACHE = min(WINDOW, PAST_LEN)
D_FF = 4 * D_MODEL
EPS = 1e-6
F_BIAS = 3.0
_SIZES = (M_HEADS * M_DQK, M_HEADS * M_DQK, M_HEADS * M_DV, M_HEADS * M_DV, M_HEADS, M_HEADS,
          A_HEADS * A_HD, A_KV * A_HD, A_KV * A_HD, D_MODEL, D_MODEL)
D_IN = 2 * M_HEADS * M_DQK + 2 * M_HEADS * M_DV + 2 * M_HEADS + A_HEADS * A_HD + 2 * A_KV * A_HD + 2 * D_MODEL

kernel_name = 'hybrid_mlstm_swa_sink_decoder_step'


def rmsnorm(x, g):
    xf = x.astype(jnp.float32)
    y = xf * lax.rsqrt(jnp.mean(xf * xf, axis=-1, keepdims=True) + EPS)
    return (y * g.astype(jnp.float32)).astype(x.dtype)


def mlstm_chunked(q, k, v, i_log, logf, C0, n0, m0):
    f32 = jnp.float32
    B, H, T, dk = q.shape
    L = M_CHUNK if T % M_CHUNK == 0 else T
    nc = T // L

    def to_chunks(a):
        return jnp.moveaxis(a.astype(f32).reshape(B, H, nc, L, *a.shape[3:]), 2, 0)

    xs = (to_chunks(q), to_chunks(k), to_chunks(v), to_chunks(i_log), to_chunks(logf))
    causal = jnp.tril(jnp.ones((L, L), dtype=bool))

    def step(carry, inp):
        C, n, m = carry
        qc, kc, vc, ic, fc = inp
        b = jnp.cumsum(fc, axis=-1)
        dmat = jnp.where(causal, b[..., :, None] - b[..., None, :] + ic[..., None, :], -jnp.inf)
        inter = b + m[..., None]
        m_row = jnp.maximum(inter, jnp.max(dmat, axis=-1))
        w = jnp.exp(dmat - m_row[..., None])
        w_int = jnp.exp(inter - m_row)
        s = jnp.einsum('bhtd,bhsd->bhts', qc, kc) * w
        num = w_int[..., None] * jnp.einsum('bhtd,bhde->bhte', qc, C) + jnp.einsum('bhts,bhse->bhte', s, vc)
        den = w_int * jnp.einsum('bhtd,bhd->bht', qc, n) + jnp.sum(s, axis=-1)
        h = num / jnp.maximum(jnp.abs(den), jnp.exp(-m_row))[..., None]
        m_new = m_row[..., -1]
        wk = jnp.exp(b[..., -1:] - b + ic - m_new[..., None])
        decay = jnp.exp(inter[..., -1] - m_new)
        C_new = decay[..., None, None] * C + jnp.einsum('bhs,bhsd,bhse->bhde', wk, kc, vc)
        n_new = decay[..., None] * n + jnp.einsum('bhs,bhsd->bhd', wk, kc)
        return (C_new, n_new, m_new), h

    (C, n, m), hs = lax.scan(step, (C0.astype(f32), n0.astype(f32), m0.astype(f32)), xs)
    h = jnp.moveaxis(hs, 0, 2).reshape(B, H, T, -1)
    return h, C, n, m


def sink_attend(q, k, v, mask, sinks):
    f32 = jnp.float32
    s = jnp.einsum('...kgqd,...ksd->...kgqs', q, k).astype(f32) * (A_HD ** -0.5)
    s = jnp.where(mask, s, -jnp.inf)
    sk = sinks.astype(f32)[:, :, None]
    mx = jnp.maximum(jnp.max(s, axis=-1), sk)
    p = jnp.exp(s - mx[..., None])
    den = jnp.sum(p, axis=-1) + jnp.exp(sk - mx)
    return jnp.einsum('...kgqs,...ksd->...kgqd', (p / den[..., None]).astype(v.dtype), v)


def swa_prompt(q, k, v, sinks):
    B, T = q.shape[:2]
    nb = T // WINDOW
    qb = q.reshape(B, nb, WINDOW, A_KV, A_GROUP, A_HD).transpose(0, 1, 3, 4, 2, 5)

    def band(a):
        ab = a.reshape(B, nb, WINDOW, A_KV, A_HD)
        prev = jnp.concatenate([jnp.zeros_like(ab[:, :1]), ab[:, :-1]], axis=1)
        return jnp.concatenate([prev, ab], axis=2).transpose(0, 1, 3, 2, 4)

    qpos = jnp.arange(nb)[:, None] * WINDOW + jnp.arange(WINDOW)[None, :]
    kpos = jnp.arange(nb)[:, None] * WINDOW - WINDOW + jnp.arange(2 * WINDOW)[None, :]
    diff = qpos[:, :, None] - kpos[:, None, :]
    mask = (diff >= 0) & (diff < WINDOW) & (kpos[:, None, :] >= 0)
    o = sink_attend(qb, band(k), band(v), mask[:, None, None], sinks)
    o = o.transpose(0, 1, 4, 2, 3, 5).reshape(B, T, A_HEADS * A_HD)
    return o, k[:, -WIN_CACHE:], v[:, -WIN_CACHE:]


def swa_sample(q, k, v, ck, cv, sinks):
    B, T = q.shape[:2]
    wc = ck.shape[1]
    kk = jnp.concatenate([ck.astype(k.dtype), k], axis=1)
    vv = jnp.concatenate([cv.astype(v.dtype), v], axis=1)
    qpos = PAST_LEN + jnp.arange(T)
    kpos = PAST_LEN - wc + jnp.arange(wc + T)
    diff = qpos[:, None] - kpos[None, :]
    mask = (diff >= 0) & (diff < WINDOW)
    qh = q.reshape(B, T, A_KV, A_GROUP, A_HD).transpose(0, 2, 3, 1, 4)
    o = sink_attend(qh, kk.transpose(0, 2, 1, 3), vv.transpose(0, 2, 1, 3), mask, sinks)
    o = o.transpose(0, 3, 1, 2, 4).reshape(B, T, A_HEADS * A_HD)
    return o, kk[:, -wc:], vv[:, -wc:]


def trunk_layer(x, norm_attn, w_in, b_gate, mlstm_norm, sinks, w_out, norm_mlp, w_up, w_down,
                C0, n0, m0, ck=None, cv=None):
    f32 = jnp.float32
    B, T, _ = x.shape
    xn = rmsnorm(x, norm_attn)
    proj = xn @ w_in
    mq, mk, mv, mo, mi, mf, aq, ak, av, ga, gb = jnp.split(proj, np.cumsum(_SIZES)[:-1].tolist(), axis=-1)
    gates = jnp.concatenate([mi, mf], axis=-1).astype(f32) + b_gate.astype(f32)
    i_log = gates[..., :M_HEADS].transpose(0, 2, 1)
    logf = jax.nn.log_sigmoid(gates[..., M_HEADS:]).transpose(0, 2, 1)
    hq = mq.reshape(B, T, M_HEADS, M_DQK).transpose(0, 2, 1, 3) * (M_DQK ** -0.5)
    hk = mk.reshape(B, T, M_HEADS, M_DQK).transpose(0, 2, 1, 3)
    hv = mv.reshape(B, T, M_HEADS, M_DV).transpose(0, 2, 1, 3)
    h, C, n, m = mlstm_chunked(hq, hk, hv, i_log, logf, C0, n0, m0)
    h = h.transpose(0, 2, 1, 3)
    h = h * lax.rsqrt(jnp.mean(h * h, axis=-1, keepdims=True) + EPS) * mlstm_norm.astype(f32).reshape(M_HEADS, M_DV)
    branch_a = jax.nn.sigmoid(mo) * h.reshape(B, T, M_HEADS * M_DV).astype(x.dtype)
    sk = sinks.reshape(A_KV, A_GROUP)
    qa = aq.reshape(B, T, A_HEADS, A_HD)
    ka = ak.reshape(B, T, A_KV, A_HD)
    va = av.reshape(B, T, A_KV, A_HD)
    if ck is None:
        branch_b, kc, vc = swa_prompt(qa, ka, va, sk)
    else:
        branch_b, kc, vc = swa_sample(qa, ka, va, ck, cv, sk)
    merged = jax.nn.sigmoid(ga) * branch_a + jax.nn.sigmoid(gb) * branch_b
    x = x + merged @ w_out
    xn2 = rmsnorm(x, norm_mlp)
    x = x + jnp.square(jax.nn.relu(xn2 @ w_up)) @ w_down
    return x, C, n, m, kc, vc


def setup_inputs(seed: int = 0) -> dict:
    key = jax.random.key(seed)
    ks = jax.random.split(key, 20)
    nrm = lambda k, shape, scale: jax.random.normal(k, shape, jnp.float32) * scale
    b_i = nrm(ks[9], (DEPTH, M_HEADS), 0.1)
    b_f = F_BIAS + nrm(ks[10], (DEPTH, M_HEADS), 0.5)
    return {
        'x_prompt': nrm(ks[0], (BATCH, SEQ, D_MODEL), 1.0),
        'x_sample': nrm(ks[1], (DEC_BATCH, DEC_SEQ, D_MODEL), 1.0),
        'state_C': nrm(ks[2], (DEPTH, DEC_BATCH, M_HEADS, M_DQK, M_DV), 0.3),
        'state_n': nrm(ks[3], (DEPTH, DEC_BATCH, M_HEADS, M_DQK), 1.0),
        'state_m': jax.random.uniform(ks[4], (DEPTH, DEC_BATCH, M_HEADS), jnp.float32, -1.0, 1.0),
        'cache_k': nrm(ks[5], (DEPTH, DEC_BATCH, WIN_CACHE, A_KV, A_HD), 1.0),
        'cache_v': nrm(ks[6], (DEPTH, DEC_BATCH, WIN_CACHE, A_KV, A_HD), 1.0),
        'norm_attn': 1.0 + nrm(ks[7], (DEPTH, D_MODEL), 0.02),
        'w_in': nrm(ks[8], (DEPTH, D_MODEL, D_IN), D_MODEL ** -0.5),
        'b_gate': jnp.concatenate([b_i, b_f], axis=-1),
        'mlstm_norm': 1.0 + nrm(ks[11], (DEPTH, M_HEADS * M_DV), 0.02),
        'sinks': nrm(ks[12], (DEPTH, A_HEADS), 0.5),
        'w_out': nrm(ks[13], (DEPTH, D_MODEL, D_MODEL), D_MODEL ** -0.5),
        'norm_mlp': 1.0 + nrm(ks[14], (DEPTH, D_MODEL), 0.02),
        'w_up': nrm(ks[15], (DEPTH, D_MODEL, D_FF), D_MODEL ** -0.5),
        'w_down': nrm(ks[16], (DEPTH, D_FF, D_MODEL), D_FF ** -0.5),
        'norm_final': 1.0 + nrm(ks[17], (D_MODEL,), 0.02),
    }


def reference(x_prompt, x_sample, state_C, state_n, state_m, cache_k, cache_v,
              norm_attn, w_in, b_gate, mlstm_norm, sinks, w_out, norm_mlp, w_up, w_down, norm_final):
    f32 = jnp.float32
    xp, xs = x_prompt, x_sample
    pC, pn, pm, pk, pv = [], [], [], [], []
    sC, sn, sm, skk, svv = [], [], [], [], []
    zC = jnp.zeros((BATCH, M_HEADS, M_DQK, M_DV), f32)
    zn = jnp.zeros((BATCH, M_HEADS, M_DQK), f32)
    zm = jnp.zeros((BATCH, M_HEADS), f32)
    for l in range(DEPTH):
        w = (norm_attn[l], w_in[l], b_gate[l], mlstm_norm[l], sinks[l], w_out[l], norm_mlp[l], w_up[l], w_down[l])
        xp, C, n, m, kc, vc = trunk_layer(xp, *w, zC, zn, zm)
        pC.append(C); pn.append(n); pm.append(m); pk.append(kc); pv.append(vc)
        xs, C, n, m, kc, vc = trunk_layer(xs, *w, state_C[l], state_n[l], state_m[l], cache_k[l], cache_v[l])
        sC.append(C); sn.append(n); sm.append(m); skk.append(kc); svv.append(vc)
    y_prompt = rmsnorm(xp, norm_final)
    y_sample = rmsnorm(xs, norm_final)
    p_C, p_n, p_m, p_k, p_v = jnp.stack(pC), jnp.stack(pn), jnp.stack(pm), jnp.stack(pk), jnp.stack(pv)
    s_C, s_n, s_m, s_k, s_v = jnp.stack(sC), jnp.stack(sn), jnp.stack(sm), jnp.stack(skk), jnp.stack(svv)
    return (y_prompt, y_sample, p_C, p_n, p_m, p_k, p_v, s_C, s_n, s_m, s_k, s_v)
```

```python
import functools

import jax
import jax.numpy as jnp
from jax import lax
from jax.experimental import pallas as pl
from jax.experimental.pallas import tpu as pltpu

F32 = jnp.float32
BF16 = jnp.bfloat16

D_MODEL = 1024
M_HEADS = 4
M_DQK = 128
M_DV = D_MODEL // M_HEADS
A_HEADS = 16
A_KV = 4
A_GROUP = A_HEADS // A_KV
A_HD = D_MODEL // A_HEADS
WINDOW = 128
D_FF = 4 * D_MODEL
EPS = 1e-6

COL_MV = 0
COL_MO = 1024
COL_AQ = 2048
COL_GA = 3072
COL_GB = 4096
COL_MQ = 5120
COL_AK = 5632
COL_AV = 5888
D_PROJ = 6144
ROW_KT = 0
ROW_GATE = M_HEADS * M_DQK
GATE_ROWS = 16
D_PROJT = ROW_GATE + GATE_ROWS
C_EXT = M_DV + 128

MLSTM_CHUNK = 256
SAMPLE_TILE = 128
VMEM_LIMIT = 56 * 1024 * 1024


def _pick(n, candidates):
    for c in candidates:
        if n % c == 0:
            return c
    raise ValueError(f"no block size for {n}")


def _sigmoid(x):
    return 1.0 / (1.0 + jnp.exp(-x))


def _log_sigmoid(x):
    return jnp.minimum(x, 0.0) - jnp.log1p(jnp.exp(-jnp.abs(x)))


def _rms(x, g):
    y = x * lax.rsqrt(jnp.mean(x * x, axis=-1, keepdims=True) + EPS)
    return y * g


def _dot(a, b):
    return jnp.dot(a, b, preferred_element_type=F32)


def _dot_nt(a, b):
    return lax.dot_general(a, b, (((1,), (1,)), ((), ())), preferred_element_type=F32)


def _in_proj_kernel(x_ref, g_ref, w_ref, wt_ref, bt_ref, scale_ref, proj_ref, projt_ref, xn_ref):
    @pl.when(pl.program_id(1) == 0)
    def _():
        xn = _rms(x_ref[...], g_ref[...]).astype(BF16)
        xn_ref[...] = xn
        projt_ref[...] = _dot_nt(wt_ref[...], xn) + bt_ref[...]

    proj_ref[...] = _dot(xn_ref[...], w_ref[...]) * scale_ref[...]


def _in_proj(x, g, w, wt, bt, scale):
    n = x.shape[0]
    tm = _pick(n, (1024, 512, 256, 128))
    tn = 1024
    return pl.pallas_call(
        _in_proj_kernel,
        grid=(n // tm, D_PROJ // tn),
        in_specs=[
            pl.BlockSpec((tm, D_MODEL), lambda i, j: (i, 0)),
            pl.BlockSpec((1, D_MODEL), lambda i, j: (0, 0)),
            pl.BlockSpec((D_MODEL, tn), lambda i, j: (0, j)),
            pl.BlockSpec((D_PROJT, D_MODEL), lambda i, j: (0, 0)),
            pl.BlockSpec((D_PROJT, 1), lambda i, j: (0, 0)),
            pl.BlockSpec((1, tn), lambda i, j: (0, j)),
        ],
        out_specs=[
            pl.BlockSpec((tm, tn), lambda i, j: (i, j)),
            pl.BlockSpec((D_PROJT, tm), lambda i, j: (0, i)),
        ],
        out_shape=[
            jax.ShapeDtypeStruct((n, D_PROJ), F32),
            jax.ShapeDtypeStruct((D_PROJT, n), F32),
        ],
        scratch_shapes=[pltpu.VMEM((tm, D_MODEL), BF16)],
        compiler_params=pltpu.CompilerParams(
            dimension_semantics=("parallel", "arbitrary"), vmem_limit_bytes=VMEM_LIMIT),
        name="in_proj",
    )(x, g, w, wt, bt, scale)


def _seg_cumsum(x, seg):
    pos = lax.broadcasted_iota(jnp.int32, x.shape, 1) & (seg - 1)
    sh = 1
    while sh < seg:
        x = x + jnp.where(pos >= sh, pltpu.roll(x, sh, axis=1), 0.0)
        sh *= 2
    return x


def _mlstm_masks(lt, seg):
    t = lax.broadcasted_iota(jnp.int32, (lt, lt), 0)
    s = lax.broadcasted_iota(jnp.int32, (lt, lt), 1)
    causal = (s <= t) & ((s | (seg - 1)) == (t | (seg - 1)))
    last = t == (s | (seg - 1))
    return causal, last


def _mlstm_tile(q, kt, v, i_row, lf_row, b_row, m_col, states, causal, last, seg):
    lt = q.shape[0]
    nseg = lt // seg
    a_row = i_row - b_row
    a_mat = jnp.where(causal, a_row, -jnp.inf)
    g_col = jnp.maximum(jnp.max(a_mat, axis=1, keepdims=True), m_col)
    b_col = jnp.sum(jnp.where(causal, lf_row, 0.0), axis=1, keepdims=True)
    w = jnp.exp(a_mat - g_col)
    w_int = jnp.exp(m_col - g_col)
    qb = q.astype(BF16)
    s = _dot(qb, kt.astype(BF16)) * w
    vb = v.astype(BF16)
    inter = jnp.concatenate(
        [_dot(qb[i * seg:(i + 1) * seg], states[i].astype(BF16)) for i in range(nseg)], axis=0)
    num = w_int * inter[:, :M_DV] + _dot(s.astype(BF16), vb)
    den = w_int * inter[:, M_DV:M_DV + 1] + jnp.sum(s, axis=1, keepdims=True)
    m_row = b_col + g_col
    h = num * (1.0 / jnp.maximum(jnp.abs(den), jnp.exp(-m_row)))

    wk_row = jnp.sum(jnp.where(last, w, 0.0), axis=0, keepdims=True)
    ktw = kt * wk_row
    one_col = (lax.broadcasted_iota(jnp.int32, (lt, C_EXT - M_DV), 1) == 0).astype(BF16)
    v_ext = jnp.concatenate([vb, one_col], axis=1)
    if nseg == 1:
        upd = _dot(ktw.astype(BF16), v_ext)
    else:
        lane_seg = lax.broadcasted_iota(jnp.int32, ktw.shape, 1) | (seg - 1)
        lhs = jnp.concatenate(
            [jnp.where(lane_seg == i * seg + seg - 1, ktw, 0.0).astype(BF16) for i in range(nseg)],
            axis=0)
        upd = _dot(lhs, v_ext)
    dk = kt.shape[0]
    new_states, new_m = [], []
    for i in range(nseg):
        r = i * seg + seg - 1
        new_states.append(w_int[r:r + 1] * states[i] + upd[i * dk:(i + 1) * dk])
        new_m.append(m_row[r:r + 1])
    return h, new_states, new_m


def _mlstm_finish(h, o_pre, nw):
    hn = h * lax.rsqrt(jnp.mean(h * h, axis=-1, keepdims=True) + EPS) * nw
    return _sigmoid(o_pre) * hn


def _mlstm_prompt_kernel(q_ref, kt_ref, v_ref, o_ref, gt_ref, nw_ref,
                         h_ref, c_out, n_out, m_out, c_scr, m_scr):
    c = pl.program_id(1)
    lt = q_ref.shape[0]

    @pl.when(c == 0)
    def _():
        c_scr[...] = jnp.zeros_like(c_scr)
        m_scr[...] = jnp.zeros_like(m_scr)

    gt = gt_ref[...]
    lf = _log_sigmoid(gt)
    bcs = _seg_cumsum(lf, lt)
    causal, last = _mlstm_masks(lt, lt)
    for hd in range(M_HEADS):
        m_col = jnp.broadcast_to(m_scr[hd:hd + 1, 0:1], (lt, 1))
        h, (c_new,), (m_new,) = _mlstm_tile(
            q_ref[:, hd * M_DQK:(hd + 1) * M_DQK],
            kt_ref[hd * M_DQK:(hd + 1) * M_DQK, :],
            v_ref[:, hd * M_DV:(hd + 1) * M_DV],
            gt[hd:hd + 1], lf[M_HEADS + hd:M_HEADS + hd + 1], bcs[M_HEADS + hd:M_HEADS + hd + 1],
            m_col, [c_scr[hd]], causal, last, lt)
        c_scr[hd] = c_new
        m_scr[hd:hd + 1, :] = jnp.broadcast_to(m_new, (1, m_scr.shape[1]))
        h_ref[:, hd * M_DV:(hd + 1) * M_DV] = _mlstm_finish(
            h, o_ref[:, hd * M_DV:(hd + 1) * M_DV], nw_ref[:, hd * M_DV:(hd + 1) * M_DV])

    @pl.when(c == pl.num_programs(1) - 1)
    def _():
        c_out[0] = c_scr[:, :, :M_DV]
        n_out[0] = c_scr[:, :, M_DV:M_DV + 1]
        m_out[0] = m_scr[...]


def _mlstm_prompt(proj, projt, nw, batch, seq, n_rows):
    lt = _pick(seq, (MLSTM_CHUNK, 128))
    nc = seq // lt
    gate_blk = ROW_GATE // GATE_ROWS
    return pl.pallas_call(
        _mlstm_prompt_kernel,
        grid=(batch, nc),
        in_specs=[
            pl.BlockSpec((lt, M_HEADS * M_DQK), lambda b, c: (b * nc + c, COL_MQ // (M_HEADS * M_DQK))),
            pl.BlockSpec((M_HEADS * M_DQK, lt), lambda b, c: (0, b * nc + c)),
            pl.BlockSpec((lt, D_MODEL), lambda b, c: (b * nc + c, COL_MV // D_MODEL)),
            pl.BlockSpec((lt, D_MODEL), lambda b, c: (b * nc + c, COL_MO // D_MODEL)),
            pl.BlockSpec((GATE_ROWS, lt), lambda b, c: (gate_blk, b * nc + c)),
            pl.BlockSpec((1, D_MODEL), lambda b, c: (0, 0)),
        ],
        out_specs=[
            pl.BlockSpec((lt, D_MODEL), lambda b, c: (b * nc + c, 0)),
            pl.BlockSpec((1, M_HEADS, M_DQK, M_DV), lambda b, c: (b, 0, 0, 0)),
            pl.BlockSpec((1, M_HEADS, M_DQK, 1), lambda b, c: (b, 0, 0, 0)),
            pl.BlockSpec((1, 8, 128), lambda b, c: (b, 0, 0)),
        ],
        out_shape=[
            jax.ShapeDtypeStruct((n_rows, D_MODEL), F32),
            jax.ShapeDtypeStruct((batch, M_HEADS, M_DQK, M_DV), F32),
            jax.ShapeDtypeStruct((batch, M_HEADS, M_DQK, 1), F32),
            jax.ShapeDtypeStruct((batch, 8, 128), F32),
        ],
        scratch_shapes=[pltpu.VMEM((M_HEADS, M_DQK, C_EXT), F32), pltpu.VMEM((8, 128), F32)],
        compiler_params=pltpu.CompilerParams(
            dimension_semantics=("parallel", "arbitrary"), vmem_limit_bytes=VMEM_LIMIT),
        name="mlstm_prompt",
    )(proj, projt, proj, proj, projt, nw)


def _mlstm_sample_kernel(q_ref, kt_ref, v_ref, o_ref, gt_ref, nw_ref, m_ref, c_in, n_in,
                         ba_in, sc_in, sn_in, sm_in, h_ref, c_out, n_out, m_out, *, seg):
    del ba_in, sc_in, sn_in, sm_in
    hd = pl.program_id(1)
    lt = q_ref.shape[0]
    nseg = lt // seg
    gt = gt_ref[...]
    lf = _log_sigmoid(gt)
    bcs = _seg_cumsum(lf, seg)
    causal, last = _mlstm_masks(lt, seg)
    row = lax.broadcasted_iota(jnp.int32, (GATE_ROWS, lt), 0)

    def pick(x, r):
        return jnp.sum(jnp.where(row == r, x, 0.0), axis=0, keepdims=True)

    lane0 = lax.broadcasted_iota(jnp.int32, (M_DQK, C_EXT - M_DV), 1) == 0
    states = [
        jnp.concatenate([c_in[0, i, 0], jnp.where(lane0, n_in[0, i, 0], 0.0)], axis=1)
        for i in range(nseg)]
    h, new_states, new_m = _mlstm_tile(
        q_ref[...], kt_ref[...], v_ref[...],
        pick(gt, hd), pick(lf, M_HEADS + hd), pick(bcs, M_HEADS + hd),
        m_ref[0, 0], states, causal, last, seg)
    for i in range(nseg):
        c_out[0, i, 0] = new_states[i][:, :M_DV]
        n_out[0, i, 0] = new_states[i][:, M_DV:M_DV + 1]
    m_out[0, 0] = jnp.concatenate(new_m, axis=0)
    h_ref[...] = _mlstm_finish(h, o_ref[...], nw_ref[...])


def _mlstm_sample(proj, projt, nw, m_tok, state_c, state_n, ba, s_c, s_n, s_m, layer, row0, dec_batch, seg):
    lt = SAMPLE_TILE
    nseg = lt // seg
    ntile = dec_batch // nseg
    rb = row0 // lt
    gate_blk = ROW_GATE // GATE_ROWS
    any_spec = pl.BlockSpec(memory_space=pl.ANY)
    c_spec = pl.BlockSpec((1, nseg, 1, M_DQK, M_DV), lambda t, h: (layer, t, h, 0, 0))
    n_spec = pl.BlockSpec((1, nseg, 1, M_DQK, 1), lambda t, h: (layer, t, h, 0, 0))
    return pl.pallas_call(
        functools.partial(_mlstm_sample_kernel, seg=seg),
        grid=(ntile, M_HEADS),
        in_specs=[
            pl.BlockSpec((lt, M_DQK), lambda t, h: (rb + t, COL_MQ // M_DQK + h)),
            pl.BlockSpec((M_DQK, lt), lambda t, h: (h, rb + t)),
            pl.BlockSpec((lt, M_DV), lambda t, h: (rb + t, COL_MV // M_DV + h)),
            pl.BlockSpec((lt, M_DV), lambda t, h: (rb + t, COL_MO // M_DV + h)),
            pl.BlockSpec((GATE_ROWS, lt), lambda t, h: (gate_blk, rb + t)),
            pl.BlockSpec((1, M_DV), lambda t, h: (0, h)),
            pl.BlockSpec((1, 1, lt, 1), lambda t, h: (layer, h, t, 0)),
            c_spec, n_spec, any_spec, any_spec, any_spec, any_spec,
        ],
        out_specs=[
            pl.BlockSpec((lt, M_DV), lambda t, h: (rb + t, h)),
            c_spec, n_spec,
            pl.BlockSpec((1, 1, nseg, 1), lambda t, h: (layer, h, t, 0)),
        ],
        out_shape=[
            jax.ShapeDtypeStruct(ba.shape, F32),
            jax.ShapeDtypeStruct(s_c.shape, F32),
            jax.ShapeDtypeStruct(s_n.shape, F32),
            jax.ShapeDtypeStruct(s_m.shape, F32),
        ],
        input_output_aliases={9: 0, 10: 1, 11: 2, 12: 3},
        compiler_params=pltpu.CompilerParams(
            dimension_semantics=("parallel", "arbitrary"), vmem_limit_bytes=VMEM_LIMIT),
        name="mlstm_sample",
    )(proj, projt, proj, proj, projt, nw, m_tok, state_c, state_n, ba, s_c, s_n, s_m)


def _swa_tile(q, kb, vb, sink_ref, valid_prev):
    tq = q.shape[0]
    rows = A_GROUP * tq
    t_idx = lax.broadcasted_iota(jnp.int32, (rows, 1), 0) & (tq - 1)
    c_idx = lax.broadcasted_iota(jnp.int32, (1, 2 * WINDOW), 1)
    mask = (c_idx > t_idx) & (c_idx <= t_idx + WINDOW) & ((c_idx >= WINDOW) | valid_prev)
    outs = []
    for kv in range(A_KV):
        q4 = jnp.concatenate(
            [q[:, (kv * A_GROUP + g) * A_HD:(kv * A_GROUP + g + 1) * A_HD] for g in range(A_GROUP)],
            axis=0).astype(BF16)
        sink = jnp.concatenate(
            [jnp.full((tq, 1), sink_ref[kv * A_GROUP + g], F32) for g in range(A_GROUP)], axis=0)
        k = kb[:, kv * A_HD:(kv + 1) * A_HD].astype(BF16)
        v = vb[:, kv * A_HD:(kv + 1) * A_HD].astype(BF16)
        s = _dot_nt(q4, k) * (A_HD ** -0.5)
        s = jnp.where(mask, s, -jnp.inf)
        mx = jnp.maximum(jnp.max(s, axis=1, keepdims=True), sink)
        p = jnp.exp(s - mx)
        den = jnp.sum(p, axis=1, keepdims=True) + jnp.exp(sink - mx)
        o = _dot((p * (1.0 / den)).astype(BF16), v)
        outs.extend(o[g * tq:(g + 1) * tq] for g in range(A_GROUP))
    return jnp.concatenate(outs, axis=1)


def _swa_prompt_kernel(sink_ref, q_ref, kp_ref, kc_ref, vp_ref, vc_ref, o_ref):
    kb = jnp.concatenate([kp_ref[...], kc_ref[...]], axis=0)
    vb = jnp.concatenate([vp_ref[...], vc_ref[...]], axis=0)
    o_ref[...] = _swa_tile(q_ref[...], kb, vb, sink_ref, pl.program_id(1) > 0)


def _swa_prompt(proj, sinks, batch, seq, n_rows):
    nb = seq // WINDOW
    kvw = A_KV * A_HD
    cur = lambda col: (lambda b, i: (b * nb + i, col))
    prev = lambda col: (lambda b, i: (b * nb + jnp.maximum(i - 1, 0), col))
    return pl.pallas_call(
        _swa_prompt_kernel,
        grid=(batch, nb),
        in_specs=[
            pl.BlockSpec(memory_space=pltpu.SMEM),
            pl.BlockSpec((WINDOW, D_MODEL), cur(COL_AQ // D_MODEL)),
            pl.BlockSpec((WINDOW, kvw), prev(COL_AK // kvw)),
            pl.BlockSpec((WINDOW, kvw), cur(COL_AK // kvw)),
            pl.BlockSpec((WINDOW, kvw), prev(COL_AV // kvw)),
            pl.BlockSpec((WINDOW, kvw), cur(COL_AV // kvw)),
        ],
        out_specs=pl.BlockSpec((WINDOW, D_MODEL), lambda b, i: (b * nb + i, 0)),
        out_shape=jax.ShapeDtypeStruct((n_rows, D_MODEL), F32),
        compiler_params=pltpu.CompilerParams(
            dimension_semantics=("parallel", "parallel"), vmem_limit_bytes=VMEM_LIMIT),
        name="swa_prompt",
    )(sinks, proj, proj, proj, proj, proj)


def _swa_sample_kernel(sink_ref, q_ref, kn_ref, vn_ref, ck_ref, cv_ref, bb_in, sk_in, sv_in,
                       o_ref, sk_ref, sv_ref, *, tq):
    del bb_in, sk_in, sv_in
    nb = ck_ref.shape[1]
    wc = ck_ref.shape[2]
    kvw = A_KV * A_HD
    pad = jnp.zeros((WINDOW - tq, kvw), F32)

    def body(b, carry):
        rows = pl.ds(pl.multiple_of(b * tq, tq), tq)
        kn = kn_ref[rows, :]
        vn = vn_ref[rows, :]
        ck = ck_ref[0, b]
        cv = cv_ref[0, b]
        kb = jnp.concatenate([ck, kn, pad], axis=0)
        vb = jnp.concatenate([cv, vn, pad], axis=0)
        o_ref[rows, :] = _swa_tile(q_ref[rows, :], kb, vb, sink_ref, True)
        sk_ref[0, b, 0:wc - tq, :] = ck[tq:]
        sk_ref[0, b, wc - tq:wc, :] = kn
        sv_ref[0, b, 0:wc - tq, :] = cv[tq:]
        sv_ref[0, b, wc - tq:wc, :] = vn
        return carry

    lax.fori_loop(0, nb, body, 0)


def _swa_sample(proj, sinks, cache_k, cache_v, bb, s_k, s_v, layer, row0, dec_batch, tq):
    nb = SAMPLE_TILE // tq
    lt = SAMPLE_TILE
    rb = row0 // lt
    kvw = A_KV * A_HD
    wc = cache_k.shape[2]
    any_spec = pl.BlockSpec(memory_space=pl.ANY)
    cache_spec = pl.BlockSpec((1, nb, wc, kvw), lambda i: (layer, i, 0, 0))
    return pl.pallas_call(
        functools.partial(_swa_sample_kernel, tq=tq),
        grid=(dec_batch // nb,),
        in_specs=[
            pl.BlockSpec(memory_space=pltpu.SMEM),
            pl.BlockSpec((lt, D_MODEL), lambda i: (rb + i, COL_AQ // D_MODEL)),
            pl.BlockSpec((lt, kvw), lambda i: (rb + i, COL_AK // kvw)),
            pl.BlockSpec((lt, kvw), lambda i: (rb + i, COL_AV // kvw)),
            cache_spec, cache_spec, any_spec, any_spec, any_spec,
        ],
        out_specs=[
            pl.BlockSpec((lt, D_MODEL), lambda i: (rb + i, 0)),
            cache_spec, cache_spec,
        ],
        out_shape=[
            jax.ShapeDtypeStruct(bb.shape, F32),
            jax.ShapeDtypeStruct(s_k.shape, F32),
            jax.ShapeDtypeStruct(s_v.shape, F32),
        ],
        input_output_aliases={6: 0, 7: 1, 8: 2},
        compiler_params=pltpu.CompilerParams(
            dimension_semantics=("parallel",), vmem_limit_bytes=VMEM_LIMIT),
        name="swa_sample",
    )(sinks, proj, proj, proj, cache_k, cache_v, bb, s_k, s_v)


def _merge_kernel(x_ref, ba_ref, bb_ref, ga_ref, gb_ref, w_ref, o_ref):
    merged = _sigmoid(ga_ref[...]) * ba_ref[...] + _sigmoid(gb_ref[...]) * bb_ref[...]
    o_ref[...] = x_ref[...] + _dot(merged.astype(BF16), w_ref[...])


def _merge(x, ba, bb, proj, w_out):
    n = x.shape[0]
    tm = _pick(n, (512, 256, 128))
    row = lambda col: pl.BlockSpec((tm, D_MODEL), lambda i: (i, col))
    return pl.pallas_call(
        _merge_kernel,
        grid=(n // tm,),
        in_specs=[row(0), row(0), row(0), row(COL_GA // D_MODEL), row(COL_GB // D_MODEL),
                  pl.BlockSpec((D_MODEL, D_MODEL), lambda i: (0, 0))],
        out_specs=row(0),
        out_shape=jax.ShapeDtypeStruct((n, D_MODEL), F32),
        compiler_params=pltpu.CompilerParams(
            dimension_semantics=("parallel",), vmem_limit_bytes=VMEM_LIMIT),
        name="merge_out_proj",
    )(x, ba, bb, proj, proj, w_out)


def _mlp_kernel(x_ref, g_ref, wu_ref, wd_ref, o_ref, xn_ref):
    @pl.when(pl.program_id(1) == 0)
    def _():
        x = x_ref[...]
        xn_ref[...] = _rms(x, g_ref[...]).astype(BF16)
        o_ref[...] = x

    h = jnp.square(jnp.maximum(_dot(xn_ref[...], wu_ref[...]), 0.0))
    o_ref[...] += _dot(h.astype(BF16), wd_ref[...])


def _mlp(x, g, w_up, w_down):
    n = x.shape[0]
    tm = _pick(n, (1024, 512, 256, 128))
    tf = 1024
    return pl.pallas_call(
        _mlp_kernel,
        grid=(n // tm, D_FF // tf),
        in_specs=[
            pl.BlockSpec((tm, D_MODEL), lambda i, j: (i, 0)),
            pl.BlockSpec((1, D_MODEL), lambda i, j: (0, 0)),
            pl.BlockSpec((D_MODEL, tf), lambda i, j: (0, j)),
            pl.BlockSpec((tf, D_MODEL), lambda i, j: (j, 0)),
        ],
        out_specs=pl.BlockSpec((tm, D_MODEL), lambda i, j: (i, 0)),
        out_shape=jax.ShapeDtypeStruct((n, D_MODEL), F32),
        scratch_shapes=[pltpu.VMEM((tm, D_MODEL), BF16)],
        compiler_params=pltpu.CompilerParams(
            dimension_semantics=("parallel", "arbitrary"), vmem_limit_bytes=VMEM_LIMIT),
        name="mlp",
    )(x, g, w_up, w_down)


def _norm_kernel(x_ref, g_ref, o_ref):
    o_ref[...] = _rms(x_ref[...], g_ref[...])


def _final_norm(x, g, row0, rows):
    tm = _pick(rows, (1024, 512, 256, 128))
    rb = row0 // tm
    return pl.pallas_call(
        _norm_kernel,
        grid=(rows // tm,),
        in_specs=[pl.BlockSpec((tm, D_MODEL), lambda i: (rb + i, 0)),
                  pl.BlockSpec((1, D_MODEL), lambda i: (0, 0))],
        out_specs=pl.BlockSpec((tm, D_MODEL), lambda i: (i, 0)),
        out_shape=jax.ShapeDtypeStruct((rows, D_MODEL), F32),
        compiler_params=pltpu.CompilerParams(dimension_semantics=("parallel",)),
        name="final_norm",
    )(x, g)


def _prep_in_proj_weights(w_in, b_gate):
    sizes = (M_HEADS * M_DQK, M_HEADS * M_DQK, D_MODEL, D_MODEL, M_HEADS, M_HEADS,
             D_MODEL, A_KV * A_HD, A_KV * A_HD, D_MODEL, D_MODEL)
    offs = [0]
    for s in sizes:
        offs.append(offs[-1] + s)
    mq, mk, mv, mo, mi, mf, aq, ak, av, ga, gb = (w_in[:, :, offs[i]:offs[i + 1]] for i in range(11))
    depth = w_in.shape[0]
    w = jnp.concatenate([mv, mo, aq, ga, gb, mq, ak, av], axis=2).astype(BF16)
    wt_cols = jnp.concatenate([mk, mi, mf], axis=2)
    wt = jnp.swapaxes(wt_cols, 1, 2)
    wt = jnp.pad(wt, ((0, 0), (0, D_PROJT - wt.shape[1]), (0, 0))).astype(BF16)
    bt = jnp.zeros((depth, D_PROJT, 1), F32).at[:, ROW_GATE:ROW_GATE + 2 * M_HEADS, 0].set(b_gate.astype(F32))
    scale = jnp.ones((1, D_PROJ), F32).at[:, COL_MQ:COL_MQ + M_HEADS * M_DQK].set(M_DQK ** -0.5)
    return w, wt, bt, scale


def kernel(x_prompt, x_sample, state_C, state_n, state_m, cache_k, cache_v, norm_attn, w_in, b_gate,
           mlstm_norm, sinks, w_out, norm_mlp, w_up, w_down, norm_final):
    batch, seq, _ = x_prompt.shape
    dec_batch, dec_seq, _ = x_sample.shape
    depth = w_in.shape[0]
    wc = cache_k.shape[2]
    n_prompt = batch * seq
    n_rows = n_prompt + dec_batch * dec_seq
    kvw = A_KV * A_HD

    w_proj, w_projt, b_projt, scale = _prep_in_proj_weights(w_in, b_gate)
    w_out_b = w_out.astype(BF16)
    w_up_b = w_up.astype(BF16)
    w_down_b = w_down.astype(BF16)

    x = jnp.concatenate([x_prompt.reshape(n_prompt, D_MODEL), x_sample.reshape(-1, D_MODEL)], axis=0)
    state_n5 = state_n.reshape(depth, dec_batch, M_HEADS, M_DQK, 1)
    m_tok = jnp.repeat(jnp.swapaxes(state_m, 1, 2), dec_seq, axis=2)[..., None]
    cache_k4 = cache_k.reshape(depth, dec_batch, wc, kvw)
    cache_v4 = cache_v.reshape(depth, dec_batch, wc, kvw)

    s_c = jnp.zeros(state_C.shape, F32)
    s_n = jnp.zeros(state_n5.shape, F32)
    s_m = jnp.zeros((depth, M_HEADS, dec_batch, 1), F32)
    s_k = jnp.zeros(cache_k4.shape, F32)
    s_v = jnp.zeros(cache_v4.shape, F32)
    p_c, p_n, p_m, p_k, p_v = [], [], [], [], []

    for l in range(depth):
        proj, projt = _in_proj(x, norm_attn[l][None], w_proj[l], w_projt[l], b_projt[l], scale)
        nw = mlstm_norm[l][None]
        ba, pc, pn, pm = _mlstm_prompt(proj, projt, nw, batch, seq, n_rows)
        ba, s_c, s_n, s_m = _mlstm_sample(proj, projt, nw, m_tok, state_C, state_n5, ba, s_c, s_n, s_m,
                                          l, n_prompt, dec_batch, dec_seq)
        bb = _swa_prompt(proj, sinks[l], batch, seq, n_rows)
        bb, s_k, s_v = _swa_sample(proj, sinks[l], cache_k4, cache_v4, bb, s_k, s_v,
                                   l, n_prompt, dec_batch, dec_seq)
        x = _merge(x, ba, bb, proj, w_out_b[l])
        x = _mlp(x, norm_mlp[l][None], w_up_b[l], w_down_b[l])
        p_c.append(pc)
        p_n.append(pn[..., 0])
        p_m.append(pm[:, :M_HEADS, 0])
        kv_rows = proj[:n_prompt, COL_AK:COL_AK + 2 * kvw].reshape(batch, seq, 2 * kvw)[:, seq - wc:]
        p_k.append(kv_rows[..., :kvw].reshape(batch, wc, A_KV, A_HD))
        p_v.append(kv_rows[..., kvw:].reshape(batch, wc, A_KV, A_HD))

    g_final = norm_final[None]
    y_prompt = _final_norm(x, g_final, 0, n_prompt).reshape(batch, seq, D_MODEL)
    y_sample = _final_norm(x, g_final, n_prompt, n_rows - n_prompt).reshape(dec_batch, dec_seq, D_MODEL)
    return (y_prompt, y_sample,
            jnp.stack(p_c), jnp.stack(p_n), jnp.stack(p_m), jnp.stack(p_k), jnp.stack(p_v),
            s_c, s_n[..., 0], jnp.swapaxes(s_m[..., 0], 1, 2),
            s_k.reshape(cache_k.shape), s_v.reshape(cache_v.shape))
```

```python
import functools

import jax
import jax.numpy as jnp
from jax import lax
from jax.experimental import pallas as pl
from jax.experimental.pallas import tpu as pltpu

F32 = jnp.float32
BF16 = jnp.bfloat16

D_MODEL = 1024
M_HEADS = 4
M_DQK = 128
M_DV = D_MODEL // M_HEADS
A_HEADS = 16
A_KV = 4
A_GROUP = A_HEADS // A_KV
A_HD = D_MODEL // A_HEADS
KV_W = A_KV * A_HD
WINDOW = 128
D_FF = 4 * D_MODEL
EPS = 1e-6
LOG2E = 1.4426950408889634

COL_MV = 0
COL_MO = 1024
COL_AQ = 2048
COL_GA = 3072
COL_GB = 4096
COL_MQ = 5120
COL_AK = 5632
COL_AV = 5888
D_PROJ = 6144
ROW_KT = 0
ROW_GATE = M_HEADS * M_DQK
GATE_ROWS = 16
D_PROJT = ROW_GATE + GATE_ROWS

MLSTM_CHUNK = 256
SAMPLE_TILE = 128
SWA_STEP = 512
VMEM_LIMIT = 56 * 1024 * 1024


def _pick(n, candidates):
    for c in candidates:
        if n % c == 0:
            return c
    raise ValueError(f"no block size for {n}")


def _sigmoid(x):
    return 1.0 / (1.0 + jnp.exp(-x))


def _log_sigmoid(x):
    return jnp.minimum(x, 0.0) - jnp.log1p(jnp.exp(-jnp.abs(x)))


def _rms(x, g):
    y = x * lax.rsqrt(jnp.mean(x * x, axis=-1, keepdims=True) + EPS)
    return y * g


def _dot(a, b):
    return jnp.dot(a, b, preferred_element_type=F32)


def _dot_nt(a, b):
    return lax.dot_general(a, b, (((1,), (1,)), ((), ())), preferred_element_type=F32)


def _params(*sem):
    return pltpu.CompilerParams(dimension_semantics=sem, vmem_limit_bytes=VMEM_LIMIT)


def _in_proj_kernel(x_ref, g_ref, w_ref, wt_ref, bt_ref, scale_ref, proj_ref, projt_ref, xn_ref):
    @pl.when(pl.program_id(1) == 0)
    def _():
        xn = _rms(x_ref[...], g_ref[...]).astype(BF16)
        xn_ref[...] = xn
        projt_ref[...] = _dot_nt(wt_ref[...], xn) + bt_ref[...]

    proj_ref[...] = _dot_nt(xn_ref[...], w_ref[...]) * scale_ref[...]


def _in_proj(x, g, w, wt, bt, scale):
    n = x.shape[0]
    tm = _pick(n, (1024, 512, 256, 128))
    tn = 1024
    return pl.pallas_call(
        _in_proj_kernel,
        grid=(n // tm, D_PROJ // tn),
        in_specs=[
            pl.BlockSpec((tm, D_MODEL), lambda i, j: (i, 0)),
            pl.BlockSpec((1, D_MODEL), lambda i, j: (0, 0)),
            pl.BlockSpec((tn, D_MODEL), lambda i, j: (j, 0)),
            pl.BlockSpec((D_PROJT, D_MODEL), lambda i, j: (0, 0)),
            pl.BlockSpec((D_PROJT, 1), lambda i, j: (0, 0)),
            pl.BlockSpec((1, tn), lambda i, j: (0, j)),
        ],
        out_specs=[
            pl.BlockSpec((tm, tn), lambda i, j: (i, j)),
            pl.BlockSpec((D_PROJT, tm), lambda i, j: (0, i)),
        ],
        out_shape=[
            jax.ShapeDtypeStruct((n, D_PROJ), F32),
            jax.ShapeDtypeStruct((D_PROJT, n), F32),
        ],
        scratch_shapes=[pltpu.VMEM((tm, D_MODEL), BF16)],
        compiler_params=_params("parallel", "arbitrary"),
        name="in_proj",
    )(x, g, w, wt, bt, scale)


def _seg_cumsum(x, seg):
    pos = lax.broadcasted_iota(jnp.int32, x.shape, 1) & (seg - 1)
    sh = 1
    while sh < seg:
        x = x + jnp.where(pos >= sh, pltpu.roll(x, sh, axis=1), 0.0)
        sh *= 2
    return x


def _mlstm_masks(lt, seg):
    t = lax.broadcasted_iota(jnp.int32, (lt, lt), 0)
    s = lax.broadcasted_iota(jnp.int32, (lt, lt), 1)
    causal = (s <= t) & ((s | (seg - 1)) == (t | (seg - 1)))
    last = t == (s | (seg - 1))
    return causal, last


def _mlstm_tile(q, kt, v, i_row, lf_row, b_row, m_col, c_states, n_tok, causal, last, seg):
    lt = q.shape[0]
    nseg = lt // seg
    a_row = i_row - b_row
    a_mat = jnp.where(causal, a_row, -jnp.inf)
    g_col = jnp.maximum(jnp.max(a_mat, axis=1, keepdims=True), m_col)
    b_col = jnp.sum(jnp.where(causal, lf_row, 0.0), axis=1, keepdims=True)
    w = jnp.exp(a_mat - g_col)
    w_int = jnp.exp(m_col - g_col)
    qb = q.astype(BF16)
    ktb = kt.astype(BF16)
    vb = v.astype(BF16)
    s = _dot(qb, ktb) * w
    inter = jnp.concatenate(
        [_dot(qb[i * seg:(i + 1) * seg], c_states[i].astype(BF16)) for i in range(nseg)], axis=0)
    qn = jnp.sum(q * n_tok, axis=1, keepdims=True)
    num = w_int * inter + _dot(s.astype(BF16), vb)
    den = w_int * qn + jnp.sum(s, axis=1, keepdims=True)
    m_row = b_col + g_col
    h = num * (1.0 / jnp.maximum(jnp.abs(den), jnp.exp(-m_row)))

    wk_row = jnp.sum(jnp.where(last, w, 0.0), axis=0, keepdims=True)
    ktw = kt * wk_row
    if nseg == 1:
        c_upd = _dot(ktw.astype(BF16), vb)
        n_upd = _dot_nt(jnp.broadcast_to(wk_row, (8, lt)).astype(BF16), ktb)
    else:
        lane_seg = lax.broadcasted_iota(jnp.int32, (1, lt), 1) | (seg - 1)
        lhs = jnp.concatenate(
            [jnp.where(lane_seg == i * seg + seg - 1, ktw, 0.0).astype(BF16) for i in range(nseg)],
            axis=0)
        c_upd = _dot(lhs, vb)
        seg_row = lax.broadcasted_iota(jnp.int32, (lt, 1), 0) * seg + (seg - 1)
        n_upd = _dot_nt(jnp.where(lane_seg == seg_row, wk_row, 0.0).astype(BF16), ktb)
    return h, c_upd, n_upd, w_int, m_row


def _mlstm_finish(h, o_pre, nw):
    hn = h * lax.rsqrt(jnp.mean(h * h, axis=-1, keepdims=True) + EPS) * nw
    return _sigmoid(o_pre) * hn


def _mlstm_prompt_kernel(q_ref, kt_ref, v_ref, o_ref, gt_ref, nw_ref,
                         h_ref, c_out, n_out, m_out, c_scr, n_scr, m_scr):
    c = pl.program_id(1)
    lt = q_ref.shape[0]

    @pl.when(c == 0)
    def _():
        c_scr[...] = jnp.zeros_like(c_scr)
        n_scr[...] = jnp.zeros_like(n_scr)
        m_scr[...] = jnp.zeros_like(m_scr)

    gt = gt_ref[...]
    lf = _log_sigmoid(gt)
    bcs = _seg_cumsum(lf, lt)
    causal, last = _mlstm_masks(lt, lt)
    for hd in range(M_HEADS):
        m_col = jnp.broadcast_to(m_scr[hd:hd + 1, 0:1], (lt, 1))
        n_old = n_scr[hd:hd + 1, :]
        h, c_upd, n_upd, w_int, m_row = _mlstm_tile(
            q_ref[:, hd * M_DQK:(hd + 1) * M_DQK],
            kt_ref[hd * M_DQK:(hd + 1) * M_DQK, :],
            v_ref[:, hd * M_DV:(hd + 1) * M_DV],
            gt[hd:hd + 1], lf[M_HEADS + hd:M_HEADS + hd + 1], bcs[M_HEADS + hd:M_HEADS + hd + 1],
            m_col, [c_scr[hd]], jnp.broadcast_to(n_old, (lt, M_DQK)), causal, last, lt)
        decay = w_int[lt - 1:lt]
        c_scr[hd] = decay * c_scr[hd] + c_upd
        n_scr[hd:hd + 1, :] = decay * n_old + n_upd[0:1]
        m_scr[hd:hd + 1, :] = jnp.broadcast_to(m_row[lt - 1:lt], (1, m_scr.shape[1]))
        h_ref[:, hd * M_DV:(hd + 1) * M_DV] = _mlstm_finish(
            h, o_ref[:, hd * M_DV:(hd + 1) * M_DV], nw_ref[:, hd * M_DV:(hd + 1) * M_DV])

    @pl.when(c == pl.num_programs(1) - 1)
    def _():
        c_out[0] = c_scr[...]
        n_out[0] = n_scr[0:M_HEADS]
        m_out[0] = m_scr[...]


def _mlstm_prompt(proj, projt, nw, batch, seq):
    lt = _pick(seq, (MLSTM_CHUNK, 128))
    nc = seq // lt
    gate_blk = ROW_GATE // GATE_ROWS
    return pl.pallas_call(
        _mlstm_prompt_kernel,
        grid=(batch, nc),
        in_specs=[
            pl.BlockSpec((lt, M_HEADS * M_DQK), lambda b, c: (b * nc + c, COL_MQ // (M_HEADS * M_DQK))),
            pl.BlockSpec((M_HEADS * M_DQK, lt), lambda b, c: (0, b * nc + c)),
            pl.BlockSpec((lt, D_MODEL), lambda b, c: (b * nc + c, COL_MV // D_MODEL)),
            pl.BlockSpec((lt, D_MODEL), lambda b, c: (b * nc + c, COL_MO // D_MODEL)),
            pl.BlockSpec((GATE_ROWS, lt), lambda b, c: (gate_blk, b * nc + c)),
            pl.BlockSpec((1, D_MODEL), lambda b, c: (0, 0)),
        ],
        out_specs=[
            pl.BlockSpec((lt, D_MODEL), lambda b, c: (b * nc + c, 0)),
            pl.BlockSpec((1, M_HEADS, M_DQK, M_DV), lambda b, c: (b, 0, 0, 0)),
            pl.BlockSpec((1, M_HEADS, M_DQK), lambda b, c: (b, 0, 0)),
            pl.BlockSpec((1, 8, 128), lambda b, c: (b, 0, 0)),
        ],
        out_shape=[
            jax.ShapeDtypeStruct((batch * seq, D_MODEL), F32),
            jax.ShapeDtypeStruct((batch, M_HEADS, M_DQK, M_DV), F32),
            jax.ShapeDtypeStruct((batch, M_HEADS, M_DQK), F32),
            jax.ShapeDtypeStruct((batch, 8, 128), F32),
        ],
        scratch_shapes=[pltpu.VMEM((M_HEADS, M_DQK, M_DV), F32), pltpu.VMEM((8, M_DQK), F32),
                        pltpu.VMEM((8, 128), F32)],
        compiler_params=_params("parallel", "arbitrary"),
        name="mlstm_prompt",
    )(proj, projt, proj, proj, projt, nw)


def _mlstm_sample_kernel(q_ref, kt_ref, v_ref, o_ref, gt_ref, nw_ref, m_ref, c_in, n_in,
                         sc_in, sn_in, sm_in, h_ref, c_out, n_out, m_out, *, seg):
    del sc_in, sn_in, sm_in
    hd = pl.program_id(1)
    lt = q_ref.shape[0]
    nseg = lt // seg
    gt = gt_ref[...]
    lf = _log_sigmoid(gt)
    bcs = _seg_cumsum(lf, seg)
    causal, last = _mlstm_masks(lt, seg)
    row = lax.broadcasted_iota(jnp.int32, (GATE_ROWS, lt), 0)

    def pick(x, r):
        return jnp.sum(jnp.where(row == r, x, 0.0), axis=0, keepdims=True)

    n_old = [n_in[0, i, pl.ds(hd, 1), :] for i in range(nseg)]
    n_tok = jnp.concatenate([jnp.broadcast_to(n, (seg, M_DQK)) for n in n_old], axis=0)
    h, c_upd, n_upd, w_int, m_row = _mlstm_tile(
        q_ref[...], kt_ref[...], v_ref[...],
        pick(gt, hd), pick(lf, M_HEADS + hd), pick(bcs, M_HEADS + hd),
        m_ref[0, 0], [c_in[0, i, 0] for i in range(nseg)], n_tok, causal, last, seg)
    new_m = []
    for i in range(nseg):
        r = i * seg + seg - 1
        decay = w_int[r:r + 1]
        c_out[0, i, 0] = decay * c_in[0, i, 0] + c_upd[i * M_DQK:(i + 1) * M_DQK]
        n_out[0, i, pl.ds(hd, 1), :] = decay * n_old[i] + n_upd[i:i + 1]
        new_m.append(m_row[r:r + 1])
    m_out[0, 0] = jnp.concatenate(new_m, axis=0)
    h_ref[...] = _mlstm_finish(h, o_ref[...], nw_ref[...])


def _mlstm_sample(proj, projt, nw, m_tok, state_c, state_n, s_c, s_n, s_m, layer, dec_batch, seg):
    lt = SAMPLE_TILE
    nseg = lt // seg
    gate_blk = ROW_GATE // GATE_ROWS
    any_spec = pl.BlockSpec(memory_space=pl.ANY)
    c_spec = pl.BlockSpec((1, nseg, 1, M_DQK, M_DV), lambda t, h: (layer, t, h, 0, 0))
    n_spec = pl.BlockSpec((1, nseg, M_HEADS, M_DQK), lambda t, h: (layer, t, 0, 0))
    return pl.pallas_call(
        functools.partial(_mlstm_sample_kernel, seg=seg),
        grid=(dec_batch // nseg, M_HEADS),
        in_specs=[
            pl.BlockSpec((lt, M_DQK), lambda t, h: (t, COL_MQ // M_DQK + h)),
            pl.BlockSpec((M_DQK, lt), lambda t, h: (h, t)),
            pl.BlockSpec((lt, M_DV), lambda t, h: (t, COL_MV // M_DV + h)),
            pl.BlockSpec((lt, M_DV), lambda t, h: (t, COL_MO // M_DV + h)),
            pl.BlockSpec((GATE_ROWS, lt), lambda t, h: (gate_blk, t)),
            pl.BlockSpec((1, M_DV), lambda t, h: (0, h)),
            pl.BlockSpec((1, 1, lt, 1), lambda t, h: (layer, h, t, 0)),
            c_spec, n_spec, any_spec, any_spec, any_spec,
        ],
        out_specs=[
            pl.BlockSpec((lt, M_DV), lambda t, h: (t, h)),
            c_spec, n_spec,
            pl.BlockSpec((1, 1, nseg, 1), lambda t, h: (layer, h, t, 0)),
        ],
        out_shape=[
            jax.ShapeDtypeStruct((dec_batch * seg, D_MODEL), F32),
            jax.ShapeDtypeStruct(s_c.shape, F32),
            jax.ShapeDtypeStruct(s_n.shape, F32),
            jax.ShapeDtypeStruct(s_m.shape, F32),
        ],
        input_output_aliases={9: 1, 10: 2, 11: 3},
        compiler_params=_params("parallel", "arbitrary"),
        name="mlstm_sample",
    )(proj, projt, proj, proj, projt, nw, m_tok, state_c, state_n, s_c, s_n, s_m)


def _swa_prompt_kernel(sink_ref, q_ref, kp_ref, kc_ref, vp_ref, vc_ref, o_ref, pk_ref, pv_ref,
                       kb_scr, vt_scr):
    i = pl.program_id(1)
    nsub = q_ref.shape[0] // WINDOW
    lanes = A_GROUP * WINDOW
    kb_scr[0:WINDOW] = kp_ref[...].astype(BF16)
    kb_scr[WINDOW:] = kc_ref[...].astype(BF16)
    vt_scr[:, 0:WINDOW] = vp_ref[...].T.astype(BF16)
    vt_scr[:, WINDOW:] = vc_ref[...].T.astype(BF16)

    all_lanes = A_HEADS * WINDOW
    c_idx = lax.broadcasted_iota(jnp.int32, (2 * WINDOW, all_lanes), 0)
    t_idx = lax.broadcasted_iota(jnp.int32, (2 * WINDOW, all_lanes), 1) & (WINDOW - 1)
    band = (c_idx > t_idx) & (c_idx <= t_idx + WINDOW)
    bias = jnp.where(band, 0.0, -jnp.inf)
    bias_first = jnp.where(band & ((c_idx >= WINDOW) | (i > 0)), 0.0, -jnp.inf)
    lane_head = lax.broadcasted_iota(jnp.int32, (1, all_lanes), 1) >> (WINDOW.bit_length() - 1)
    sink = jnp.zeros((1, all_lanes), F32)
    for h in range(A_HEADS):
        sink = jnp.where(lane_head == h, sink_ref[h] * LOG2E, sink)
    zeros = jnp.zeros((A_HD, lanes), F32)
    ones = jnp.ones((16, 2 * WINDOW), BF16)

    for j in range(nsub):
        qt = q_ref[j * WINDOW:(j + 1) * WINDOW, :].T
        qt4 = [jnp.concatenate([qt[(kv * A_GROUP + g) * A_HD:(kv * A_GROUP + g + 1) * A_HD]
                                for g in range(A_GROUP)], axis=1) for kv in range(A_KV)]
        st = []
        for pair in range(A_KV // 2):
            rhs = jnp.concatenate([jnp.concatenate([qt4[2 * pair], zeros], axis=1),
                                   jnp.concatenate([zeros, qt4[2 * pair + 1]], axis=1)], axis=0)
            kpair = kb_scr[j * WINDOW:(j + 2) * WINDOW, pair * 128:(pair + 1) * 128]
            st.append(_dot(kpair, rhs.astype(BF16)))
        st = jnp.concatenate(st, axis=1) + (bias_first if j == 0 else bias)
        mx = jnp.maximum(jnp.max(st, axis=0, keepdims=True), sink)
        p = jnp.exp2(st - mx).astype(BF16)
        sink_p = jnp.exp2(sink - mx)
        pieces = []
        for kv in range(A_KV):
            vt = jnp.concatenate([vt_scr[kv * A_HD:(kv + 1) * A_HD, j * WINDOW:(j + 2) * WINDOW], ones], axis=0)
            ot = _dot(vt, p[:, kv * lanes:(kv + 1) * lanes])
            den = ot[A_HD:A_HD + 1] + sink_p[:, kv * lanes:(kv + 1) * lanes]
            ot = ot[:A_HD] * (1.0 / den)
            pieces.extend(ot[:, g * WINDOW:(g + 1) * WINDOW] for g in range(A_GROUP))
        o_ref[j * WINDOW:(j + 1) * WINDOW, :] = jnp.concatenate(pieces, axis=0).T

    @pl.when(i == pl.num_programs(1) - 1)
    def _():
        rows = kc_ref.shape[0]
        pk_ref[0] = kc_ref[rows - WINDOW:rows, :].T
        pv_ref[0] = vc_ref[rows - WINDOW:rows, :].T


def _swa_prompt(proj, sinks, batch, seq):
    tq = _pick(seq, (SWA_STEP, 256, 128))
    ns = seq // tq
    per = tq // WINDOW
    nb = seq // WINDOW
    cur = lambda col: (lambda b, i: (b * ns + i, col))
    prev = lambda col: (lambda b, i: (b * nb + jnp.maximum(i * per - 1, 0), col))
    return pl.pallas_call(
        _swa_prompt_kernel,
        grid=(batch, ns),
        in_specs=[
            pl.BlockSpec(memory_space=pltpu.SMEM),
            pl.BlockSpec((tq, D_MODEL), cur(COL_AQ // D_MODEL)),
            pl.BlockSpec((WINDOW, KV_W), prev(COL_AK // KV_W)),
            pl.BlockSpec((tq, KV_W), cur(COL_AK // KV_W)),
            pl.BlockSpec((WINDOW, KV_W), prev(COL_AV // KV_W)),
            pl.BlockSpec((tq, KV_W), cur(COL_AV // KV_W)),
        ],
        out_specs=[
            pl.BlockSpec((tq, D_MODEL), lambda b, i: (b * ns + i, 0)),
            pl.BlockSpec((1, KV_W, WINDOW), lambda b, i: (b, 0, 0)),
            pl.BlockSpec((1, KV_W, WINDOW), lambda b, i: (b, 0, 0)),
        ],
        out_shape=[
            jax.ShapeDtypeStruct((batch * seq, D_MODEL), F32),
            jax.ShapeDtypeStruct((batch, KV_W, WINDOW), F32),
            jax.ShapeDtypeStruct((batch, KV_W, WINDOW), F32),
        ],
        scratch_shapes=[pltpu.VMEM((WINDOW + tq, KV_W), BF16), pltpu.VMEM((KV_W, WINDOW + tq), BF16)],
        compiler_params=_params("parallel", "arbitrary"),
        name="swa_prompt",
    )(sinks, proj, proj, proj, proj, proj)


def _bdot(a, b, contract_b):
    return lax.dot_general(a, b, (((2,), (contract_b,)), ((0,), (0,))), preferred_element_type=F32)


def _swa_sample_kernel(sink_ref, q_ref, kn_ref, vn_ref, ck_ref, cv_ref, sk_in, sv_in,
                       o_ref, sk_ref, sv_ref, *, tq):
    del sk_in, sv_in
    nb = ck_ref.shape[1]
    wc = ck_ref.shape[3]
    keep = wc - tq
    q3 = q_ref[...].reshape(nb, tq, D_MODEL)
    ck = ck_ref[0]
    cv = cv_ref[0]

    lane = lax.broadcasted_iota(jnp.int32, (nb, KV_W, wc), 2)

    def appended(cache, new_rows):
        new_t = new_rows.T
        placed = jnp.stack([pltpu.roll(new_t, (keep - b * tq) % wc, axis=1) for b in range(nb)])
        return jnp.where(lane >= keep, placed, pltpu.roll(cache, keep, axis=2))

    sk = appended(ck, kn_ref[...])
    sv = appended(cv, vn_ref[...])
    sk_ref[0] = sk
    sv_ref[0] = sv

    rows = A_GROUP * tq
    t_idx = lax.broadcasted_iota(jnp.int32, (1, rows, 1), 1) & (tq - 1)
    c_idx = lax.broadcasted_iota(jnp.int32, (1, 1, wc), 2)
    mask_old = c_idx > t_idx
    mask_new = (c_idx >= keep) & (c_idx - keep <= t_idx)
    g_idx = lax.broadcasted_iota(jnp.int32, (1, rows, 1), 1) >> (tq.bit_length() - 1)
    pieces = []
    for kv in range(A_KV):
        heads = [kv * A_GROUP + g for g in range(A_GROUP)]
        q4 = jnp.concatenate([q3[:, :, h * A_HD:(h + 1) * A_HD] for h in heads], axis=1).astype(BF16)
        sink = jnp.zeros((1, rows, 1), F32)
        for g in range(A_GROUP):
            sink = jnp.where(g_idx == g, sink_ref[heads[g]] * LOG2E, sink)
        sl = slice(kv * A_HD, (kv + 1) * A_HD)
        s1 = jnp.where(mask_old, _bdot(q4, ck[:, sl, :].astype(BF16), 1), -jnp.inf)
        s2 = jnp.where(mask_new, _bdot(q4, sk[:, sl, :].astype(BF16), 1), -jnp.inf)
        mx = jnp.maximum(jnp.maximum(jnp.max(s1, axis=2, keepdims=True),
                                     jnp.max(s2, axis=2, keepdims=True)), sink)
        p1 = jnp.exp2(s1 - mx)
        p2 = jnp.exp2(s2 - mx)
        den = (jnp.sum(p1, axis=2, keepdims=True) + jnp.sum(p2, axis=2, keepdims=True)
               + jnp.exp2(sink - mx))
        r = 1.0 / den
        o = (_bdot((p1 * r).astype(BF16), cv[:, sl, :].astype(BF16), 2)
             + _bdot((p2 * r).astype(BF16), sv[:, sl, :].astype(BF16), 2))
        pieces.extend(o[:, g * tq:(g + 1) * tq, :] for g in range(A_GROUP))
    o_ref[...] = jnp.concatenate(pieces, axis=2).reshape(nb * tq, D_MODEL)


def _swa_sample(proj, sinks, cache_kt, cache_vt, s_k, s_v, layer, dec_batch, tq):
    lt = SAMPLE_TILE
    nb = lt // tq
    wc = cache_kt.shape[3]
    assert wc == lt, "the appended keys are placed with lane rolls over one cache row"
    any_spec = pl.BlockSpec(memory_space=pl.ANY)
    cache_spec = pl.BlockSpec((1, nb, KV_W, wc), lambda i: (layer, i, 0, 0))
    return pl.pallas_call(
        functools.partial(_swa_sample_kernel, tq=tq),
        grid=(dec_batch // nb,),
        in_specs=[
            pl.BlockSpec(memory_space=pltpu.SMEM),
            pl.BlockSpec((lt, D_MODEL), lambda i: (i, COL_AQ // D_MODEL)),
            pl.BlockSpec((lt, KV_W), lambda i: (i, COL_AK // KV_W)),
            pl.BlockSpec((lt, KV_W), lambda i: (i, COL_AV // KV_W)),
            cache_spec, cache_spec, any_spec, any_spec,
        ],
        out_specs=[
            pl.BlockSpec((lt, D_MODEL), lambda i: (i, 0)),
            cache_spec, cache_spec,
        ],
        out_shape=[
            jax.ShapeDtypeStruct((dec_batch * tq, D_MODEL), F32),
            jax.ShapeDtypeStruct(s_k.shape, F32),
            jax.ShapeDtypeStruct(s_v.shape, F32),
        ],
        input_output_aliases={6: 1, 7: 2},
        compiler_params=_params("parallel"),
        name="swa_sample",
    )(sinks, proj, proj, proj, cache_kt, cache_vt, s_k, s_v)


def _merge_kernel(x_ref, ba_ref, bb_ref, ga_ref, gb_ref, w_ref, o_ref):
    merged = _sigmoid(ga_ref[...]) * ba_ref[...] + _sigmoid(gb_ref[...]) * bb_ref[...]
    o_ref[...] = x_ref[...] + _dot(merged.astype(BF16), w_ref[...])


def _merge(x, ba, bb, proj, w_out):
    n = x.shape[0]
    tm = _pick(n, (512, 256, 128))
    row = lambda col: pl.BlockSpec((tm, D_MODEL), lambda i: (i, col))
    return pl.pallas_call(
        _merge_kernel,
        grid=(n // tm,),
        in_specs=[row(0), row(0), row(0), row(COL_GA // D_MODEL), row(COL_GB // D_MODEL),
                  pl.BlockSpec((D_MODEL, D_MODEL), lambda i: (0, 0))],
        out_specs=row(0),
        out_shape=jax.ShapeDtypeStruct((n, D_MODEL), F32),
        compiler_params=_params("parallel"),
        name="merge_out_proj",
    )(x, ba, bb, proj, proj, w_out)


def _mlp_kernel(x_ref, g_ref, wu_ref, wd_ref, o_ref, xn_ref):
    @pl.when(pl.program_id(1) == 0)
    def _():
        x = x_ref[...]
        xn_ref[...] = _rms(x, g_ref[...]).astype(BF16)
        o_ref[...] = x

    h = jnp.square(jnp.maximum(_dot(xn_ref[...], wu_ref[...]), 0.0))
    o_ref[...] += _dot(h.astype(BF16), wd_ref[...])


def _mlp(x, g, w_up, w_down):
    n = x.shape[0]
    tm = _pick(n, (1024, 512, 256, 128))
    tf = 1024
    return pl.pallas_call(
        _mlp_kernel,
        grid=(n // tm, D_FF // tf),
        in_specs=[
            pl.BlockSpec((tm, D_MODEL), lambda i, j: (i, 0)),
            pl.BlockSpec((1, D_MODEL), lambda i, j: (0, 0)),
            pl.BlockSpec((D_MODEL, tf), lambda i, j: (0, j)),
            pl.BlockSpec((tf, D_MODEL), lambda i, j: (j, 0)),
        ],
        out_specs=pl.BlockSpec((tm, D_MODEL), lambda i, j: (i, 0)),
        out_shape=jax.ShapeDtypeStruct((n, D_MODEL), F32),
        scratch_shapes=[pltpu.VMEM((tm, D_MODEL), BF16)],
        compiler_params=_params("parallel", "arbitrary"),
        name="mlp",
    )(x, g, w_up, w_down)


def _norm_kernel(x_ref, g_ref, o_ref):
    o_ref[...] = _rms(x_ref[...], g_ref[...])


def _final_norm(x, g):
    n = x.shape[0]
    tm = _pick(n, (1024, 512, 256, 128))
    return pl.pallas_call(
        _norm_kernel,
        grid=(n // tm,),
        in_specs=[pl.BlockSpec((tm, D_MODEL), lambda i: (i, 0)),
                  pl.BlockSpec((1, D_MODEL), lambda i: (0, 0))],
        out_specs=pl.BlockSpec((tm, D_MODEL), lambda i: (i, 0)),
        out_shape=jax.ShapeDtypeStruct((n, D_MODEL), F32),
        compiler_params=_params("parallel"),
        name="final_norm",
    )(x, g)


def _prep_in_proj_weights(w_in, b_gate):
    sizes = (M_HEADS * M_DQK, M_HEADS * M_DQK, D_MODEL, D_MODEL, M_HEADS, M_HEADS,
             D_MODEL, KV_W, KV_W, D_MODEL, D_MODEL)
    offs = [0]
    for s in sizes:
        offs.append(offs[-1] + s)
    w_t = jnp.swapaxes(w_in, 1, 2).astype(BF16)
    mq, mk, mv, mo, mi, mf, aq, ak, av, ga, gb = (w_t[:, offs[i]:offs[i + 1]] for i in range(11))
    depth = w_in.shape[0]
    w = jnp.concatenate([mv, mo, aq, ga, gb, mq, ak, av], axis=1)
    pad = jnp.zeros((depth, D_PROJT - ROW_GATE - 2 * M_HEADS, D_MODEL), BF16)
    wt = jnp.concatenate([mk, mi, mf, pad], axis=1)
    bt = jnp.zeros((depth, D_PROJT, 1), F32).at[:, ROW_GATE:ROW_GATE + 2 * M_HEADS, 0].set(b_gate.astype(F32))
    scale = jnp.ones((1, D_PROJ), F32)
    scale = scale.at[:, COL_MQ:COL_MQ + M_HEADS * M_DQK].set(M_DQK ** -0.5)
    scale = scale.at[:, COL_AQ:COL_AQ + D_MODEL].set(A_HD ** -0.5 * LOG2E)
    return w, wt, bt, scale


def kernel(x_prompt, x_sample, state_C, state_n, state_m, cache_k, cache_v, norm_attn, w_in, b_gate,
           mlstm_norm, sinks, w_out, norm_mlp, w_up, w_down, norm_final):
    batch, seq, _ = x_prompt.shape
    dec_batch, dec_seq, _ = x_sample.shape
    depth = w_in.shape[0]
    wc = cache_k.shape[2]

    w_proj, w_projt, b_projt, scale = _prep_in_proj_weights(w_in, b_gate)
    w_out_b = w_out.astype(BF16)
    w_up_b = w_up.astype(BF16)
    w_down_b = w_down.astype(BF16)

    xp = x_prompt.reshape(batch * seq, D_MODEL)
    xs = x_sample.reshape(dec_batch * dec_seq, D_MODEL)
    m_tok = jnp.repeat(jnp.swapaxes(state_m, 1, 2), dec_seq, axis=2)[..., None]
    cache_kt = jnp.transpose(cache_k, (0, 1, 3, 4, 2)).reshape(depth, dec_batch, KV_W, wc)
    cache_vt = jnp.transpose(cache_v, (0, 1, 3, 4, 2)).reshape(depth, dec_batch, KV_W, wc)

    s_c = jnp.zeros(state_C.shape, F32)
    s_n = jnp.zeros(state_n.shape, F32)
    s_m = jnp.zeros((depth, M_HEADS, dec_batch, 1), F32)
    s_k = jnp.zeros(cache_kt.shape, F32)
    s_v = jnp.zeros(cache_vt.shape, F32)
    p_c, p_n, p_m, p_k, p_v = [], [], [], [], []

    for l in range(depth):
        g_attn = norm_attn[l][None]
        nw = mlstm_norm[l][None]
        g_mlp = norm_mlp[l][None]

        proj, projt = _in_proj(xp, g_attn, w_proj[l], w_projt[l], b_projt[l], scale)
        ba, pc, pn, pm = _mlstm_prompt(proj, projt, nw, batch, seq)
        bb, pk, pv = _swa_prompt(proj, sinks[l], batch, seq)
        xp = _merge(xp, ba, bb, proj, w_out_b[l])
        xp = _mlp(xp, g_mlp, w_up_b[l], w_down_b[l])
        p_c.append(pc)
        p_n.append(pn)
        p_m.append(pm[:, :M_HEADS, 0])
        p_k.append(pk)
        p_v.append(pv)

        proj, projt = _in_proj(xs, g_attn, w_proj[l], w_projt[l], b_projt[l], scale)
        ba, s_c, s_n, s_m = _mlstm_sample(proj, projt, nw, m_tok, state_C, state_n, s_c, s_n, s_m,
                                          l, dec_batch, dec_seq)
        bb, s_k, s_v = _swa_sample(proj, sinks[l], cache_kt, cache_vt, s_k, s_v, l, dec_batch, dec_seq)
        xs = _merge(xs, ba, bb, proj, w_out_b[l])
        xs = _mlp(xs, g_mlp, w_up_b[l], w_down_b[l])

    g_final = norm_final[None]
    y_prompt = _final_norm(xp, g_final).reshape(batch, seq, D_MODEL)
    y_sample = _final_norm(xs, g_final).reshape(dec_batch, dec_seq, D_MODEL)

    def positions_major(t, lead):
        t = t.reshape(*lead, A_KV, A_HD, wc)
        return jnp.moveaxis(t, -1, -3)

    return (y_prompt, y_sample,
            jnp.stack(p_c), jnp.stack(p_n), jnp.stack(p_m),
            positions_major(jnp.stack(p_k), (depth, batch)), positions_major(jnp.stack(p_v), (depth, batch)),
            s_c, s_n, jnp.swapaxes(s_m[..., 0], 1, 2),
            positions_major(s_k, (depth, dec_batch)), positions_major(s_v, (depth, dec_batch)))
```

```python
import functools

import jax
import jax.numpy as jnp
from jax import lax
from jax.experimental import pallas as pl
from jax.experimental.pallas import tpu as pltpu

F32 = jnp.float32
BF16 = jnp.bfloat16

D_MODEL = 1024
M_HEADS = 4
M_DQK = 128
M_DV = D_MODEL // M_HEADS
MQ_W = M_HEADS * M_DQK
A_HEADS = 16
A_KV = 4
A_GROUP = A_HEADS // A_KV
A_HD = D_MODEL // A_HEADS
KV_W = A_KV * A_HD
WINDOW = 128
D_FF = 4 * D_MODEL
EPS = 1e-6
LOG2E = 1.4426950408889634

PROJ_BLOCK = 1024
D_PROJ = 3 * PROJ_BLOCK
GATE_ROWS = 16

MLSTM_CHUNK = 256
SAMPLE_TILE = 128
SWA_STEP = 512
VMEM_LIMIT = 56 * 1024 * 1024


def _pick(n, candidates):
    for c in candidates:
        if n % c == 0:
            return c
    raise ValueError(f"no block size for {n}")


def _sigmoid(x):
    return 1.0 / (1.0 + jnp.exp(-x))


def _log_sigmoid(x):
    return jnp.minimum(x, 0.0) - jnp.log1p(jnp.exp(-jnp.abs(x)))


def _rms(x, g):
    y = x * lax.rsqrt(jnp.mean(x * x, axis=-1, keepdims=True) + EPS)
    return y * g


def _dot(a, b):
    return jnp.dot(a, b, preferred_element_type=F32)


def _dot_nt(a, b):
    return lax.dot_general(a, b, (((1,), (1,)), ((), ())), preferred_element_type=F32)


def _params(*sem):
    return pltpu.CompilerParams(dimension_semantics=sem, vmem_limit_bytes=VMEM_LIMIT)


def _in_proj_kernel(x_ref, g_ref, w_ref, wk_ref, wg_ref, bg_ref, scale_ref,
                    pb_ref, pq_ref, pf_ref, kt_ref, gt_ref, xn_ref):
    j = pl.program_id(1)

    @pl.when(j == 0)
    def _():
        xn = _rms(x_ref[...], g_ref[...]).astype(BF16)
        xn_ref[...] = xn
        kt_ref[...] = _dot_nt(wk_ref[...], xn).astype(BF16)
        gt_ref[...] = _dot_nt(wg_ref[...], xn) + bg_ref[...]

    acc = _dot_nt(xn_ref[...], w_ref[...]) * scale_ref[...]

    @pl.when(j < 2)
    def _():
        pb_ref[...] = acc.astype(BF16)

    @pl.when(j == 2)
    def _():
        pq_ref[...] = acc[:, :MQ_W].astype(BF16)
        pf_ref[...] = acc[:, MQ_W:]


def _in_proj(x, g, w, wk, wg, bg, scale):
    n = x.shape[0]
    tm = _pick(n, (1024, 512, 256, 128))
    tn = PROJ_BLOCK
    const = lambda i, j: (0, 0)
    return pl.pallas_call(
        _in_proj_kernel,
        grid=(n // tm, D_PROJ // tn),
        in_specs=[
            pl.BlockSpec((tm, D_MODEL), lambda i, j: (i, 0)),
            pl.BlockSpec((1, D_MODEL), const),
            pl.BlockSpec((tn, D_MODEL), lambda i, j: (j, 0)),
            pl.BlockSpec((MQ_W, D_MODEL), const),
            pl.BlockSpec((GATE_ROWS, D_MODEL), const),
            pl.BlockSpec((GATE_ROWS, 1), const),
            pl.BlockSpec((1, tn), lambda i, j: (0, j)),
        ],
        out_specs=[
            pl.BlockSpec((tm, tn), lambda i, j: (i, jnp.minimum(j, 1))),
            pl.BlockSpec((tm, MQ_W), lambda i, j: (i, 0)),
            pl.BlockSpec((tm, 2 * KV_W), lambda i, j: (i, 0)),
            pl.BlockSpec((MQ_W, tm), lambda i, j: (0, i)),
            pl.BlockSpec((GATE_ROWS, tm), lambda i, j: (0, i)),
        ],
        out_shape=[
            jax.ShapeDtypeStruct((n, 2 * PROJ_BLOCK), BF16),
            jax.ShapeDtypeStruct((n, MQ_W), BF16),
            jax.ShapeDtypeStruct((n, 2 * KV_W), F32),
            jax.ShapeDtypeStruct((MQ_W, n), BF16),
            jax.ShapeDtypeStruct((GATE_ROWS, n), F32),
        ],
        scratch_shapes=[pltpu.VMEM((tm, D_MODEL), BF16)],
        compiler_params=_params("parallel", "arbitrary"),
        name="in_proj",
    )(x, g, w, wk, wg, bg, scale)


def _seg_cumsum(x, seg):
    pos = lax.broadcasted_iota(jnp.int32, x.shape, 1) & (seg - 1)
    sh = 1
    while sh < seg:
        x = x + jnp.where(pos >= sh, pltpu.roll(x, sh, axis=1), 0.0)
        sh *= 2
    return x


def _mlstm_finish(h, o_pre, nw):
    hn = h * lax.rsqrt(jnp.mean(h * h, axis=-1, keepdims=True) + EPS) * nw
    return _sigmoid(o_pre) * hn


def _mlstm_prompt_kernel(x_ref, g_ref, wo_ref, q_ref, kt_ref, v_ref, gt_ref, nw_ref,
                         h_ref, c_out, n_out, m_out, c_scr, n_scr, m_scr):
    c = pl.program_id(1)
    lt = q_ref.shape[0]
    heads = range(M_HEADS)

    @pl.when(c == 0)
    def _():
        c_scr[...] = jnp.zeros_like(c_scr)
        n_scr[...] = jnp.zeros_like(n_scr)
        m_scr[...] = jnp.zeros_like(m_scr)

    gt = gt_ref[...]
    lf = _log_sigmoid(gt)
    bcs = _seg_cumsum(lf, lt)
    bcs_t = jnp.concatenate([bcs, jnp.zeros((128 - GATE_ROWS, lt), F32)], axis=0).T
    t_idx = lax.broadcasted_iota(jnp.int32, (lt, lt), 0)
    s_idx = lax.broadcasted_iota(jnp.int32, (lt, lt), 1)
    causal = s_idx <= t_idx

    q = [q_ref[:, h * M_DQK:(h + 1) * M_DQK] for h in heads]
    kt = [kt_ref[h * M_DQK:(h + 1) * M_DQK, :] for h in heads]
    v = [v_ref[:, h * M_DV:(h + 1) * M_DV] for h in heads]
    m_col = [jnp.broadcast_to(m_scr[h:h + 1, 0:1], (lt, 1)) for h in heads]
    n_old = [n_scr[h:h + 1, :] for h in heads]
    c_old = [c_scr[h] for h in heads]

    a_mat = [jnp.where(causal, gt[h:h + 1] - bcs[M_HEADS + h:M_HEADS + h + 1], -jnp.inf) for h in heads]
    g_col = [jnp.maximum(jnp.max(a_mat[h], axis=1, keepdims=True), m_col[h]) for h in heads]
    w = [jnp.exp(a_mat[h] - g_col[h]) for h in heads]
    w_int = [jnp.exp(m_col[h] - g_col[h]) for h in heads]
    s = [_dot(q[h], kt[h]) * w[h] for h in heads]
    inter = [_dot(q[h], c_old[h].astype(BF16)) for h in heads]
    qn = [jnp.sum(q[h].astype(F32) * n_old[h], axis=1, keepdims=True) for h in heads]
    num = [w_int[h] * inter[h] + _dot(s[h].astype(BF16), v[h]) for h in heads]
    den = [w_int[h] * qn[h] + jnp.sum(s[h], axis=1, keepdims=True) for h in heads]
    m_row = [bcs_t[:, M_HEADS + h:M_HEADS + h + 1] + g_col[h] for h in heads]
    hh = [num[h] * (1.0 / jnp.maximum(jnp.abs(den[h]), jnp.exp(-m_row[h]))) for h in heads]

    wk_row = [w[h][lt - 1:lt, :] for h in heads]
    c_upd = [_dot((kt[h].astype(F32) * wk_row[h]).astype(BF16), v[h]) for h in heads]
    n_upd = [_dot_nt(jnp.broadcast_to(wk_row[h], (8, lt)).astype(BF16), kt[h]) for h in heads]
    for h in heads:
        decay = w_int[h][lt - 1:lt]
        c_scr[h] = decay * c_old[h] + c_upd[h]
        n_scr[h:h + 1, :] = decay * n_old[h] + n_upd[h][0:1]
        m_scr[h:h + 1, :] = jnp.broadcast_to(m_row[h][lt - 1:lt], (1, m_scr.shape[1]))

    o_pre = _dot_nt(_rms(x_ref[...], g_ref[...]).astype(BF16), wo_ref[...])
    for h in heads:
        sl = slice(h * M_DV, (h + 1) * M_DV)
        h_ref[:, sl] = _mlstm_finish(hh[h], o_pre[:, sl], nw_ref[:, sl])

    @pl.when(c == pl.num_programs(1) - 1)
    def _():
        c_out[0] = c_scr[...]
        n_out[0] = n_scr[0:M_HEADS]
        m_out[0] = m_scr[...]


def _mlstm_prompt(x, g, w_mo, pb, pq, kt, gt, nw, batch, seq):
    lt = _pick(seq, (MLSTM_CHUNK, 128))
    nc = seq // lt
    row = lambda b, c: (b * nc + c, 0)
    col = lambda b, c: (0, b * nc + c)
    const = lambda b, c: (0, 0)
    return pl.pallas_call(
        _mlstm_prompt_kernel,
        grid=(batch, nc),
        in_specs=[
            pl.BlockSpec((lt, D_MODEL), row),
            pl.BlockSpec((1, D_MODEL), const),
            pl.BlockSpec((D_MODEL, D_MODEL), const),
            pl.BlockSpec((lt, MQ_W), row),
            pl.BlockSpec((MQ_W, lt), col),
            pl.BlockSpec((lt, D_MODEL), row),
            pl.BlockSpec((GATE_ROWS, lt), col),
            pl.BlockSpec((1, D_MODEL), const),
        ],
        out_specs=[
            pl.BlockSpec((lt, D_MODEL), row),
            pl.BlockSpec((1, M_HEADS, M_DQK, M_DV), lambda b, c: (b, 0, 0, 0)),
            pl.BlockSpec((1, M_HEADS, M_DQK), lambda b, c: (b, 0, 0)),
            pl.BlockSpec((1, 8, 128), lambda b, c: (b, 0, 0)),
        ],
        out_shape=[
            jax.ShapeDtypeStruct((batch * seq, D_MODEL), F32),
            jax.ShapeDtypeStruct((batch, M_HEADS, M_DQK, M_DV), F32),
            jax.ShapeDtypeStruct((batch, M_HEADS, M_DQK), F32),
            jax.ShapeDtypeStruct((batch, 8, 128), F32),
        ],
        scratch_shapes=[pltpu.VMEM((M_HEADS, M_DQK, M_DV), F32), pltpu.VMEM((8, M_DQK), F32),
                        pltpu.VMEM((8, 128), F32)],
        compiler_params=_params("parallel", "arbitrary"),
        name="mlstm_prompt",
    )(x, g, w_mo, pq, kt, pb, gt, nw)


def _mlstm_sample_kernel(x_ref, g_ref, wo_ref, q_ref, kt_ref, v_ref, gt_ref, nw_ref, m_ref, c_in, n_in,
                         sc_in, sn_in, sm_in, h_ref, c_out, n_out, m_out, *, seg):
    del sc_in, sn_in, sm_in
    hd = pl.program_id(1)
    lt = q_ref.shape[0]
    nseg = lt // seg
    gt = gt_ref[...]
    lf = _log_sigmoid(gt)
    bcs = _seg_cumsum(lf, seg)
    row = lax.broadcasted_iota(jnp.int32, (GATE_ROWS, lt), 0)

    def pick(x, r):
        return jnp.sum(jnp.where(row == r, x, 0.0), axis=0, keepdims=True)

    i_row, lf_row, b_row = pick(gt, hd), pick(lf, M_HEADS + hd), pick(bcs, M_HEADS + hd)
    t_idx = lax.broadcasted_iota(jnp.int32, (lt, lt), 0)
    s_idx = lax.broadcasted_iota(jnp.int32, (lt, lt), 1)
    causal = (s_idx <= t_idx) & ((s_idx | (seg - 1)) == (t_idx | (seg - 1)))
    last = t_idx == (s_idx | (seg - 1))

    q = q_ref[...]
    kt = kt_ref[...]
    v = v_ref[...]
    m_col = m_ref[0, 0]
    n_old = [n_in[0, i, pl.ds(hd, 1), :] for i in range(nseg)]
    n_tok = jnp.concatenate([jnp.broadcast_to(n, (seg, M_DQK)) for n in n_old], axis=0)

    a_mat = jnp.where(causal, i_row - b_row, -jnp.inf)
    g_col = jnp.maximum(jnp.max(a_mat, axis=1, keepdims=True), m_col)
    b_col = jnp.sum(jnp.where(causal, lf_row, 0.0), axis=1, keepdims=True)
    w = jnp.exp(a_mat - g_col)
    w_int = jnp.exp(m_col - g_col)
    s = _dot(q, kt) * w
    inter = jnp.concatenate(
        [_dot(q[i * seg:(i + 1) * seg], c_in[0, i, 0].astype(BF16)) for i in range(nseg)], axis=0)
    qn = jnp.sum(q.astype(F32) * n_tok, axis=1, keepdims=True)
    num = w_int * inter + _dot(s.astype(BF16), v)
    den = w_int * qn + jnp.sum(s, axis=1, keepdims=True)
    m_row = b_col + g_col
    h = num * (1.0 / jnp.maximum(jnp.abs(den), jnp.exp(-m_row)))

    wk_row = jnp.sum(jnp.where(last, w, 0.0), axis=0, keepdims=True)
    ktw = kt.astype(F32) * wk_row
    lane_seg = lax.broadcasted_iota(jnp.int32, (1, lt), 1) | (seg - 1)
    lhs = jnp.concatenate(
        [jnp.where(lane_seg == i * seg + seg - 1, ktw, 0.0).astype(BF16) for i in range(nseg)], axis=0)
    c_upd = _dot(lhs, v)
    seg_row = lax.broadcasted_iota(jnp.int32, (lt, 1), 0) * seg + (seg - 1)
    n_upd = _dot_nt(jnp.where(lane_seg == seg_row, wk_row, 0.0).astype(BF16), kt)
    new_m = []
    for i in range(nseg):
        r = i * seg + seg - 1
        decay = w_int[r:r + 1]
        c_out[0, i, 0] = decay * c_in[0, i, 0] + c_upd[i * M_DQK:(i + 1) * M_DQK]
        n_out[0, i, pl.ds(hd, 1), :] = decay * n_old[i] + n_upd[i:i + 1]
        new_m.append(m_row[r:r + 1])
    m_out[0, 0] = jnp.concatenate(new_m, axis=0)
    o_pre = _dot_nt(_rms(x_ref[...], g_ref[...]).astype(BF16), wo_ref[...])
    h_ref[...] = _mlstm_finish(h, o_pre, nw_ref[...])


def _mlstm_sample(x, g, w_mo, pb, pq, kt, gt, nw, m_tok, state_c, state_n, s_c, s_n, s_m,
                  layer, dec_batch, seg):
    lt = SAMPLE_TILE
    nseg = lt // seg
    any_spec = pl.BlockSpec(memory_space=pl.ANY)
    c_spec = pl.BlockSpec((1, nseg, 1, M_DQK, M_DV), lambda t, h: (layer, t, h, 0, 0))
    n_spec = pl.BlockSpec((1, nseg, M_HEADS, M_DQK), lambda t, h: (layer, t, 0, 0))
    return pl.pallas_call(
        functools.partial(_mlstm_sample_kernel, seg=seg),
        grid=(dec_batch // nseg, M_HEADS),
        in_specs=[
            pl.BlockSpec((lt, D_MODEL), lambda t, h: (t, 0)),
            pl.BlockSpec((1, D_MODEL), lambda t, h: (0, 0)),
            pl.BlockSpec((M_DV, D_MODEL), lambda t, h: (h, 0)),
            pl.BlockSpec((lt, M_DQK), lambda t, h: (t, h)),
            pl.BlockSpec((M_DQK, lt), lambda t, h: (h, t)),
            pl.BlockSpec((lt, M_DV), lambda t, h: (t, h)),
            pl.BlockSpec((GATE_ROWS, lt), lambda t, h: (0, t)),
            pl.BlockSpec((1, M_DV), lambda t, h: (0, h)),
            pl.BlockSpec((1, 1, lt, 1), lambda t, h: (layer, h, t, 0)),
            c_spec, n_spec, any_spec, any_spec, any_spec,
        ],
        out_specs=[
            pl.BlockSpec((lt, M_DV), lambda t, h: (t, h)),
            c_spec, n_spec,
            pl.BlockSpec((1, 1, nseg, 1), lambda t, h: (layer, h, t, 0)),
        ],
        out_shape=[
            jax.ShapeDtypeStruct((dec_batch * seg, D_MODEL), F32),
            jax.ShapeDtypeStruct(s_c.shape, F32),
            jax.ShapeDtypeStruct(s_n.shape, F32),
            jax.ShapeDtypeStruct(s_m.shape, F32),
        ],
        input_output_aliases={11: 1, 12: 2, 13: 3},
        compiler_params=_params("parallel", "arbitrary"),
        name="mlstm_sample",
    )(x, g, w_mo, pq, kt, pb, gt, nw, m_tok, state_c, state_n, s_c, s_n, s_m)


def _swa_prompt_kernel(sink_ref, q_ref, kp_ref, kc_ref, vp_ref, vc_ref, o_ref, pk_ref, pv_ref,
                       kb_scr, vt_scr):
    i = pl.program_id(1)
    nsub = q_ref.shape[0] // WINDOW
    lanes = A_GROUP * WINDOW
    kb_scr[0:WINDOW] = kp_ref[...].astype(BF16)
    kb_scr[WINDOW:] = kc_ref[...].astype(BF16)
    vt_scr[:, 0:WINDOW] = vp_ref[...].T.astype(BF16)
    vt_scr[:, WINDOW:] = vc_ref[...].T.astype(BF16)

    all_lanes = A_HEADS * WINDOW
    c_idx = lax.broadcasted_iota(jnp.int32, (2 * WINDOW, all_lanes), 0)
    t_idx = lax.broadcasted_iota(jnp.int32, (2 * WINDOW, all_lanes), 1) & (WINDOW - 1)
    band = (c_idx > t_idx) & (c_idx <= t_idx + WINDOW)
    bias = jnp.where(band, 0.0, -jnp.inf)
    bias_first = jnp.where(band & ((c_idx >= WINDOW) | (i > 0)), 0.0, -jnp.inf)
    lane_head = lax.broadcasted_iota(jnp.int32, (1, all_lanes), 1) >> (WINDOW.bit_length() - 1)
    sink = jnp.zeros((1, all_lanes), F32)
    for h in range(A_HEADS):
        sink = jnp.where(lane_head == h, sink_ref[h] * LOG2E, sink)
    zeros = jnp.zeros((A_HD, lanes), BF16)
    ones = jnp.ones((16, 2 * WINDOW), BF16)

    for j in range(nsub):
        qt = q_ref[j * WINDOW:(j + 1) * WINDOW, :].astype(F32).T.astype(BF16)
        qt4 = [jnp.concatenate([qt[(kv * A_GROUP + g) * A_HD:(kv * A_GROUP + g + 1) * A_HD]
                                for g in range(A_GROUP)], axis=1) for kv in range(A_KV)]
        st = []
        for pair in range(A_KV // 2):
            rhs = jnp.concatenate([jnp.concatenate([qt4[2 * pair], zeros], axis=1),
                                   jnp.concatenate([zeros, qt4[2 * pair + 1]], axis=1)], axis=0)
            kpair = kb_scr[j * WINDOW:(j + 2) * WINDOW, pair * 128:(pair + 1) * 128]
            st.append(_dot(kpair, rhs))
        st = jnp.concatenate(st, axis=1) + (bias_first if j == 0 else bias)
        mx = jnp.maximum(jnp.max(st, axis=0, keepdims=True), sink)
        p = jnp.exp2(st - mx).astype(BF16)
        sink_p = jnp.exp2(sink - mx)
        pieces = []
        for kv in range(A_KV):
            vt = jnp.concatenate([vt_scr[kv * A_HD:(kv + 1) * A_HD, j * WINDOW:(j + 2) * WINDOW], ones], axis=0)
            ot = _dot(vt, p[:, kv * lanes:(kv + 1) * lanes])
            den = ot[A_HD:A_HD + 1] + sink_p[:, kv * lanes:(kv + 1) * lanes]
            ot = ot[:A_HD] * (1.0 / den)
            pieces.extend(ot[:, g * WINDOW:(g + 1) * WINDOW] for g in range(A_GROUP))
        o_ref[j * WINDOW:(j + 1) * WINDOW, :] = jnp.concatenate(pieces, axis=0).T

    @pl.when(i == pl.num_programs(1) - 1)
    def _():
        rows = kc_ref.shape[0]
        pk_ref[0] = kc_ref[rows - WINDOW:rows, :].T
        pv_ref[0] = vc_ref[rows - WINDOW:rows, :].T


def _swa_prompt(pb, pf, sinks, batch, seq):
    tq = _pick(seq, (SWA_STEP, 256, 128))
    ns = seq // tq
    per = tq // WINDOW
    nb = seq // WINDOW
    cur = lambda col: (lambda b, i: (b * ns + i, col))
    prev = lambda col: (lambda b, i: (b * nb + jnp.maximum(i * per - 1, 0), col))
    return pl.pallas_call(
        _swa_prompt_kernel,
        grid=(batch, ns),
        in_specs=[
            pl.BlockSpec(memory_space=pltpu.SMEM),
            pl.BlockSpec((tq, D_MODEL), cur(1)),
            pl.BlockSpec((WINDOW, KV_W), prev(0)),
            pl.BlockSpec((tq, KV_W), cur(0)),
            pl.BlockSpec((WINDOW, KV_W), prev(1)),
            pl.BlockSpec((tq, KV_W), cur(1)),
        ],
        out_specs=[
            pl.BlockSpec((tq, D_MODEL), lambda b, i: (b * ns + i, 0)),
            pl.BlockSpec((1, KV_W, WINDOW), lambda b, i: (b, 0, 0)),
            pl.BlockSpec((1, KV_W, WINDOW), lambda b, i: (b, 0, 0)),
        ],
        out_shape=[
            jax.ShapeDtypeStruct((batch * seq, D_MODEL), F32),
            jax.ShapeDtypeStruct((batch, KV_W, WINDOW), F32),
            jax.ShapeDtypeStruct((batch, KV_W, WINDOW), F32),
        ],
        scratch_shapes=[pltpu.VMEM((WINDOW + tq, KV_W), BF16), pltpu.VMEM((KV_W, WINDOW + tq), BF16)],
        compiler_params=_params("parallel", "arbitrary"),
        name="swa_prompt",
    )(sinks, pb, pf, pf, pf, pf)


def _bdot(a, b, contract_b):
    return lax.dot_general(a, b, (((2,), (contract_b,)), ((0,), (0,))), preferred_element_type=F32)


def _swa_sample_kernel(sink_ref, q_ref, kn_ref, vn_ref, ck_ref, cv_ref, sk_in, sv_in,
                       o_ref, sk_ref, sv_ref, *, tq):
    del sk_in, sv_in
    nb = ck_ref.shape[1]
    wc = ck_ref.shape[3]
    keep = wc - tq
    q3 = q_ref[...].astype(F32).reshape(nb, tq, D_MODEL)
    ck = ck_ref[0]
    cv = cv_ref[0]

    lane = lax.broadcasted_iota(jnp.int32, (nb, KV_W, wc), 2)

    def appended(cache, new_rows):
        new_t = new_rows.T
        placed = jnp.stack([pltpu.roll(new_t, (keep - b * tq) % wc, axis=1) for b in range(nb)])
        return jnp.where(lane >= keep, placed, pltpu.roll(cache, keep, axis=2))

    sk = appended(ck, kn_ref[...])
    sv = appended(cv, vn_ref[...])
    sk_ref[0] = sk
    sv_ref[0] = sv

    rows = A_GROUP * tq
    t_idx = lax.broadcasted_iota(jnp.int32, (1, rows, 1), 1) & (tq - 1)
    c_idx = lax.broadcasted_iota(jnp.int32, (1, 1, wc), 2)
    mask_old = c_idx > t_idx
    mask_new = (c_idx >= keep) & (c_idx - keep <= t_idx)
    g_idx = lax.broadcasted_iota(jnp.int32, (1, rows, 1), 1) >> (tq.bit_length() - 1)
    pieces = []
    for kv in range(A_KV):
        heads = [kv * A_GROUP + g for g in range(A_GROUP)]
        q4 = jnp.concatenate([q3[:, :, h * A_HD:(h + 1) * A_HD] for h in heads], axis=1).astype(BF16)
        sink = jnp.zeros((1, rows, 1), F32)
        for g in range(A_GROUP):
            sink = jnp.where(g_idx == g, sink_ref[heads[g]] * LOG2E, sink)
        sl = slice(kv * A_HD, (kv + 1) * A_HD)
        s1 = jnp.where(mask_old, _bdot(q4, ck[:, sl, :].astype(BF16), 1), -jnp.inf)
        s2 = jnp.where(mask_new, _bdot(q4, sk[:, sl, :].astype(BF16), 1), -jnp.inf)
        mx = jnp.maximum(jnp.maximum(jnp.max(s1, axis=2, keepdims=True),
                                     jnp.max(s2, axis=2, keepdims=True)), sink)
        p1 = jnp.exp2(s1 - mx)
        p2 = jnp.exp2(s2 - mx)
        den = (jnp.sum(p1, axis=2, keepdims=True) + jnp.sum(p2, axis=2, keepdims=True)
               + jnp.exp2(sink - mx))
        r = 1.0 / den
        o = (_bdot((p1 * r).astype(BF16), cv[:, sl, :].astype(BF16), 2)
             + _bdot((p2 * r).astype(BF16), sv[:, sl, :].astype(BF16), 2))
        pieces.extend(o[:, g * tq:(g + 1) * tq, :] for g in range(A_GROUP))
    o_ref[...] = jnp.concatenate(pieces, axis=2).reshape(nb * tq, D_MODEL)


def _swa_sample(pb, pf, sinks, cache_kt, cache_vt, s_k, s_v, layer, dec_batch, tq):
    lt = SAMPLE_TILE
    nb = lt // tq
    wc = cache_kt.shape[3]
    assert wc == lt, "the appended keys are placed with lane rolls over one cache row"
    any_spec = pl.BlockSpec(memory_space=pl.ANY)
    cache_spec = pl.BlockSpec((1, nb, KV_W, wc), lambda i: (layer, i, 0, 0))
    return pl.pallas_call(
        functools.partial(_swa_sample_kernel, tq=tq),
        grid=(dec_batch // nb,),
        in_specs=[
            pl.BlockSpec(memory_space=pltpu.SMEM),
            pl.BlockSpec((lt, D_MODEL), lambda i: (i, 1)),
            pl.BlockSpec((lt, KV_W), lambda i: (i, 0)),
            pl.BlockSpec((lt, KV_W), lambda i: (i, 1)),
            cache_spec, cache_spec, any_spec, any_spec,
        ],
        out_specs=[
            pl.BlockSpec((lt, D_MODEL), lambda i: (i, 0)),
            cache_spec, cache_spec,
        ],
        out_shape=[
            jax.ShapeDtypeStruct((dec_batch * tq, D_MODEL), F32),
            jax.ShapeDtypeStruct(s_k.shape, F32),
            jax.ShapeDtypeStruct(s_v.shape, F32),
        ],
        input_output_aliases={6: 1, 7: 2},
        compiler_params=_params("parallel"),
        name="swa_sample",
    )(sinks, pb, pf, pf, cache_kt, cache_vt, s_k, s_v)


def _merge_kernel(x_ref, g_ref, wg_ref, ba_ref, bb_ref, w_ref, o_ref):
    x = x_ref[...]
    gates = _dot_nt(_rms(x, g_ref[...]).astype(BF16), wg_ref[...])
    merged = _sigmoid(gates[:, :D_MODEL]) * ba_ref[...] + _sigmoid(gates[:, D_MODEL:]) * bb_ref[...]
    o_ref[...] = x + _dot(merged.astype(BF16), w_ref[...])


def _merge(x, g, w_gates, ba, bb, w_out):
    n = x.shape[0]
    tm = _pick(n, (512, 256, 128))
    row = pl.BlockSpec((tm, D_MODEL), lambda i: (i, 0))
    const = lambda shape: pl.BlockSpec(shape, lambda i: (0, 0))
    return pl.pallas_call(
        _merge_kernel,
        grid=(n // tm,),
        in_specs=[row, const((1, D_MODEL)), const((2 * D_MODEL, D_MODEL)), row, row,
                  const((D_MODEL, D_MODEL))],
        out_specs=row,
        out_shape=jax.ShapeDtypeStruct((n, D_MODEL), F32),
        compiler_params=_params("parallel"),
        name="merge_out_proj",
    )(x, g, w_gates, ba, bb, w_out)


def _mlp_kernel(x_ref, g_ref, wu_ref, wd_ref, gf_ref, o_ref, xn_ref, *, final_norm):
    j = pl.program_id(1)

    @pl.when(j == 0)
    def _():
        x = x_ref[...]
        xn_ref[...] = _rms(x, g_ref[...]).astype(BF16)
        o_ref[...] = x

    h = jnp.square(jnp.maximum(_dot(xn_ref[...], wu_ref[...]), 0.0))
    o_ref[...] += _dot(h.astype(BF16), wd_ref[...])

    if final_norm:
        @pl.when(j == pl.num_programs(1) - 1)
        def _():
            o_ref[...] = _rms(o_ref[...], gf_ref[...])


def _mlp(x, g, w_up, w_down, g_final, final_norm):
    n = x.shape[0]
    tm = _pick(n, (1024, 512, 256, 128))
    tf = 1024
    return pl.pallas_call(
        functools.partial(_mlp_kernel, final_norm=final_norm),
        grid=(n // tm, D_FF // tf),
        in_specs=[
            pl.BlockSpec((tm, D_MODEL), lambda i, j: (i, 0)),
            pl.BlockSpec((1, D_MODEL), lambda i, j: (0, 0)),
            pl.BlockSpec((D_MODEL, tf), lambda i, j: (0, j)),
            pl.BlockSpec((tf, D_MODEL), lambda i, j: (j, 0)),
            pl.BlockSpec((1, D_MODEL), lambda i, j: (0, 0)),
        ],
        out_specs=pl.BlockSpec((tm, D_MODEL), lambda i, j: (i, 0)),
        out_shape=jax.ShapeDtypeStruct((n, D_MODEL), F32),
        scratch_shapes=[pltpu.VMEM((tm, D_MODEL), BF16)],
        compiler_params=_params("parallel", "arbitrary"),
        name="mlp",
    )(x, g, w_up, w_down, g_final)


def _prep_in_proj_weights(w_in, b_gate):
    sizes = (MQ_W, MQ_W, D_MODEL, D_MODEL, M_HEADS, M_HEADS, D_MODEL, KV_W, KV_W, D_MODEL, D_MODEL)
    offs = [0]
    for s in sizes:
        offs.append(offs[-1] + s)
    w_t = jnp.swapaxes(w_in, 1, 2).astype(BF16)
    mq, mk, mv, mo, mi, mf, aq, ak, av, ga, gb = (w_t[:, offs[i]:offs[i + 1]] for i in range(11))
    depth = w_in.shape[0]
    w_proj = jnp.concatenate([mv, aq, mq, ak, av], axis=1)
    pad = jnp.zeros((depth, GATE_ROWS - 2 * M_HEADS, D_MODEL), BF16)
    w_gate = jnp.concatenate([mi, mf, pad], axis=1)
    b_gate_col = jnp.zeros((depth, GATE_ROWS, 1), F32).at[:, :2 * M_HEADS, 0].set(b_gate.astype(F32))
    w_merge = jnp.concatenate([ga, gb], axis=1)
    scale = jnp.ones((1, D_PROJ), F32)
    scale = scale.at[:, PROJ_BLOCK:2 * PROJ_BLOCK].set(A_HD ** -0.5 * LOG2E)
    scale = scale.at[:, 2 * PROJ_BLOCK:2 * PROJ_BLOCK + MQ_W].set(M_DQK ** -0.5)
    return w_proj, mk, w_gate, b_gate_col, scale, mo, w_merge


def kernel(x_prompt, x_sample, state_C, state_n, state_m, cache_k, cache_v, norm_attn, w_in, b_gate,
           mlstm_norm, sinks, w_out, norm_mlp, w_up, w_down, norm_final):
    batch, seq, _ = x_prompt.shape
    dec_batch, dec_seq, _ = x_sample.shape
    depth = w_in.shape[0]
    wc = cache_k.shape[2]

    w_proj, w_mk, w_gate, b_gate_col, scale, w_mo, w_merge = _prep_in_proj_weights(w_in, b_gate)
    w_out_b = w_out.astype(BF16)
    w_up_b = w_up.astype(BF16)
    w_down_b = w_down.astype(BF16)

    xp = x_prompt.reshape(batch * seq, D_MODEL)
    xs = x_sample.reshape(dec_batch * dec_seq, D_MODEL)
    m_tok = jnp.repeat(jnp.swapaxes(state_m, 1, 2), dec_seq, axis=2)[..., None]
    cache_kt = jnp.transpose(cache_k, (0, 1, 3, 4, 2)).reshape(depth, dec_batch, KV_W, wc)
    cache_vt = jnp.transpose(cache_v, (0, 1, 3, 4, 2)).reshape(depth, dec_batch, KV_W, wc)

    s_c = jnp.zeros(state_C.shape, F32)
    s_n = jnp.zeros(state_n.shape, F32)
    s_m = jnp.zeros((depth, M_HEADS, dec_batch, 1), F32)
    s_k = jnp.zeros(cache_kt.shape, F32)
    s_v = jnp.zeros(cache_vt.shape, F32)
    p_c, p_n, p_m, p_k, p_v = [], [], [], [], []
    g_final = norm_final[None]

    for l in range(depth):
        g_attn = norm_attn[l][None]
        nw = mlstm_norm[l][None]
        g_mlp = norm_mlp[l][None]
        last = l == depth - 1

        pb, pq, pf, kt, gt = _in_proj(xp, g_attn, w_proj[l], w_mk[l], w_gate[l], b_gate_col[l], scale)
        ba, pc, pn, pm = _mlstm_prompt(xp, g_attn, w_mo[l], pb, pq, kt, gt, nw, batch, seq)
        bb, pk, pv = _swa_prompt(pb, pf, sinks[l], batch, seq)
        xp = _merge(xp, g_attn, w_merge[l], ba, bb, w_out_b[l])
        xp = _mlp(xp, g_mlp, w_up_b[l], w_down_b[l], g_final, last)
        p_c.append(pc)
        p_n.append(pn)
        p_m.append(pm[:, :M_HEADS, 0])
        p_k.append(pk)
        p_v.append(pv)

        pb, pq, pf, kt, gt = _in_proj(xs, g_attn, w_proj[l], w_mk[l], w_gate[l], b_gate_col[l], scale)
        ba, s_c, s_n, s_m = _mlstm_sample(xs, g_attn, w_mo[l], pb, pq, kt, gt, nw, m_tok, state_C, state_n,
                                          s_c, s_n, s_m, l, dec_batch, dec_seq)
        bb, s_k, s_v = _swa_sample(pb, pf, sinks[l], cache_kt, cache_vt, s_k, s_v, l, dec_batch, dec_seq)
        xs = _merge(xs, g_attn, w_merge[l], ba, bb, w_out_b[l])
        xs = _mlp(xs, g_mlp, w_up_b[l], w_down_b[l], g_final, last)

    def positions_major(t, lead):
        t = t.reshape(*lead, A_KV, A_HD, wc)
        return jnp.moveaxis(t, -1, -3)

    return (xp.reshape(batch, seq, D_MODEL), xs.reshape(dec_batch, dec_seq, D_MODEL),
            jnp.stack(p_c), jnp.stack(p_n), jnp.stack(p_m),
            positions_major(jnp.stack(p_k), (depth, batch)), positions_major(jnp.stack(p_v), (depth, batch)),
            s_c, s_n, jnp.swapaxes(s_m[..., 0], 1, 2),
            positions_major(s_k, (depth, dec_batch)), positions_major(s_v, (depth, dec_batch)))
```

```python
import functools

import jax
import jax.numpy as jnp
from jax import lax
from jax.experimental import pallas as pl
from jax.experimental.pallas import tpu as pltpu

F32 = jnp.float32
BF16 = jnp.bfloat16

D_MODEL = 1024
M_HEADS = 4
M_DQK = 128
M_DV = D_MODEL // M_HEADS
MQ_W = M_HEADS * M_DQK
A_HEADS = 16
A_KV = 4
A_GROUP = A_HEADS // A_KV
A_HD = D_MODEL // A_HEADS
KV_W = A_KV * A_HD
WINDOW = 128
D_FF = 4 * D_MODEL
EPS = 1e-6
LOG2E = 1.4426950408889634

PROJ_BLOCK = 1024
D_PROJ = 3 * PROJ_BLOCK
GATE_ROWS = 16

MLSTM_CHUNK = 256
SAMPLE_TILE = 128
SWA_STEP = 512
VMEM_LIMIT = 56 * 1024 * 1024


def _pick(n, candidates):
    for c in candidates:
        if n % c == 0:
            return c
    raise ValueError(f"no block size for {n}")


def _sigmoid(x):
    return 1.0 / (1.0 + jnp.exp(-x))


def _log_sigmoid(x):
    return jnp.minimum(x, 0.0) - jnp.log1p(jnp.exp(-jnp.abs(x)))


def _rms(x, g):
    y = x * lax.rsqrt(jnp.mean(x * x, axis=-1, keepdims=True) + EPS)
    return y * g


def _dot(a, b):
    return jnp.dot(a, b, preferred_element_type=F32)


def _dot_nt(a, b):
    return lax.dot_general(a, b, (((1,), (1,)), ((), ())), preferred_element_type=F32)


def _params(*sem):
    return pltpu.CompilerParams(dimension_semantics=sem, vmem_limit_bytes=VMEM_LIMIT)


def _in_proj_kernel(*refs, q_transposed):
    (x_ref, g_ref, w_ref, wk_ref, wg_ref, bg_ref, scale_ref, scale_t_ref,
     xn_out, mv_ref, mq_ref, pf_ref, kt_ref, gt_ref, aq_ref, xn_ref) = refs
    j = pl.program_id(1)

    @pl.when(j == 0)
    def _():
        xn = _rms(x_ref[...], g_ref[...]).astype(BF16)
        xn_ref[...] = xn
        xn_out[...] = xn
        kt_ref[...] = _dot_nt(wk_ref[...], xn).astype(BF16)
        gt_ref[...] = _dot_nt(wg_ref[...], xn) + bg_ref[...]
        mv_ref[...] = (_dot_nt(xn, w_ref[...]) * scale_ref[...]).astype(BF16)

    @pl.when(j == 1)
    def _():
        if q_transposed:
            aq_ref[...] = (_dot_nt(w_ref[...], xn_ref[...]) * scale_t_ref[...]).astype(BF16)
        else:
            aq_ref[...] = (_dot_nt(xn_ref[...], w_ref[...]) * scale_ref[...]).astype(BF16)

    @pl.when(j == 2)
    def _():
        acc = _dot_nt(xn_ref[...], w_ref[...]) * scale_ref[...]
        mq_ref[...] = acc[:, :MQ_W].astype(BF16)
        pf_ref[...] = acc[:, MQ_W:]


def _in_proj(x, g, w, wk, wg, bg, scale, scale_t, q_transposed):
    n = x.shape[0]
    tm = _pick(n, (1024, 512, 256, 128))
    tn = PROJ_BLOCK
    const = lambda i, j: (0, 0)
    row = lambda i, j: (i, 0)
    col = lambda i, j: (0, i)
    if q_transposed:
        aq_spec, aq_shape = pl.BlockSpec((D_MODEL, tm), col), (D_MODEL, n)
    else:
        aq_spec, aq_shape = pl.BlockSpec((tm, D_MODEL), row), (n, D_MODEL)
    return pl.pallas_call(
        functools.partial(_in_proj_kernel, q_transposed=q_transposed),
        grid=(n // tm, D_PROJ // tn),
        in_specs=[
            pl.BlockSpec((tm, D_MODEL), row),
            pl.BlockSpec((1, D_MODEL), const),
            pl.BlockSpec((tn, D_MODEL), lambda i, j: (j, 0)),
            pl.BlockSpec((MQ_W, D_MODEL), const),
            pl.BlockSpec((GATE_ROWS, D_MODEL), const),
            pl.BlockSpec((GATE_ROWS, 1), const),
            pl.BlockSpec((1, tn), lambda i, j: (0, j)),
            pl.BlockSpec((tn, 1), lambda i, j: (j, 0)),
        ],
        out_specs=[
            pl.BlockSpec((tm, D_MODEL), row),
            pl.BlockSpec((tm, D_MODEL), row),
            pl.BlockSpec((tm, MQ_W), row),
            pl.BlockSpec((tm, 2 * KV_W), row),
            pl.BlockSpec((MQ_W, tm), col),
            pl.BlockSpec((GATE_ROWS, tm), col),
            aq_spec,
        ],
        out_shape=[
            jax.ShapeDtypeStruct((n, D_MODEL), BF16),
            jax.ShapeDtypeStruct((n, D_MODEL), BF16),
            jax.ShapeDtypeStruct((n, MQ_W), BF16),
            jax.ShapeDtypeStruct((n, 2 * KV_W), F32),
            jax.ShapeDtypeStruct((MQ_W, n), BF16),
            jax.ShapeDtypeStruct((GATE_ROWS, n), F32),
            jax.ShapeDtypeStruct(aq_shape, BF16),
        ],
        scratch_shapes=[pltpu.VMEM((tm, D_MODEL), BF16)],
        compiler_params=_params("parallel", "arbitrary"),
        name="in_proj",
    )(x, g, w, wk, wg, bg, scale, scale_t)


def _seg_cumsum(x, seg):
    pos = lax.broadcasted_iota(jnp.int32, x.shape, 1) & (seg - 1)
    sh = 1
    while sh < seg:
        x = x + jnp.where(pos >= sh, pltpu.roll(x, sh, axis=1), 0.0)
        sh *= 2
    return x


def _mlstm_finish(num, inv, o_pre, nw):
    msq = jnp.mean(num * num, axis=-1, keepdims=True)
    r = inv * lax.rsqrt(inv * inv * msq + EPS)
    return _sigmoid(o_pre) * (num * r) * nw


def _mlstm_prompt_kernel(xn_ref, wo_ref, q_ref, kt_ref, v_ref, gt_ref, nw_ref,
                         h_ref, c_out, n_out, m_out, c_scr, n_scr, m_scr):
    c = pl.program_id(1)
    lt = q_ref.shape[0]
    heads = range(M_HEADS)

    @pl.when(c == 0)
    def _():
        c_scr[...] = jnp.zeros_like(c_scr)
        n_scr[...] = jnp.zeros_like(n_scr)
        m_scr[...] = jnp.zeros_like(m_scr)

    gt = gt_ref[...]
    i2 = gt * LOG2E
    b2 = _seg_cumsum(_log_sigmoid(gt), lt) * LOG2E
    b2_t = jnp.concatenate([b2, jnp.zeros((128 - GATE_ROWS, lt), F32)], axis=0).T
    t_idx = lax.broadcasted_iota(jnp.int32, (lt, lt), 0)
    s_idx = lax.broadcasted_iota(jnp.int32, (lt, lt), 1)
    causal = s_idx <= t_idx

    q = [q_ref[:, h * M_DQK:(h + 1) * M_DQK] for h in heads]
    kt = [kt_ref[h * M_DQK:(h + 1) * M_DQK, :] for h in heads]
    v = [v_ref[:, h * M_DV:(h + 1) * M_DV] for h in heads]
    m2_col = [jnp.broadcast_to(m_scr[h:h + 1, 0:1] * LOG2E, (lt, 1)) for h in heads]
    n_old = [n_scr[h:h + 1, :] for h in heads]
    c_old = [c_scr[h] for h in heads]

    a_mat = [jnp.where(causal, i2[h:h + 1] - b2[M_HEADS + h:M_HEADS + h + 1], -jnp.inf) for h in heads]
    g_col = [jnp.maximum(jnp.max(a_mat[h], axis=1, keepdims=True), m2_col[h]) for h in heads]
    w = [jnp.exp2(a_mat[h] - g_col[h]) for h in heads]
    w_int = [jnp.exp2(m2_col[h] - g_col[h]) for h in heads]
    s = [_dot(q[h], kt[h]) * w[h] for h in heads]
    inter = [_dot(q[h], c_old[h].astype(BF16)) for h in heads]
    qn = [jnp.sum(q[h].astype(F32) * n_old[h], axis=1, keepdims=True) for h in heads]
    num = [w_int[h] * inter[h] + _dot(s[h].astype(BF16), v[h]) for h in heads]
    den = [w_int[h] * qn[h] + jnp.sum(s[h], axis=1, keepdims=True) for h in heads]
    m2_row = [b2_t[:, M_HEADS + h:M_HEADS + h + 1] + g_col[h] for h in heads]
    inv = [1.0 / jnp.maximum(jnp.abs(den[h]), jnp.exp2(-m2_row[h])) for h in heads]

    wk_row = [w[h][lt - 1:lt, :] for h in heads]
    c_upd = [_dot((kt[h].astype(F32) * wk_row[h]).astype(BF16), v[h]) for h in heads]
    n_upd = [_dot_nt(jnp.broadcast_to(wk_row[h], (8, lt)).astype(BF16), kt[h]) for h in heads]
    for h in heads:
        decay = w_int[h][lt - 1:lt]
        c_scr[h] = decay * c_old[h] + c_upd[h]
        n_scr[h:h + 1, :] = decay * n_old[h] + n_upd[h][0:1]
        m_scr[h:h + 1, :] = jnp.broadcast_to(m2_row[h][lt - 1:lt] * (1.0 / LOG2E), (1, m_scr.shape[1]))

    o_pre = _dot_nt(xn_ref[...], wo_ref[...])
    for h in heads:
        sl = slice(h * M_DV, (h + 1) * M_DV)
        h_ref[:, sl] = _mlstm_finish(num[h], inv[h], o_pre[:, sl], nw_ref[:, sl])

    @pl.when(c == pl.num_programs(1) - 1)
    def _():
        c_out[0] = c_scr[...]
        n_out[0] = n_scr[0:M_HEADS]
        m_out[0] = m_scr[...]


def _mlstm_prompt(xn, w_mo, mv, mq, kt, gt, nw, batch, seq):
    lt = _pick(seq, (MLSTM_CHUNK, 128))
    nc = seq // lt
    row = lambda b, c: (b * nc + c, 0)
    col = lambda b, c: (0, b * nc + c)
    const = lambda b, c: (0, 0)
    return pl.pallas_call(
        _mlstm_prompt_kernel,
        grid=(batch, nc),
        in_specs=[
            pl.BlockSpec((lt, D_MODEL), row),
            pl.BlockSpec((D_MODEL, D_MODEL), const),
            pl.BlockSpec((lt, MQ_W), row),
            pl.BlockSpec((MQ_W, lt), col),
            pl.BlockSpec((lt, D_MODEL), row),
            pl.BlockSpec((GATE_ROWS, lt), col),
            pl.BlockSpec((1, D_MODEL), const),
        ],
        out_specs=[
            pl.BlockSpec((lt, D_MODEL), row),
            pl.BlockSpec((1, M_HEADS, M_DQK, M_DV), lambda b, c: (b, 0, 0, 0)),
            pl.BlockSpec((1, M_HEADS, M_DQK), lambda b, c: (b, 0, 0)),
            pl.BlockSpec((1, 8, 128), lambda b, c: (b, 0, 0)),
        ],
        out_shape=[
            jax.ShapeDtypeStruct((batch * seq, D_MODEL), F32),
            jax.ShapeDtypeStruct((batch, M_HEADS, M_DQK, M_DV), F32),
            jax.ShapeDtypeStruct((batch, M_HEADS, M_DQK), F32),
            jax.ShapeDtypeStruct((batch, 8, 128), F32),
        ],
        scratch_shapes=[pltpu.VMEM((M_HEADS, M_DQK, M_DV), F32), pltpu.VMEM((8, M_DQK), F32),
                        pltpu.VMEM((8, 128), F32)],
        compiler_params=_params("parallel", "arbitrary"),
        name="mlstm_prompt",
    )(xn, w_mo, mq, kt, mv, gt, nw)


def _mlstm_sample_kernel(*refs, seg):
    xn_ref, wo_ref, q_ref, kt_ref, v_ref, gt_ref, nw_ref, m_ref, c_in, n_in = refs[:10]
    h_ref, c_out, n_out, m_out = refs[-4:]
    hd = pl.program_id(1)
    lt = q_ref.shape[0]
    nseg = lt // seg
    gt = gt_ref[...]
    lf = _log_sigmoid(gt)
    bcs = _seg_cumsum(lf, seg)
    row = lax.broadcasted_iota(jnp.int32, (GATE_ROWS, lt), 0)

    def pick(x, r):
        return jnp.sum(jnp.where(row == r, x, 0.0), axis=0, keepdims=True)

    i_row, lf_row, b_row = pick(gt, hd), pick(lf, M_HEADS + hd), pick(bcs, M_HEADS + hd)
    t_idx = lax.broadcasted_iota(jnp.int32, (lt, lt), 0)
    s_idx = lax.broadcasted_iota(jnp.int32, (lt, lt), 1)
    causal = (s_idx <= t_idx) & ((s_idx | (seg - 1)) == (t_idx | (seg - 1)))
    last = t_idx == (s_idx | (seg - 1))

    q = q_ref[...]
    kt = kt_ref[...]
    v = v_ref[...]
    m_col = m_ref[0, 0]
    n_old = [n_in[0, i, pl.ds(hd, 1), :] for i in range(nseg)]
    n_tok = jnp.concatenate([jnp.broadcast_to(n, (seg, M_DQK)) for n in n_old], axis=0)

    a_mat = jnp.where(causal, i_row - b_row, -jnp.inf)
    g_col = jnp.maximum(jnp.max(a_mat, axis=1, keepdims=True), m_col)
    b_col = jnp.sum(jnp.where(causal, lf_row, 0.0), axis=1, keepdims=True)
    w = jnp.exp(a_mat - g_col)
    w_int = jnp.exp(m_col - g_col)
    s = _dot(q, kt) * w
    inter = jnp.concatenate(
        [_dot(q[i * seg:(i + 1) * seg], c_in[0, i, 0].astype(BF16)) for i in range(nseg)], axis=0)
    qn = jnp.sum(q.astype(F32) * n_tok, axis=1, keepdims=True)
    num = w_int * inter + _dot(s.astype(BF16), v)
    den = w_int * qn + jnp.sum(s, axis=1, keepdims=True)
    m_row = b_col + g_col
    inv = 1.0 / jnp.maximum(jnp.abs(den), jnp.exp(-m_row))

    wk_row = jnp.sum(jnp.where(last, w, 0.0), axis=0, keepdims=True)
    ktw = kt.astype(F32) * wk_row
    lane_seg = lax.broadcasted_iota(jnp.int32, (1, lt), 1) | (seg - 1)
    lhs = jnp.concatenate(
        [jnp.where(lane_seg == i * seg + seg - 1, ktw, 0.0).astype(BF16) for i in range(nseg)], axis=0)
    c_upd = _dot(lhs, v)
    seg_row = lax.broadcasted_iota(jnp.int32, (lt, 1), 0) * seg + (seg - 1)
    n_upd = _dot_nt(jnp.where(lane_seg == seg_row, wk_row, 0.0).astype(BF16), kt)
    new_m = []
    for i in range(nseg):
        r = i * seg + seg - 1
        decay = w_int[r:r + 1]
        c_out[0, i, 0] = decay * c_in[0, i, 0] + c_upd[i * M_DQK:(i + 1) * M_DQK]
        n_out[0, i, pl.ds(hd, 1), :] = decay * n_old[i] + n_upd[i:i + 1]
        new_m.append(m_row[r:r + 1])
    m_out[0, 0] = jnp.concatenate(new_m, axis=0)
    h_ref[...] = _mlstm_finish(num, inv, _dot_nt(xn_ref[...], wo_ref[...]), nw_ref[...])


def _mlstm_sample(xn, w_mo, mv, mq, kt, gt, nw, m_tok, state_c, state_n, carried, layer, dec_batch, seg):
    lt = SAMPLE_TILE
    nseg = lt // seg
    depth = state_c.shape[0]
    any_spec = pl.BlockSpec(memory_space=pl.ANY)
    c_spec = pl.BlockSpec((1, nseg, 1, M_DQK, M_DV), lambda t, h: (layer, t, h, 0, 0))
    n_spec = pl.BlockSpec((1, nseg, M_HEADS, M_DQK), lambda t, h: (layer, t, 0, 0))
    operands = [xn, w_mo, mq, kt, mv, gt, nw, m_tok, state_c, state_n]
    in_specs = [
        pl.BlockSpec((lt, D_MODEL), lambda t, h: (t, 0)),
        pl.BlockSpec((M_DV, D_MODEL), lambda t, h: (h, 0)),
        pl.BlockSpec((lt, M_DQK), lambda t, h: (t, h)),
        pl.BlockSpec((M_DQK, lt), lambda t, h: (h, t)),
        pl.BlockSpec((lt, M_DV), lambda t, h: (t, h)),
        pl.BlockSpec((GATE_ROWS, lt), lambda t, h: (0, t)),
        pl.BlockSpec((1, M_DV), lambda t, h: (0, h)),
        pl.BlockSpec((1, 1, lt, 1), lambda t, h: (layer, h, t, 0)),
        c_spec, n_spec,
    ]
    aliases = {}
    if carried is not None:
        aliases = {len(operands) + k: 1 + k for k in range(3)}
        operands += list(carried)
        in_specs += [any_spec] * 3
    return pl.pallas_call(
        functools.partial(_mlstm_sample_kernel, seg=seg),
        grid=(dec_batch // nseg, M_HEADS),
        in_specs=in_specs,
        out_specs=[
            pl.BlockSpec((lt, M_DV), lambda t, h: (t, h)),
            c_spec, n_spec,
            pl.BlockSpec((1, 1, nseg, 1), lambda t, h: (layer, h, t, 0)),
        ],
        out_shape=[
            jax.ShapeDtypeStruct((dec_batch * seg, D_MODEL), F32),
            jax.ShapeDtypeStruct(state_c.shape, F32),
            jax.ShapeDtypeStruct(state_n.shape, F32),
            jax.ShapeDtypeStruct((depth, M_HEADS, dec_batch, 1), F32),
        ],
        input_output_aliases=aliases,
        compiler_params=_params("parallel", "arbitrary"),
        name="mlstm_sample",
    )(*operands)


def _swa_prompt_kernel(sink_ref, bias_ref, qt_ref, kp_ref, kc_ref, vp_ref, vc_ref, o_ref, pk_ref, pv_ref,
                       kb_scr, vt_scr):
    i = pl.program_id(1)
    nsub = qt_ref.shape[1] // WINDOW
    lanes = A_GROUP * WINDOW
    all_lanes = A_HEADS * WINDOW
    kb_scr[0:WINDOW] = kp_ref[...].astype(BF16)
    kb_scr[WINDOW:] = kc_ref[...].astype(BF16)
    vt_scr[:, 0:WINDOW] = vp_ref[...].T.astype(BF16)
    vt_scr[:, WINDOW:] = vc_ref[...].T.astype(BF16)

    lane_head = lax.broadcasted_iota(jnp.int32, (1, all_lanes), 1) >> (WINDOW.bit_length() - 1)
    sink = jnp.zeros((1, all_lanes), F32)
    for h in range(A_HEADS):
        sink = jnp.where(lane_head == h, sink_ref[h] * LOG2E, sink)
    has_prev = (lax.broadcasted_iota(jnp.int32, (2 * WINDOW, 1), 0) >= WINDOW) | (i > 0)
    zeros = jnp.zeros((A_HD, lanes), BF16)
    ones = jnp.ones((16, 2 * WINDOW), BF16)

    for j in range(nsub):
        qt = qt_ref[:, j * WINDOW:(j + 1) * WINDOW]
        qt4 = [jnp.concatenate([qt[(kv * A_GROUP + g) * A_HD:(kv * A_GROUP + g + 1) * A_HD]
                                for g in range(A_GROUP)], axis=1) for kv in range(A_KV)]
        st = []
        for pair in range(A_KV // 2):
            rhs = jnp.concatenate([jnp.concatenate([qt4[2 * pair], zeros], axis=1),
                                   jnp.concatenate([zeros, qt4[2 * pair + 1]], axis=1)], axis=0)
            kpair = kb_scr[j * WINDOW:(j + 2) * WINDOW, pair * 128:(pair + 1) * 128]
            st.append(_dot(kpair, rhs))
        st = jnp.concatenate(st, axis=1) + bias_ref[...]
        if j == 0:
            st = jnp.where(has_prev, st, -jnp.inf)
        mx = jnp.maximum(jnp.max(st, axis=0, keepdims=True), sink)
        p = jnp.exp2(st - mx).astype(BF16)
        sink_p = jnp.exp2(sink - mx)
        pieces = []
        for kv in range(A_KV):
            vt = jnp.concatenate([vt_scr[kv * A_HD:(kv + 1) * A_HD, j * WINDOW:(j + 2) * WINDOW], ones], axis=0)
            ot = _dot(vt, p[:, kv * lanes:(kv + 1) * lanes])
            den = ot[A_HD:A_HD + 1] + sink_p[:, kv * lanes:(kv + 1) * lanes]
            ot = ot[:A_HD] * (1.0 / den)
            pieces.extend(ot[:, g * WINDOW:(g + 1) * WINDOW] for g in range(A_GROUP))
        o_ref[j * WINDOW:(j + 1) * WINDOW, :] = jnp.concatenate(pieces, axis=0).T

    @pl.when(i == pl.num_programs(1) - 1)
    def _():
        rows = kc_ref.shape[0]
        pk_ref[0] = kc_ref[rows - WINDOW:rows, :].T
        pv_ref[0] = vc_ref[rows - WINDOW:rows, :].T


def _band_bias():
    c = jnp.arange(2 * WINDOW, dtype=jnp.int32)[:, None]
    t = jnp.arange(A_HEADS * WINDOW, dtype=jnp.int32)[None, :] % WINDOW
    return jnp.where((c > t) & (c <= t + WINDOW), 0.0, -jnp.inf).astype(F32)


def _swa_prompt(aq_t, pf, sinks, bias, batch, seq):
    tq = _pick(seq, (SWA_STEP, 256, 128))
    ns = seq // tq
    per = tq // WINDOW
    nb = seq // WINDOW
    cur = lambda col: (lambda b, i: (b * ns + i, col))
    prev = lambda col: (lambda b, i: (b * nb + jnp.maximum(i * per - 1, 0), col))
    return pl.pallas_call(
        _swa_prompt_kernel,
        grid=(batch, ns),
        in_specs=[
            pl.BlockSpec(memory_space=pltpu.SMEM),
            pl.BlockSpec(bias.shape, lambda b, i: (0, 0)),
            pl.BlockSpec((D_MODEL, tq), lambda b, i: (0, b * ns + i)),
            pl.BlockSpec((WINDOW, KV_W), prev(0)),
            pl.BlockSpec((tq, KV_W), cur(0)),
            pl.BlockSpec((WINDOW, KV_W), prev(1)),
            pl.BlockSpec((tq, KV_W), cur(1)),
        ],
        out_specs=[
            pl.BlockSpec((tq, D_MODEL), lambda b, i: (b * ns + i, 0)),
            pl.BlockSpec((1, KV_W, WINDOW), lambda b, i: (b, 0, 0)),
            pl.BlockSpec((1, KV_W, WINDOW), lambda b, i: (b, 0, 0)),
        ],
        out_shape=[
            jax.ShapeDtypeStruct((batch * seq, D_MODEL), F32),
            jax.ShapeDtypeStruct((batch, KV_W, WINDOW), F32),
            jax.ShapeDtypeStruct((batch, KV_W, WINDOW), F32),
        ],
        scratch_shapes=[pltpu.VMEM((WINDOW + tq, KV_W), BF16), pltpu.VMEM((KV_W, WINDOW + tq), BF16)],
        compiler_params=_params("parallel", "arbitrary"),
        name="swa_prompt",
    )(sinks, bias, aq_t, pf, pf, pf, pf)


def _bdot(a, b, contract_b):
    return lax.dot_general(a, b, (((2,), (contract_b,)), ((0,), (0,))), preferred_element_type=F32)


def _swa_sample_kernel(*refs, tq):
    sink_ref, q_ref, kn_ref, vn_ref, ck_ref, cv_ref = refs[:6]
    o_ref, sk_ref, sv_ref = refs[-3:]
    nb = ck_ref.shape[1]
    wc = ck_ref.shape[3]
    keep = wc - tq
    q3 = q_ref[...].astype(F32).reshape(nb, tq, D_MODEL)
    ck = ck_ref[0]
    cv = cv_ref[0]

    lane = lax.broadcasted_iota(jnp.int32, (nb, KV_W, wc), 2)

    def appended(cache, new_rows):
        new_t = new_rows.T
        placed = jnp.stack([pltpu.roll(new_t, (keep - b * tq) % wc, axis=1) for b in range(nb)])
        return jnp.where(lane >= keep, placed, pltpu.roll(cache, keep, axis=2))

    sk = appended(ck, kn_ref[...])
    sv = appended(cv, vn_ref[...])
    sk_ref[0] = sk
    sv_ref[0] = sv

    rows = A_GROUP * tq
    t_idx = lax.broadcasted_iota(jnp.int32, (1, rows, 1), 1) & (tq - 1)
    c_idx = lax.broadcasted_iota(jnp.int32, (1, 1, wc), 2)
    mask_old = c_idx > t_idx
    mask_new = (c_idx >= keep) & (c_idx - keep <= t_idx)
    g_idx = lax.broadcasted_iota(jnp.int32, (1, rows, 1), 1) >> (tq.bit_length() - 1)
    pieces = []
    for kv in range(A_KV):
        heads = [kv * A_GROUP + g for g in range(A_GROUP)]
        q4 = jnp.concatenate([q3[:, :, h * A_HD:(h + 1) * A_HD] for h in heads], axis=1).astype(BF16)
        sink = jnp.zeros((1, rows, 1), F32)
        for g in range(A_GROUP):
            sink = jnp.where(g_idx == g, sink_ref[heads[g]] * LOG2E, sink)
        sl = slice(kv * A_HD, (kv + 1) * A_HD)
        s1 = jnp.where(mask_old, _bdot(q4, ck[:, sl, :].astype(BF16), 1), -jnp.inf)
        s2 = jnp.where(mask_new, _bdot(q4, sk[:, sl, :].astype(BF16), 1), -jnp.inf)
        mx = jnp.maximum(jnp.maximum(jnp.max(s1, axis=2, keepdims=True),
                                     jnp.max(s2, axis=2, keepdims=True)), sink)
        p1 = jnp.exp2(s1 - mx)
        p2 = jnp.exp2(s2 - mx)
        den = (jnp.sum(p1, axis=2, keepdims=True) + jnp.sum(p2, axis=2, keepdims=True)
               + jnp.exp2(sink - mx))
        r = 1.0 / den
        o = (_bdot((p1 * r).astype(BF16), cv[:, sl, :].astype(BF16), 2)
             + _bdot((p2 * r).astype(BF16), sv[:, sl, :].astype(BF16), 2))
        pieces.extend(o[:, g * tq:(g + 1) * tq, :] for g in range(A_GROUP))
    o_ref[...] = jnp.concatenate(pieces, axis=2).reshape(nb * tq, D_MODEL)


def _swa_sample(aq, pf, sinks, cache_kt, cache_vt, carried, layer, dec_batch, tq):
    lt = SAMPLE_TILE
    nb = lt // tq
    wc = cache_kt.shape[3]
    assert wc == lt, "the appended keys are placed with lane rolls over one cache row"
    any_spec = pl.BlockSpec(memory_space=pl.ANY)
    cache_spec = pl.BlockSpec((1, nb, KV_W, wc), lambda i: (layer, i, 0, 0))
    operands = [sinks, aq, pf, pf, cache_kt, cache_vt]
    in_specs = [
        pl.BlockSpec(memory_space=pltpu.SMEM),
        pl.BlockSpec((lt, D_MODEL), lambda i: (i, 0)),
        pl.BlockSpec((lt, KV_W), lambda i: (i, 0)),
        pl.BlockSpec((lt, KV_W), lambda i: (i, 1)),
        cache_spec, cache_spec,
    ]
    aliases = {}
    if carried is not None:
        aliases = {len(operands) + k: 1 + k for k in range(2)}
        operands += list(carried)
        in_specs += [any_spec] * 2
    return pl.pallas_call(
        functools.partial(_swa_sample_kernel, tq=tq),
        grid=(dec_batch // nb,),
        in_specs=in_specs,
        out_specs=[
            pl.BlockSpec((lt, D_MODEL), lambda i: (i, 0)),
            cache_spec, cache_spec,
        ],
        out_shape=[
            jax.ShapeDtypeStruct((dec_batch * tq, D_MODEL), F32),
            jax.ShapeDtypeStruct(cache_kt.shape, F32),
            jax.ShapeDtypeStruct(cache_vt.shape, F32),
        ],
        input_output_aliases=aliases,
        compiler_params=_params("parallel"),
        name="swa_sample",
    )(*operands)


def _merge_kernel(x_ref, xn_ref, wg_ref, ba_ref, bb_ref, w_ref, o_ref):
    gates = _dot_nt(xn_ref[...], wg_ref[...])
    merged = _sigmoid(gates[:, :D_MODEL]) * ba_ref[...] + _sigmoid(gates[:, D_MODEL:]) * bb_ref[...]
    o_ref[...] = x_ref[...] + _dot(merged.astype(BF16), w_ref[...])


def _merge(x, xn, w_gates, ba, bb, w_out):
    n = x.shape[0]
    tm = _pick(n, (512, 256, 128))
    row = pl.BlockSpec((tm, D_MODEL), lambda i: (i, 0))
    const = lambda shape: pl.BlockSpec(shape, lambda i: (0, 0))
    return pl.pallas_call(
        _merge_kernel,
        grid=(n // tm,),
        in_specs=[row, row, const((2 * D_MODEL, D_MODEL)), row, row, const((D_MODEL, D_MODEL))],
        out_specs=row,
        out_shape=jax.ShapeDtypeStruct((n, D_MODEL), F32),
        compiler_params=_params("parallel"),
        name="merge_out_proj",
    )(x, xn, w_gates, ba, bb, w_out)


def _mlp_kernel(x_ref, g_ref, wu_ref, wd_ref, gf_ref, o_ref, xn_ref, *, final_norm):
    j = pl.program_id(1)

    @pl.when(j == 0)
    def _():
        x = x_ref[...]
        xn_ref[...] = _rms(x, g_ref[...]).astype(BF16)
        o_ref[...] = x

    h = jnp.square(jnp.maximum(_dot(xn_ref[...], wu_ref[...]), 0.0))
    o_ref[...] += _dot(h.astype(BF16), wd_ref[...])

    if final_norm:
        @pl.when(j == pl.num_programs(1) - 1)
        def _():
            o_ref[...] = _rms(o_ref[...], gf_ref[...])


def _mlp(x, g, w_up, w_down, g_final, final_norm):
    n = x.shape[0]
    tm = _pick(n, (1024, 512, 256, 128))
    tf = 1024
    return pl.pallas_call(
        functools.partial(_mlp_kernel, final_norm=final_norm),
        grid=(n // tm, D_FF // tf),
        in_specs=[
            pl.BlockSpec((tm, D_MODEL), lambda i, j: (i, 0)),
            pl.BlockSpec((1, D_MODEL), lambda i, j: (0, 0)),
            pl.BlockSpec((D_MODEL, tf), lambda i, j: (0, j)),
            pl.BlockSpec((tf, D_MODEL), lambda i, j: (j, 0)),
            pl.BlockSpec((1, D_MODEL), lambda i, j: (0, 0)),
        ],
        out_specs=pl.BlockSpec((tm, D_MODEL), lambda i, j: (i, 0)),
        out_shape=jax.ShapeDtypeStruct((n, D_MODEL), F32),
        scratch_shapes=[pltpu.VMEM((tm, D_MODEL), BF16)],
        compiler_params=_params("parallel", "arbitrary"),
        name="mlp",
    )(x, g, w_up, w_down, g_final)


def _prep_in_proj_weights(w_in, b_gate):
    sizes = (MQ_W, MQ_W, D_MODEL, D_MODEL, M_HEADS, M_HEADS, D_MODEL, KV_W, KV_W, D_MODEL, D_MODEL)
    offs = [0]
    for s in sizes:
        offs.append(offs[-1] + s)
    w_t = jnp.swapaxes(w_in, 1, 2).astype(BF16)
    mq, mk, mv, mo, mi, mf, aq, ak, av, ga, gb = (w_t[:, offs[i]:offs[i + 1]] for i in range(11))
    depth = w_in.shape[0]
    w_proj = jnp.concatenate([mv, aq, mq, ak, av], axis=1)
    pad = jnp.zeros((depth, GATE_ROWS - 2 * M_HEADS, D_MODEL), BF16)
    w_gate = jnp.concatenate([mi, mf, pad], axis=1)
    b_gate_col = jnp.zeros((depth, GATE_ROWS, 1), F32).at[:, :2 * M_HEADS, 0].set(b_gate.astype(F32))
    w_merge = jnp.concatenate([ga, gb], axis=1)
    scale = jnp.ones((1, D_PROJ), F32)
    scale = scale.at[:, PROJ_BLOCK:2 * PROJ_BLOCK].set(A_HD ** -0.5 * LOG2E)
    scale = scale.at[:, 2 * PROJ_BLOCK:2 * PROJ_BLOCK + MQ_W].set(M_DQK ** -0.5)
    return w_proj, mk, w_gate, b_gate_col, scale, mo, w_merge


def kernel(x_prompt, x_sample, state_C, state_n, state_m, cache_k, cache_v, norm_attn, w_in, b_gate,
           mlstm_norm, sinks, w_out, norm_mlp, w_up, w_down, norm_final):
    batch, seq, _ = x_prompt.shape
    dec_batch, dec_seq, _ = x_sample.shape
    depth = w_in.shape[0]
    wc = cache_k.shape[2]

    w_proj, w_mk, w_gate, b_gate_col, scale, w_mo, w_merge = _prep_in_proj_weights(w_in, b_gate)
    scale_t = scale.reshape(D_PROJ, 1)
    w_out_b = w_out.astype(BF16)
    w_up_b = w_up.astype(BF16)
    w_down_b = w_down.astype(BF16)
    bias = _band_bias()

    xp = x_prompt.reshape(batch * seq, D_MODEL)
    xs = x_sample.reshape(dec_batch * dec_seq, D_MODEL)
    m_tok = jnp.repeat(jnp.swapaxes(state_m, 1, 2), dec_seq, axis=2)[..., None]
    cache_kt = jnp.transpose(cache_k, (0, 1, 3, 4, 2)).reshape(depth, dec_batch, KV_W, wc)
    cache_vt = jnp.transpose(cache_v, (0, 1, 3, 4, 2)).reshape(depth, dec_batch, KV_W, wc)

    states = None
    caches = None
    p_c, p_n, p_m, p_k, p_v = [], [], [], [], []
    g_final = norm_final[None]

    for l in range(depth):
        g_attn = norm_attn[l][None]
        nw = mlstm_norm[l][None]
        g_mlp = norm_mlp[l][None]
        last = l == depth - 1
        proj_w = (w_proj[l], w_mk[l], w_gate[l], b_gate_col[l], scale, scale_t)

        xn, mv, mq, pf, kt, gt, aq_t = _in_proj(xp, g_attn, *proj_w, q_transposed=True)
        ba, pc, pn, pm = _mlstm_prompt(xn, w_mo[l], mv, mq, kt, gt, nw, batch, seq)
        bb, pk, pv = _swa_prompt(aq_t, pf, sinks[l], bias, batch, seq)
        xp = _merge(xp, xn, w_merge[l], ba, bb, w_out_b[l])
        xp = _mlp(xp, g_mlp, w_up_b[l], w_down_b[l], g_final, last)
        p_c.append(pc)
        p_n.append(pn)
        p_m.append(pm[:, :M_HEADS, 0])
        p_k.append(pk)
        p_v.append(pv)

        xn, mv, mq, pf, kt, gt, aq = _in_proj(xs, g_attn, *proj_w, q_transposed=False)
        ba, *states = _mlstm_sample(xn, w_mo[l], mv, mq, kt, gt, nw, m_tok, state_C, state_n, states,
                                    l, dec_batch, dec_seq)
        bb, *caches = _swa_sample(aq, pf, sinks[l], cache_kt, cache_vt, caches, l, dec_batch, dec_seq)
        xs = _merge(xs, xn, w_merge[l], ba, bb, w_out_b[l])
        xs = _mlp(xs, g_mlp, w_up_b[l], w_down_b[l], g_final, last)

    s_c, s_n, s_m = states
    s_k, s_v = caches

    def positions_major(t, lead):
        t = t.reshape(*lead, A_KV, A_HD, wc)
        return jnp.moveaxis(t, -1, -3)

    return (xp.reshape(batch, seq, D_MODEL), xs.reshape(dec_batch, dec_seq, D_MODEL),
            jnp.stack(p_c), jnp.stack(p_n), jnp.stack(p_m),
            positions_major(jnp.stack(p_k), (depth, batch)), positions_major(jnp.stack(p_v), (depth, batch)),
            s_c, s_n, jnp.swapaxes(s_m[..., 0], 1, 2),
            positions_major(s_k, (depth, dec_batch)), positions_major(s_v, (depth, dec_batch)))
```

```python
import functools

import jax
import jax.numpy as jnp
from jax import lax
from jax.experimental import pallas as pl
from jax.experimental.pallas import tpu as pltpu

F32 = jnp.float32
BF16 = jnp.bfloat16

D_MODEL = 1024
M_HEADS = 4
M_DQK = 128
M_DV = D_MODEL // M_HEADS
MQ_W = M_HEADS * M_DQK
A_HEADS = 16
A_KV = 4
A_GROUP = A_HEADS // A_KV
A_HD = D_MODEL // A_HEADS
KV_W = A_KV * A_HD
WINDOW = 128
D_FF = 4 * D_MODEL
EPS = 1e-6
LOG2E = 1.4426950408889634

PROJ_BLOCK = 1024
D_PROJ = 3 * PROJ_BLOCK
GATE_ROWS = 16

MLSTM_CHUNK = 256
SAMPLE_TILE = 128
SWA_STEP = 512
VMEM_LIMIT = 56 * 1024 * 1024


def _pick(n, candidates):
    for c in candidates:
        if n % c == 0:
            return c
    raise ValueError(f"no block size for {n}")


def _sigmoid(x):
    return 1.0 / (1.0 + jnp.exp(-x))


def _log_sigmoid(x):
    return jnp.minimum(x, 0.0) - jnp.log1p(jnp.exp(-jnp.abs(x)))


def _rms(x, g):
    y = x * lax.rsqrt(jnp.mean(x * x, axis=-1, keepdims=True) + EPS)
    return y * g


def _dot(a, b):
    return jnp.dot(a, b, preferred_element_type=F32)


def _dot_nt(a, b):
    return lax.dot_general(a, b, (((1,), (1,)), ((), ())), preferred_element_type=F32)


def _params(*sem):
    return pltpu.CompilerParams(dimension_semantics=sem, vmem_limit_bytes=VMEM_LIMIT)


def _in_proj_kernel(x_ref, g_ref, w_ref, wk_ref, wg_ref, bg_ref, scale_ref, scale_t_ref,
                    xn_out, mv_ref, mq_ref, pf_ref, kt_ref, gt_ref, aq_ref, *, q_transposed):
    xn = _rms(x_ref[...], g_ref[...]).astype(BF16)
    xn_out[...] = xn
    kt_ref[...] = _dot_nt(wk_ref[...], xn).astype(BF16)
    gt_ref[...] = _dot_nt(wg_ref[...], xn) + bg_ref[...]
    blk = lambda j: slice(j * PROJ_BLOCK, (j + 1) * PROJ_BLOCK)
    mv_ref[...] = (_dot_nt(xn, w_ref[blk(0), :]) * scale_ref[:, blk(0)]).astype(BF16)
    if q_transposed:
        aq_ref[...] = (_dot_nt(w_ref[blk(1), :], xn) * scale_t_ref[blk(1), :]).astype(BF16)
    else:
        aq_ref[...] = (_dot_nt(xn, w_ref[blk(1), :]) * scale_ref[:, blk(1)]).astype(BF16)
    acc = _dot_nt(xn, w_ref[blk(2), :]) * scale_ref[:, blk(2)]
    mq_ref[...] = acc[:, :MQ_W].astype(BF16)
    pf_ref[...] = acc[:, MQ_W:]


def _in_proj(x, g, w, wk, wg, bg, scale, scale_t, q_transposed):
    n = x.shape[0]
    tm = _pick(n, (512, 256, 128))
    const = lambda i: (0, 0)
    row = lambda i: (i, 0)
    col = lambda i: (0, i)
    if q_transposed:
        aq_spec, aq_shape = pl.BlockSpec((D_MODEL, tm), col), (D_MODEL, n)
    else:
        aq_spec, aq_shape = pl.BlockSpec((tm, D_MODEL), row), (n, D_MODEL)
    return pl.pallas_call(
        functools.partial(_in_proj_kernel, q_transposed=q_transposed),
        grid=(n // tm,),
        in_specs=[
            pl.BlockSpec((tm, D_MODEL), row),
            pl.BlockSpec((1, D_MODEL), const),
            pl.BlockSpec((D_PROJ, D_MODEL), const),
            pl.BlockSpec((MQ_W, D_MODEL), const),
            pl.BlockSpec((GATE_ROWS, D_MODEL), const),
            pl.BlockSpec((GATE_ROWS, 1), const),
            pl.BlockSpec((1, D_PROJ), const),
            pl.BlockSpec((D_PROJ, 1), const),
        ],
        out_specs=[
            pl.BlockSpec((tm, D_MODEL), row),
            pl.BlockSpec((tm, D_MODEL), row),
            pl.BlockSpec((tm, MQ_W), row),
            pl.BlockSpec((tm, 2 * KV_W), row),
            pl.BlockSpec((MQ_W, tm), col),
            pl.BlockSpec((GATE_ROWS, tm), col),
            aq_spec,
        ],
        out_shape=[
            jax.ShapeDtypeStruct((n, D_MODEL), BF16),
            jax.ShapeDtypeStruct((n, D_MODEL), BF16),
            jax.ShapeDtypeStruct((n, MQ_W), BF16),
            jax.ShapeDtypeStruct((n, 2 * KV_W), F32),
            jax.ShapeDtypeStruct((MQ_W, n), BF16),
            jax.ShapeDtypeStruct((GATE_ROWS, n), F32),
            jax.ShapeDtypeStruct(aq_shape, BF16),
        ],
        compiler_params=_params("parallel"),
        name="in_proj",
    )(x, g, w, wk, wg, bg, scale, scale_t)


def _seg_cumsum(x, seg):
    pos = lax.broadcasted_iota(jnp.int32, x.shape, 1) & (seg - 1)
    sh = 1
    while sh < seg:
        x = x + jnp.where(pos >= sh, pltpu.roll(x, sh, axis=1), 0.0)
        sh *= 2
    return x


def _mlstm_finish(num, inv, o_pre, nw):
    msq = jnp.mean(num * num, axis=-1, keepdims=True)
    r = inv * lax.rsqrt(inv * inv * msq + EPS)
    return _sigmoid(o_pre) * (num * r) * nw


def _mlstm_prompt_kernel(xn_ref, wo_ref, q_ref, kt_ref, v_ref, gt_ref, nw_ref,
                         h_ref, c_out, n_out, m_out, c_scr, n_scr, m_scr):
    c = pl.program_id(1)
    lt = q_ref.shape[0]
    heads = range(M_HEADS)

    @pl.when(c == 0)
    def _():
        c_scr[...] = jnp.zeros_like(c_scr)
        n_scr[...] = jnp.zeros_like(n_scr)
        m_scr[...] = jnp.zeros_like(m_scr)

    gt = gt_ref[...]
    i2 = gt * LOG2E
    b2 = _seg_cumsum(_log_sigmoid(gt), lt) * LOG2E
    b2_t = jnp.concatenate([b2, jnp.zeros((128 - GATE_ROWS, lt), F32)], axis=0).T
    t_idx = lax.broadcasted_iota(jnp.int32, (lt, lt), 0)
    s_idx = lax.broadcasted_iota(jnp.int32, (lt, lt), 1)
    causal = s_idx <= t_idx

    q = [q_ref[:, h * M_DQK:(h + 1) * M_DQK] for h in heads]
    kt = [kt_ref[h * M_DQK:(h + 1) * M_DQK, :] for h in heads]
    v = [v_ref[:, h * M_DV:(h + 1) * M_DV] for h in heads]
    m2_col = [jnp.broadcast_to(m_scr[h:h + 1, 0:1] * LOG2E, (lt, 1)) for h in heads]
    n_old = [n_scr[h:h + 1, :] for h in heads]
    c_old = [c_scr[h] for h in heads]

    a_mat = [jnp.where(causal, i2[h:h + 1] - b2[M_HEADS + h:M_HEADS + h + 1], -jnp.inf) for h in heads]
    g_col = [jnp.maximum(jnp.max(a_mat[h], axis=1, keepdims=True), m2_col[h]) for h in heads]
    w = [jnp.exp2(a_mat[h] - g_col[h]) for h in heads]
    w_int = [jnp.exp2(m2_col[h] - g_col[h]) for h in heads]
    s = [_dot(q[h], kt[h]) * w[h] for h in heads]
    inter = [_dot(q[h], c_old[h].astype(BF16)) for h in heads]
    qn = [jnp.sum(q[h].astype(F32) * n_old[h], axis=1, keepdims=True) for h in heads]
    num = [w_int[h] * inter[h] + _dot(s[h].astype(BF16), v[h]) for h in heads]
    den = [w_int[h] * qn[h] + jnp.sum(s[h], axis=1, keepdims=True) for h in heads]
    m2_row = [b2_t[:, M_HEADS + h:M_HEADS + h + 1] + g_col[h] for h in heads]
    inv = [1.0 / jnp.maximum(jnp.abs(den[h]), jnp.exp2(-m2_row[h])) for h in heads]

    wk_row = [w[h][lt - 1:lt, :] for h in heads]
    c_upd = [_dot((kt[h].astype(F32) * wk_row[h]).astype(BF16), v[h]) for h in heads]
    n_upd = [_dot_nt(jnp.broadcast_to(wk_row[h], (8, lt)).astype(BF16), kt[h]) for h in heads]
    for h in heads:
        decay = w_int[h][lt - 1:lt]
        c_scr[h] = decay * c_old[h] + c_upd[h]
        n_scr[h:h + 1, :] = decay * n_old[h] + n_upd[h][0:1]
        m_scr[h:h + 1, :] = jnp.broadcast_to(m2_row[h][lt - 1:lt] * (1.0 / LOG2E), (1, m_scr.shape[1]))

    o_pre = _dot_nt(xn_ref[...], wo_ref[...])
    for h in heads:
        sl = slice(h * M_DV, (h + 1) * M_DV)
        h_ref[:, sl] = _mlstm_finish(num[h], inv[h], o_pre[:, sl], nw_ref[:, sl])

    @pl.when(c == pl.num_programs(1) - 1)
    def _():
        c_out[0] = c_scr[...]
        n_out[0] = n_scr[0:M_HEADS]
        m_out[0] = m_scr[...]


def _mlstm_prompt(xn, w_mo, mv, mq, kt, gt, nw, batch, seq):
    lt = _pick(seq, (MLSTM_CHUNK, 128))
    nc = seq // lt
    row = lambda b, c: (b * nc + c, 0)
    col = lambda b, c: (0, b * nc + c)
    const = lambda b, c: (0, 0)
    return pl.pallas_call(
        _mlstm_prompt_kernel,
        grid=(batch, nc),
        in_specs=[
            pl.BlockSpec((lt, D_MODEL), row),
            pl.BlockSpec((D_MODEL, D_MODEL), const),
            pl.BlockSpec((lt, MQ_W), row),
            pl.BlockSpec((MQ_W, lt), col),
            pl.BlockSpec((lt, D_MODEL), row),
            pl.BlockSpec((GATE_ROWS, lt), col),
            pl.BlockSpec((1, D_MODEL), const),
        ],
        out_specs=[
            pl.BlockSpec((lt, D_MODEL), row),
            pl.BlockSpec((1, M_HEADS, M_DQK, M_DV), lambda b, c: (b, 0, 0, 0)),
            pl.BlockSpec((1, M_HEADS, M_DQK), lambda b, c: (b, 0, 0)),
            pl.BlockSpec((1, 8, 128), lambda b, c: (b, 0, 0)),
        ],
        out_shape=[
            jax.ShapeDtypeStruct((batch * seq, D_MODEL), F32),
            jax.ShapeDtypeStruct((batch, M_HEADS, M_DQK, M_DV), F32),
            jax.ShapeDtypeStruct((batch, M_HEADS, M_DQK), F32),
            jax.ShapeDtypeStruct((batch, 8, 128), F32),
        ],
        scratch_shapes=[pltpu.VMEM((M_HEADS, M_DQK, M_DV), F32), pltpu.VMEM((8, M_DQK), F32),
                        pltpu.VMEM((8, 128), F32)],
        compiler_params=_params("parallel", "arbitrary"),
        name="mlstm_prompt",
    )(xn, w_mo, mq, kt, mv, gt, nw)


def _mlstm_sample_kernel(*refs, seg):
    xn_ref, wo_ref, q_ref, kt_ref, v_ref, gt_ref, nw_ref, m_ref, c_in, n_in = refs[:10]
    h_ref, c_out, n_out, m_out = refs[-4:]
    hd = pl.program_id(1)
    lt = q_ref.shape[0]
    nseg = lt // seg
    gt = gt_ref[...]
    lf = _log_sigmoid(gt)
    bcs = _seg_cumsum(lf, seg)
    row = lax.broadcasted_iota(jnp.int32, (GATE_ROWS, lt), 0)

    def pick(x, r):
        return jnp.sum(jnp.where(row == r, x, 0.0), axis=0, keepdims=True)

    i_row, lf_row, b_row = pick(gt, hd), pick(lf, M_HEADS + hd), pick(bcs, M_HEADS + hd)
    t_idx = lax.broadcasted_iota(jnp.int32, (lt, lt), 0)
    s_idx = lax.broadcasted_iota(jnp.int32, (lt, lt), 1)
    causal = (s_idx <= t_idx) & ((s_idx | (seg - 1)) == (t_idx | (seg - 1)))
    last = t_idx == (s_idx | (seg - 1))

    q = q_ref[...]
    kt = kt_ref[...]
    v = v_ref[...]
    m_col = m_ref[0, 0]
    n_old = [n_in[0, i, pl.ds(hd, 1), :] for i in range(nseg)]
    n_tok = jnp.concatenate([jnp.broadcast_to(n, (seg, M_DQK)) for n in n_old], axis=0)

    a_mat = jnp.where(causal, i_row - b_row, -jnp.inf)
    g_col = jnp.maximum(jnp.max(a_mat, axis=1, keepdims=True), m_col)
    b_col = jnp.sum(jnp.where(causal, lf_row, 0.0), axis=1, keepdims=True)
    w = jnp.exp(a_mat - g_col)
    w_int = jnp.exp(m_col - g_col)
    s = _dot(q, kt) * w
    inter = jnp.concatenate(
        [_dot(q[i * seg:(i + 1) * seg], c_in[0, i, 0].astype(BF16)) for i in range(nseg)], axis=0)
    qn = jnp.sum(q.astype(F32) * n_tok, axis=1, keepdims=True)
    num = w_int * inter + _dot(s.astype(BF16), v)
    den = w_int * qn + jnp.sum(s, axis=1, keepdims=True)
    m_row = b_col + g_col
    inv = 1.0 / jnp.maximum(jnp.abs(den), jnp.exp(-m_row))

    wk_row = jnp.sum(jnp.where(last, w, 0.0), axis=0, keepdims=True)
    ktw = kt.astype(F32) * wk_row
    lane_seg = lax.broadcasted_iota(jnp.int32, (1, lt), 1) | (seg - 1)
    lhs = jnp.concatenate(
        [jnp.where(lane_seg == i * seg + seg - 1, ktw, 0.0).astype(BF16) for i in range(nseg)], axis=0)
    c_upd = _dot(lhs, v)
    seg_row = lax.broadcasted_iota(jnp.int32, (lt, 1), 0) * seg + (seg - 1)
    n_upd = _dot_nt(jnp.where(lane_seg == seg_row, wk_row, 0.0).astype(BF16), kt)
    new_m = []
    for i in range(nseg):
        r = i * seg + seg - 1
        decay = w_int[r:r + 1]
        c_out[0, i, 0] = decay * c_in[0, i, 0] + c_upd[i * M_DQK:(i + 1) * M_DQK]
        n_out[0, i, pl.ds(hd, 1), :] = decay * n_old[i] + n_upd[i:i + 1]
        new_m.append(m_row[r:r + 1])
    m_out[0, 0] = jnp.concatenate(new_m, axis=0)
    h_ref[...] = _mlstm_finish(num, inv, _dot_nt(xn_ref[...], wo_ref[...]), nw_ref[...])


def _mlstm_sample(xn, w_mo, mv, mq, kt, gt, nw, m_tok, state_c, state_n, carried, layer, dec_batch, seg):
    lt = SAMPLE_TILE
    nseg = lt // seg
    depth = state_c.shape[0]
    any_spec = pl.BlockSpec(memory_space=pl.ANY)
    c_spec = pl.BlockSpec((1, nseg, 1, M_DQK, M_DV), lambda t, h: (layer, t, h, 0, 0))
    n_spec = pl.BlockSpec((1, nseg, M_HEADS, M_DQK), lambda t, h: (layer, t, 0, 0))
    operands = [xn, w_mo, mq, kt, mv, gt, nw, m_tok, state_c, state_n]
    in_specs = [
        pl.BlockSpec((lt, D_MODEL), lambda t, h: (t, 0)),
        pl.BlockSpec((M_DV, D_MODEL), lambda t, h: (h, 0)),
        pl.BlockSpec((lt, M_DQK), lambda t, h: (t, h)),
        pl.BlockSpec((M_DQK, lt), lambda t, h: (h, t)),
        pl.BlockSpec((lt, M_DV), lambda t, h: (t, h)),
        pl.BlockSpec((GATE_ROWS, lt), lambda t, h: (0, t)),
        pl.BlockSpec((1, M_DV), lambda t, h: (0, h)),
        pl.BlockSpec((1, 1, lt, 1), lambda t, h: (layer, h, t, 0)),
        c_spec, n_spec,
    ]
    aliases = {}
    if carried is not None:
        aliases = {len(operands) + k: 1 + k for k in range(3)}
        operands += list(carried)
        in_specs += [any_spec] * 3
    return pl.pallas_call(
        functools.partial(_mlstm_sample_kernel, seg=seg),
        grid=(dec_batch // nseg, M_HEADS),
        in_specs=in_specs,
        out_specs=[
            pl.BlockSpec((lt, M_DV), lambda t, h: (t, h)),
            c_spec, n_spec,
            pl.BlockSpec((1, 1, nseg, 1), lambda t, h: (layer, h, t, 0)),
        ],
        out_shape=[
            jax.ShapeDtypeStruct((dec_batch * seg, D_MODEL), F32),
            jax.ShapeDtypeStruct(state_c.shape, F32),
            jax.ShapeDtypeStruct(state_n.shape, F32),
            jax.ShapeDtypeStruct((depth, M_HEADS, dec_batch, 1), F32),
        ],
        input_output_aliases=aliases,
        compiler_params=_params("parallel", "arbitrary"),
        name="mlstm_sample",
    )(*operands)


def _swa_merge_prompt_kernel(sink_ref, bias_ref, qt_ref, kp_ref, kc_ref, vp_ref, vc_ref,
                             x_ref, xn_ref, ba_ref, wg_ref, w_ref,
                             o_ref, pk_ref, pv_ref, kb_scr, vt_scr, bb_scr):
    i = pl.program_id(1)
    nsub = qt_ref.shape[1] // WINDOW
    lanes = A_GROUP * WINDOW
    all_lanes = A_HEADS * WINDOW
    gates = _dot_nt(xn_ref[...], wg_ref[...])
    gated_a = _sigmoid(gates[:, :D_MODEL]) * ba_ref[...]
    gate_b = _sigmoid(gates[:, D_MODEL:])

    kb_scr[0:WINDOW] = kp_ref[...].astype(BF16)
    kb_scr[WINDOW:] = kc_ref[...].astype(BF16)
    vt_scr[:, 0:WINDOW] = vp_ref[...].T.astype(BF16)
    vt_scr[:, WINDOW:] = vc_ref[...].T.astype(BF16)

    lane_head = lax.broadcasted_iota(jnp.int32, (1, all_lanes), 1) >> (WINDOW.bit_length() - 1)
    sink = jnp.zeros((1, all_lanes), F32)
    for h in range(A_HEADS):
        sink = jnp.where(lane_head == h, sink_ref[h] * LOG2E, sink)
    has_prev = (lax.broadcasted_iota(jnp.int32, (2 * WINDOW, 1), 0) >= WINDOW) | (i > 0)
    zeros = jnp.zeros((A_HD, lanes), BF16)
    ones = jnp.ones((16, 2 * WINDOW), BF16)

    for j in range(nsub):
        qt = qt_ref[:, j * WINDOW:(j + 1) * WINDOW]
        qt4 = [jnp.concatenate([qt[(kv * A_GROUP + g) * A_HD:(kv * A_GROUP + g + 1) * A_HD]
                                for g in range(A_GROUP)], axis=1) for kv in range(A_KV)]
        st = []
        for pair in range(A_KV // 2):
            rhs = jnp.concatenate([jnp.concatenate([qt4[2 * pair], zeros], axis=1),
                                   jnp.concatenate([zeros, qt4[2 * pair + 1]], axis=1)], axis=0)
            kpair = kb_scr[j * WINDOW:(j + 2) * WINDOW, pair * 128:(pair + 1) * 128]
            st.append(_dot(kpair, rhs))
        st = jnp.concatenate(st, axis=1) + bias_ref[...]
        if j == 0:
            st = jnp.where(has_prev, st, -jnp.inf)
        mx = jnp.maximum(jnp.max(st, axis=0, keepdims=True), sink)
        p = jnp.exp2(st - mx).astype(BF16)
        sink_p = jnp.exp2(sink - mx)
        pieces = []
        for kv in range(A_KV):
            vt = jnp.concatenate([vt_scr[kv * A_HD:(kv + 1) * A_HD, j * WINDOW:(j + 2) * WINDOW], ones], axis=0)
            ot = _dot(vt, p[:, kv * lanes:(kv + 1) * lanes])
            den = ot[A_HD:A_HD + 1] + sink_p[:, kv * lanes:(kv + 1) * lanes]
            ot = ot[:A_HD] * (1.0 / den)
            pieces.extend(ot[:, g * WINDOW:(g + 1) * WINDOW] for g in range(A_GROUP))
        bb_scr[j * WINDOW:(j + 1) * WINDOW, :] = jnp.concatenate(pieces, axis=0).T

    merged = gated_a + gate_b * bb_scr[...]
    o_ref[...] = x_ref[...] + _dot(merged.astype(BF16), w_ref[...])

    @pl.when(i == pl.num_programs(1) - 1)
    def _():
        rows = kc_ref.shape[0]
        pk_ref[0] = kc_ref[rows - WINDOW:rows, :].T
        pv_ref[0] = vc_ref[rows - WINDOW:rows, :].T


def _band_bias():
    c = jnp.arange(2 * WINDOW, dtype=jnp.int32)[:, None]
    t = jnp.arange(A_HEADS * WINDOW, dtype=jnp.int32)[None, :] % WINDOW
    return jnp.where((c > t) & (c <= t + WINDOW), 0.0, -jnp.inf).astype(F32)


def _swa_merge_prompt(aq_t, pf, sinks, bias, x, xn, ba, w_gates, w_out, batch, seq):
    tq = _pick(seq, (SWA_STEP, 256, 128))
    ns = seq // tq
    per = tq // WINDOW
    nb = seq // WINDOW
    cur = lambda col: (lambda b, i: (b * ns + i, col))
    prev = lambda col: (lambda b, i: (b * nb + jnp.maximum(i * per - 1, 0), col))
    const = lambda b, i: (0, 0)
    row = pl.BlockSpec((tq, D_MODEL), cur(0))
    return pl.pallas_call(
        _swa_merge_prompt_kernel,
        grid=(batch, ns),
        in_specs=[
            pl.BlockSpec(memory_space=pltpu.SMEM),
            pl.BlockSpec(bias.shape, const),
            pl.BlockSpec((D_MODEL, tq), lambda b, i: (0, b * ns + i)),
            pl.BlockSpec((WINDOW, KV_W), prev(0)),
            pl.BlockSpec((tq, KV_W), cur(0)),
            pl.BlockSpec((WINDOW, KV_W), prev(1)),
            pl.BlockSpec((tq, KV_W), cur(1)),
            row, row, row,
            pl.BlockSpec((2 * D_MODEL, D_MODEL), const),
            pl.BlockSpec((D_MODEL, D_MODEL), const),
        ],
        out_specs=[
            pl.BlockSpec((tq, D_MODEL), lambda b, i: (b * ns + i, 0)),
            pl.BlockSpec((1, KV_W, WINDOW), lambda b, i: (b, 0, 0)),
            pl.BlockSpec((1, KV_W, WINDOW), lambda b, i: (b, 0, 0)),
        ],
        out_shape=[
            jax.ShapeDtypeStruct((batch * seq, D_MODEL), F32),
            jax.ShapeDtypeStruct((batch, KV_W, WINDOW), F32),
            jax.ShapeDtypeStruct((batch, KV_W, WINDOW), F32),
        ],
        scratch_shapes=[pltpu.VMEM((WINDOW + tq, KV_W), BF16), pltpu.VMEM((KV_W, WINDOW + tq), BF16),
                        pltpu.VMEM((tq, D_MODEL), F32)],
        compiler_params=_params("parallel", "arbitrary"),
        name="swa_merge_prompt",
    )(sinks, bias, aq_t, pf, pf, pf, pf, x, xn, ba, w_gates, w_out)


def _bdot(a, b, contract_b):
    return lax.dot_general(a, b, (((2,), (contract_b,)), ((0,), (0,))), preferred_element_type=F32)


def _swa_sample_kernel(*refs, tq):
    sink_ref, q_ref, kn_ref, vn_ref, ck_ref, cv_ref = refs[:6]
    o_ref, sk_ref, sv_ref = refs[-3:]
    nb = ck_ref.shape[1]
    wc = ck_ref.shape[3]
    keep = wc - tq
    q3 = q_ref[...].astype(F32).reshape(nb, tq, D_MODEL)
    ck = ck_ref[0]
    cv = cv_ref[0]

    lane = lax.broadcasted_iota(jnp.int32, (nb, KV_W, wc), 2)

    def appended(cache, new_rows):
        new_t = new_rows.T
        placed = jnp.stack([pltpu.roll(new_t, (keep - b * tq) % wc, axis=1) for b in range(nb)])
        return jnp.where(lane >= keep, placed, pltpu.roll(cache, keep, axis=2))

    sk = appended(ck, kn_ref[...])
    sv = appended(cv, vn_ref[...])
    sk_ref[0] = sk
    sv_ref[0] = sv

    rows = A_GROUP * tq
    t_idx = lax.broadcasted_iota(jnp.int32, (1, rows, 1), 1) & (tq - 1)
    c_idx = lax.broadcasted_iota(jnp.int32, (1, 1, wc), 2)
    mask_old = c_idx > t_idx
    mask_new = (c_idx >= keep) & (c_idx - keep <= t_idx)
    g_idx = lax.broadcasted_iota(jnp.int32, (1, rows, 1), 1) >> (tq.bit_length() - 1)
    pieces = []
    for kv in range(A_KV):
        heads = [kv * A_GROUP + g for g in range(A_GROUP)]
        q4 = jnp.concatenate([q3[:, :, h * A_HD:(h + 1) * A_HD] for h in heads], axis=1).astype(BF16)
        sink = jnp.zeros((1, rows, 1), F32)
        for g in range(A_GROUP):
            sink = jnp.where(g_idx == g, sink_ref[heads[g]] * LOG2E, sink)
        sl = slice(kv * A_HD, (kv + 1) * A_HD)
        s1 = jnp.where(mask_old, _bdot(q4, ck[:, sl, :].astype(BF16), 1), -jnp.inf)
        s2 = jnp.where(mask_new, _bdot(q4, sk[:, sl, :].astype(BF16), 1), -jnp.inf)
        mx = jnp.maximum(jnp.maximum(jnp.max(s1, axis=2, keepdims=True),
                                     jnp.max(s2, axis=2, keepdims=True)), sink)
        p1 = jnp.exp2(s1 - mx)
        p2 = jnp.exp2(s2 - mx)
        den = (jnp.sum(p1, axis=2, keepdims=True) + jnp.sum(p2, axis=2, keepdims=True)
               + jnp.exp2(sink - mx))
        r = 1.0 / den
        o = (_bdot((p1 * r).astype(BF16), cv[:, sl, :].astype(BF16), 2)
             + _bdot((p2 * r).astype(BF16), sv[:, sl, :].astype(BF16), 2))
        pieces.extend(o[:, g * tq:(g + 1) * tq, :] for g in range(A_GROUP))
    o_ref[...] = jnp.concatenate(pieces, axis=2).reshape(nb * tq, D_MODEL)


def _swa_sample(aq, pf, sinks, cache_kt, cache_vt, carried, layer, dec_batch, tq):
    lt = SAMPLE_TILE
    nb = lt // tq
    wc = cache_kt.shape[3]
    assert wc == lt, "the appended keys are placed with lane rolls over one cache row"
    any_spec = pl.BlockSpec(memory_space=pl.ANY)
    cache_spec = pl.BlockSpec((1, nb, KV_W, wc), lambda i: (layer, i, 0, 0))
    operands = [sinks, aq, pf, pf, cache_kt, cache_vt]
    in_specs = [
        pl.BlockSpec(memory_space=pltpu.SMEM),
        pl.BlockSpec((lt, D_MODEL), lambda i: (i, 0)),
        pl.BlockSpec((lt, KV_W), lambda i: (i, 0)),
        pl.BlockSpec((lt, KV_W), lambda i: (i, 1)),
        cache_spec, cache_spec,
    ]
    aliases = {}
    if carried is not None:
        aliases = {len(operands) + k: 1 + k for k in range(2)}
        operands += list(carried)
        in_specs += [any_spec] * 2
    return pl.pallas_call(
        functools.partial(_swa_sample_kernel, tq=tq),
        grid=(dec_batch // nb,),
        in_specs=in_specs,
        out_specs=[
            pl.BlockSpec((lt, D_MODEL), lambda i: (i, 0)),
            cache_spec, cache_spec,
        ],
        out_shape=[
            jax.ShapeDtypeStruct((dec_batch * tq, D_MODEL), F32),
            jax.ShapeDtypeStruct(cache_kt.shape, F32),
            jax.ShapeDtypeStruct(cache_vt.shape, F32),
        ],
        input_output_aliases=aliases,
        compiler_params=_params("parallel"),
        name="swa_sample",
    )(*operands)


def _merge_kernel(x_ref, xn_ref, wg_ref, ba_ref, bb_ref, w_ref, o_ref):
    gates = _dot_nt(xn_ref[...], wg_ref[...])
    merged = _sigmoid(gates[:, :D_MODEL]) * ba_ref[...] + _sigmoid(gates[:, D_MODEL:]) * bb_ref[...]
    o_ref[...] = x_ref[...] + _dot(merged.astype(BF16), w_ref[...])


def _merge(x, xn, w_gates, ba, bb, w_out):
    n = x.shape[0]
    tm = _pick(n, (512, 256, 128))
    row = pl.BlockSpec((tm, D_MODEL), lambda i: (i, 0))
    const = lambda shape: pl.BlockSpec(shape, lambda i: (0, 0))
    return pl.pallas_call(
        _merge_kernel,
        grid=(n // tm,),
        in_specs=[row, row, const((2 * D_MODEL, D_MODEL)), row, row, const((D_MODEL, D_MODEL))],
        out_specs=row,
        out_shape=jax.ShapeDtypeStruct((n, D_MODEL), F32),
        compiler_params=_params("parallel"),
        name="merge_out_proj",
    )(x, xn, w_gates, ba, bb, w_out)


def _mlp_kernel(x_ref, g_ref, wu_ref, wd_ref, gf_ref, o_ref, xn_ref, *, final_norm):
    j = pl.program_id(1)

    @pl.when(j == 0)
    def _():
        x = x_ref[...]
        xn_ref[...] = _rms(x, g_ref[...]).astype(BF16)
        o_ref[...] = x

    h = jnp.square(jnp.maximum(_dot(xn_ref[...], wu_ref[...]), 0.0))
    o_ref[...] += _dot(h.astype(BF16), wd_ref[...])

    if final_norm:
        @pl.when(j == pl.num_programs(1) - 1)
        def _():
            o_ref[...] = _rms(o_ref[...], gf_ref[...])


def _mlp(x, g, w_up, w_down, g_final, final_norm):
    n = x.shape[0]
    tm = _pick(n, (1024, 512, 256, 128))
    tf = 2048
    return pl.pallas_call(
        functools.partial(_mlp_kernel, final_norm=final_norm),
        grid=(n // tm, D_FF // tf),
        in_specs=[
            pl.BlockSpec((tm, D_MODEL), lambda i, j: (i, 0)),
            pl.BlockSpec((1, D_MODEL), lambda i, j: (0, 0)),
            pl.BlockSpec((D_MODEL, tf), lambda i, j: (0, j)),
            pl.BlockSpec((tf, D_MODEL), lambda i, j: (j, 0)),
            pl.BlockSpec((1, D_MODEL), lambda i, j: (0, 0)),
        ],
        out_specs=pl.BlockSpec((tm, D_MODEL), lambda i, j: (i, 0)),
        out_shape=jax.ShapeDtypeStruct((n, D_MODEL), F32),
        scratch_shapes=[pltpu.VMEM((tm, D_MODEL), BF16)],
        compiler_params=_params("parallel", "arbitrary"),
        name="mlp",
    )(x, g, w_up, w_down, g_final)


def _prep_in_proj_weights(w_in, b_gate):
    sizes = (MQ_W, MQ_W, D_MODEL, D_MODEL, M_HEADS, M_HEADS, D_MODEL, KV_W, KV_W, D_MODEL, D_MODEL)
    offs = [0]
    for s in sizes:
        offs.append(offs[-1] + s)
    w_t = jnp.swapaxes(w_in, 1, 2).astype(BF16)
    mq, mk, mv, mo, mi, mf, aq, ak, av, ga, gb = (w_t[:, offs[i]:offs[i + 1]] for i in range(11))
    depth = w_in.shape[0]
    w_proj = jnp.concatenate([mv, aq, mq, ak, av], axis=1)
    pad = jnp.zeros((depth, GATE_ROWS - 2 * M_HEADS, D_MODEL), BF16)
    w_gate = jnp.concatenate([mi, mf, pad], axis=1)
    b_gate_col = jnp.zeros((depth, GATE_ROWS, 1), F32).at[:, :2 * M_HEADS, 0].set(b_gate.astype(F32))
    w_merge = jnp.concatenate([ga, gb], axis=1)
    scale = jnp.ones((1, D_PROJ), F32)
    scale = scale.at[:, PROJ_BLOCK:2 * PROJ_BLOCK].set(A_HD ** -0.5 * LOG2E)
    scale = scale.at[:, 2 * PROJ_BLOCK:2 * PROJ_BLOCK + MQ_W].set(M_DQK ** -0.5)
    return w_proj, mk, w_gate, b_gate_col, scale, mo, w_merge


def kernel(x_prompt, x_sample, state_C, state_n, state_m, cache_k, cache_v, norm_attn, w_in, b_gate,
           mlstm_norm, sinks, w_out, norm_mlp, w_up, w_down, norm_final):
    batch, seq, _ = x_prompt.shape
    dec_batch, dec_seq, _ = x_sample.shape
    depth = w_in.shape[0]
    wc = cache_k.shape[2]

    w_proj, w_mk, w_gate, b_gate_col, scale, w_mo, w_merge = _prep_in_proj_weights(w_in, b_gate)
    scale_t = scale.reshape(D_PROJ, 1)
    w_out_b = w_out.astype(BF16)
    w_up_b = w_up.astype(BF16)
    w_down_b = w_down.astype(BF16)
    bias = _band_bias()

    xp = x_prompt.reshape(batch * seq, D_MODEL)
    xs = x_sample.reshape(dec_batch * dec_seq, D_MODEL)
    m_tok = jnp.repeat(jnp.swapaxes(state_m, 1, 2), dec_seq, axis=2)[..., None]
    cache_kt = jnp.transpose(cache_k, (0, 1, 3, 4, 2)).reshape(depth, dec_batch, KV_W, wc)
    cache_vt = jnp.transpose(cache_v, (0, 1, 3, 4, 2)).reshape(depth, dec_batch, KV_W, wc)

    states = None
    caches = None
    p_c, p_n, p_m, p_k, p_v = [], [], [], [], []
    g_final = norm_final[None]

    for l in range(depth):
        g_attn = norm_attn[l][None]
        nw = mlstm_norm[l][None]
        g_mlp = norm_mlp[l][None]
        last = l == depth - 1
        proj_w = (w_proj[l], w_mk[l], w_gate[l], b_gate_col[l], scale, scale_t)

        xn, mv, mq, pf, kt, gt, aq_t = _in_proj(xp, g_attn, *proj_w, q_transposed=True)
        ba, pc, pn, pm = _mlstm_prompt(xn, w_mo[l], mv, mq, kt, gt, nw, batch, seq)
        xp, pk, pv = _swa_merge_prompt(aq_t, pf, sinks[l], bias, xp, xn, ba, w_merge[l], w_out_b[l], batch, seq)
        xp = _mlp(xp, g_mlp, w_up_b[l], w_down_b[l], g_final, last)
        p_c.append(pc)
        p_n.append(pn)
        p_m.append(pm[:, :M_HEADS, 0])
        p_k.append(pk)
        p_v.append(pv)

        xn, mv, mq, pf, kt, gt, aq = _in_proj(xs, g_attn, *proj_w, q_transposed=False)
        ba, *states = _mlstm_sample(xn, w_mo[l], mv, mq, kt, gt, nw, m_tok, state_C, state_n, states,
                                    l, dec_batch, dec_seq)
        bb, *caches = _swa_sample(aq, pf, sinks[l], cache_kt, cache_vt, caches, l, dec_batch, dec_seq)
        xs = _merge(xs, xn, w_merge[l], ba, bb, w_out_b[l])
        xs = _mlp(xs, g_mlp, w_up_b[l], w_down_b[l], g_final, last)

    s_c, s_n, s_m = states
    s_k, s_v = caches

    def positions_major(t, lead):
        t = t.reshape(*lead, A_KV, A_HD, wc)
        return jnp.moveaxis(t, -1, -3)

    return (xp.reshape(batch, seq, D_MODEL), xs.reshape(dec_batch, dec_seq, D_MODEL),
            jnp.stack(p_c), jnp.stack(p_n), jnp.stack(p_m),
            positions_major(jnp.stack(p_k), (depth, batch)), positions_major(jnp.stack(p_v), (depth, batch)),
            s_c, s_n, jnp.swapaxes(s_m[..., 0], 1, 2),
            positions_major(s_k, (depth, dec_batch)), positions_major(s_v, (depth, dec_batch)))
```

```python
import functools

import jax
import jax.numpy as jnp
from jax import lax
from jax.experimental import pallas as pl
from jax.experimental.pallas import tpu as pltpu

F32 = jnp.float32
BF16 = jnp.bfloat16

D_MODEL = 1024
M_HEADS = 4
M_DQK = 128
M_DV = D_MODEL // M_HEADS
MQ_W = M_HEADS * M_DQK
A_HEADS = 16
A_KV = 4
A_GROUP = A_HEADS // A_KV
A_HD = D_MODEL // A_HEADS
KV_W = A_KV * A_HD
WINDOW = 128
D_FF = 4 * D_MODEL
EPS = 1e-6
LOG2E = 1.4426950408889634

PROJ_BLOCK = 1024
D_PROJ = 3 * PROJ_BLOCK
GATE_ROWS = 16

MLSTM_CHUNK = 256
SAMPLE_TILE = 128
SWA_STEP = 512
VMEM_LIMIT = 56 * 1024 * 1024


def _pick(n, candidates):
    for c in candidates:
        if n % c == 0:
            return c
    raise ValueError(f"no block size for {n}")


def _sigmoid(x):
    return 1.0 / (1.0 + jnp.exp(-x))


def _log_sigmoid(x):
    return jnp.minimum(x, 0.0) - jnp.log1p(jnp.exp(-jnp.abs(x)))


def _rms(x, g):
    y = x * lax.rsqrt(jnp.mean(x * x, axis=-1, keepdims=True) + EPS)
    return y * g


def _dot(a, b):
    return jnp.dot(a, b, preferred_element_type=F32)


def _dot_nt(a, b):
    return lax.dot_general(a, b, (((1,), (1,)), ((), ())), preferred_element_type=F32)


def _params(*sem):
    return pltpu.CompilerParams(dimension_semantics=sem, vmem_limit_bytes=VMEM_LIMIT)


def _in_proj_kernel(x_ref, g_ref, w_ref, wk_ref, wg_ref, bg_ref, scale_ref, scale_t_ref,
                    xn_out, mv_ref, mq_ref, pf_ref, kt_ref, gt_ref, aq_ref, *, q_transposed):
    xn = _rms(x_ref[...], g_ref[...]).astype(BF16)
    xn_out[...] = xn
    kt_ref[...] = _dot_nt(wk_ref[...], xn).astype(BF16)
    gt_ref[...] = _dot_nt(wg_ref[...], xn) + bg_ref[...]
    blk = lambda j: slice(j * PROJ_BLOCK, (j + 1) * PROJ_BLOCK)
    mv_ref[...] = (_dot_nt(xn, w_ref[blk(0), :]) * scale_ref[:, blk(0)]).astype(BF16)
    if q_transposed:
        aq_ref[...] = (_dot_nt(w_ref[blk(1), :], xn) * scale_t_ref[blk(1), :]).astype(BF16)
    else:
        aq_ref[...] = (_dot_nt(xn, w_ref[blk(1), :]) * scale_ref[:, blk(1)]).astype(BF16)
    acc = _dot_nt(xn, w_ref[blk(2), :]) * scale_ref[:, blk(2)]
    mq_ref[...] = acc[:, :MQ_W].astype(BF16)
    pf_ref[...] = acc[:, MQ_W:]


def _in_proj(x, g, w, wk, wg, bg, scale, scale_t, q_transposed):
    n = x.shape[0]
    tm = _pick(n, (512, 256, 128))
    const = lambda i: (0, 0)
    row = lambda i: (i, 0)
    col = lambda i: (0, i)
    if q_transposed:
        aq_spec, aq_shape = pl.BlockSpec((D_MODEL, tm), col), (D_MODEL, n)
    else:
        aq_spec, aq_shape = pl.BlockSpec((tm, D_MODEL), row), (n, D_MODEL)
    return pl.pallas_call(
        functools.partial(_in_proj_kernel, q_transposed=q_transposed),
        grid=(n // tm,),
        in_specs=[
            pl.BlockSpec((tm, D_MODEL), row),
            pl.BlockSpec((1, D_MODEL), const),
            pl.BlockSpec((D_PROJ, D_MODEL), const),
            pl.BlockSpec((MQ_W, D_MODEL), const),
            pl.BlockSpec((GATE_ROWS, D_MODEL), const),
            pl.BlockSpec((GATE_ROWS, 1), const),
            pl.BlockSpec((1, D_PROJ), const),
            pl.BlockSpec((D_PROJ, 1), const),
        ],
        out_specs=[
            pl.BlockSpec((tm, D_MODEL), row),
            pl.BlockSpec((tm, D_MODEL), row),
            pl.BlockSpec((tm, MQ_W), row),
            pl.BlockSpec((tm, 2 * KV_W), row),
            pl.BlockSpec((MQ_W, tm), col),
            pl.BlockSpec((GATE_ROWS, tm), col),
            aq_spec,
        ],
        out_shape=[
            jax.ShapeDtypeStruct((n, D_MODEL), BF16),
            jax.ShapeDtypeStruct((n, D_MODEL), BF16),
            jax.ShapeDtypeStruct((n, MQ_W), BF16),
            jax.ShapeDtypeStruct((n, 2 * KV_W), F32),
            jax.ShapeDtypeStruct((MQ_W, n), BF16),
            jax.ShapeDtypeStruct((GATE_ROWS, n), F32),
            jax.ShapeDtypeStruct(aq_shape, BF16),
        ],
        compiler_params=_params("parallel"),
        name="in_proj",
    )(x, g, w, wk, wg, bg, scale, scale_t)


def _seg_cumsum(x, seg):
    pos = lax.broadcasted_iota(jnp.int32, x.shape, 1) & (seg - 1)
    sh = 1
    while sh < seg:
        x = x + jnp.where(pos >= sh, pltpu.roll(x, sh, axis=1), 0.0)
        sh *= 2
    return x


def _mlstm_finish(num, inv, o_pre, nw):
    msq = jnp.mean(num * num, axis=-1, keepdims=True)
    r = inv * lax.rsqrt(inv * inv * msq + EPS)
    return _sigmoid(o_pre) * (num * r) * nw


def _mlstm_prompt_kernel(*refs, nseq):
    xn_ref, wo_ref, q_ref, v_ref, nw_ref = refs[:5]
    kt_refs = refs[5:5 + nseq]
    gt_refs = refs[5 + nseq:5 + 2 * nseq]
    h_ref, c_out, n_out, m_out, c_scr, n_scr, m_scr = refs[5 + 2 * nseq:]
    c = pl.program_id(0)
    lt = q_ref.shape[1]
    units = [(b, h) for b in range(nseq) for h in range(M_HEADS)]
    idx = range(len(units))

    @pl.when(c == 0)
    def _():
        c_scr[...] = jnp.zeros_like(c_scr)
        n_scr[...] = jnp.zeros_like(n_scr)
        m_scr[...] = jnp.zeros_like(m_scr)

    i2, b2, b2_t = [], [], []
    for b in range(nseq):
        gt = gt_refs[b][...]
        i2.append(gt * LOG2E)
        b2.append(_seg_cumsum(_log_sigmoid(gt), lt) * LOG2E)
        b2_t.append(jnp.concatenate([b2[b], jnp.zeros((128 - GATE_ROWS, lt), F32)], axis=0).T)
    t_idx = lax.broadcasted_iota(jnp.int32, (lt, lt), 0)
    s_idx = lax.broadcasted_iota(jnp.int32, (lt, lt), 1)
    causal = s_idx <= t_idx

    q = [q_ref[b, :, h * M_DQK:(h + 1) * M_DQK] for b, h in units]
    kt = [kt_refs[b][h * M_DQK:(h + 1) * M_DQK, :] for b, h in units]
    v = [v_ref[b, :, h * M_DV:(h + 1) * M_DV] for b, h in units]
    m2_col = [jnp.broadcast_to(m_scr[8 * b + h:8 * b + h + 1, 0:1] * LOG2E, (lt, 1)) for b, h in units]
    n_old = [n_scr[8 * b + h:8 * b + h + 1, :] for b, h in units]
    c_old = [c_scr[M_HEADS * b + h] for b, h in units]

    a_mat = [jnp.where(causal, i2[b][h:h + 1] - b2[b][M_HEADS + h:M_HEADS + h + 1], -jnp.inf)
             for b, h in units]
    g_col = [jnp.maximum(jnp.max(a_mat[u], axis=1, keepdims=True), m2_col[u]) for u in idx]
    w = [jnp.exp2(a_mat[u] - g_col[u]) for u in idx]
    w_int = [jnp.exp2(m2_col[u] - g_col[u]) for u in idx]
    s = [_dot(q[u], kt[u]) * w[u] for u in idx]
    inter = [_dot(q[u], c_old[u].astype(BF16)) for u in idx]
    qn = [_dot_nt(q[u], jnp.broadcast_to(n_old[u], (8, M_DQK)).astype(BF16))[:, 0:1] for u in idx]
    num = [w_int[u] * inter[u] + _dot(s[u].astype(BF16), v[u]) for u in idx]
    den = [w_int[u] * qn[u] + jnp.sum(s[u], axis=1, keepdims=True) for u in idx]
    m2_row = [b2_t[b][:, M_HEADS + h:M_HEADS + h + 1] + g_col[u] for u, (b, h) in enumerate(units)]
    inv = [1.0 / jnp.maximum(jnp.abs(den[u]), jnp.exp2(-m2_row[u])) for u in idx]

    wk_row = [w[u][lt - 1:lt, :] for u in idx]
    c_upd = [_dot((kt[u].astype(F32) * wk_row[u]).astype(BF16), v[u]) for u in idx]
    n_upd = [_dot_nt(jnp.broadcast_to(wk_row[u], (8, lt)).astype(BF16), kt[u]) for u in idx]
    for u, (b, h) in enumerate(units):
        decay = w_int[u][lt - 1:lt]
        c_scr[M_HEADS * b + h] = decay * c_old[u] + c_upd[u]
        n_scr[8 * b + h:8 * b + h + 1, :] = decay * n_old[u] + n_upd[u][0:1]
        m_scr[8 * b + h:8 * b + h + 1, :] = jnp.broadcast_to(
            m2_row[u][lt - 1:lt] * (1.0 / LOG2E), (1, m_scr.shape[1]))

    o_pre = _dot_nt(xn_ref[...].reshape(nseq * lt, D_MODEL), wo_ref[...])
    for u, (b, h) in enumerate(units):
        sl = slice(h * M_DV, (h + 1) * M_DV)
        h_ref[b, :, sl] = _mlstm_finish(num[u], inv[u], o_pre[b * lt:(b + 1) * lt, sl], nw_ref[:, sl])

    @pl.when(c == pl.num_programs(0) - 1)
    def _():
        c_out[...] = c_scr[...].reshape(c_out.shape)
        for b in range(nseq):
            n_out[b] = n_scr[8 * b:8 * b + M_HEADS]
        m_out[...] = m_scr[...].reshape(m_out.shape)


def _mlstm_prompt(xn, w_mo, mv, mq, kt, gt, nw, batch, seq):
    lt = _pick(seq, (MLSTM_CHUNK, 128))
    nc = seq // lt
    tok = lambda width: pl.BlockSpec((batch, lt, width), lambda c: (0, c, 0))
    const = lambda shape: pl.BlockSpec(shape, lambda c: (0,) * len(shape))
    lanes = lambda rows: [pl.BlockSpec((rows, lt), lambda c, b=b: (0, b * nc + c)) for b in range(batch)]
    per_seq = lambda a: a.reshape(batch, seq, a.shape[-1])
    ba, pc, pn, pm = pl.pallas_call(
        functools.partial(_mlstm_prompt_kernel, nseq=batch),
        grid=(nc,),
        in_specs=[tok(D_MODEL), const((D_MODEL, D_MODEL)), tok(MQ_W), tok(D_MODEL), const((1, D_MODEL))]
        + lanes(MQ_W) + lanes(GATE_ROWS),
        out_specs=[
            tok(D_MODEL),
            const((batch, M_HEADS, M_DQK, M_DV)),
            const((batch, M_HEADS, M_DQK)),
            const((batch, 8, 128)),
        ],
        out_shape=[
            jax.ShapeDtypeStruct((batch, seq, D_MODEL), F32),
            jax.ShapeDtypeStruct((batch, M_HEADS, M_DQK, M_DV), F32),
            jax.ShapeDtypeStruct((batch, M_HEADS, M_DQK), F32),
            jax.ShapeDtypeStruct((batch, 8, 128), F32),
        ],
        scratch_shapes=[pltpu.VMEM((batch * M_HEADS, M_DQK, M_DV), F32), pltpu.VMEM((batch * 8, M_DQK), F32),
                        pltpu.VMEM((batch * 8, 128), F32)],
        compiler_params=_params("arbitrary"),
        name="mlstm_prompt",
    )(per_seq(xn), w_mo, per_seq(mq), per_seq(mv), nw, *([kt] * batch), *([gt] * batch))
    return ba.reshape(batch * seq, D_MODEL), pc, pn, pm


def _mlstm_sample_kernel(*refs, seg):
    xn_ref, wo_ref, q_ref, kt_ref, v_ref, gt_ref, nw_ref, m_ref, c_in, n_in = refs[:10]
    h_ref, c_out, n_out, m_out = refs[-4:]
    hd = pl.program_id(1)
    lt = q_ref.shape[0]
    nseg = lt // seg
    gt = gt_ref[...]
    lf = _log_sigmoid(gt)
    bcs = _seg_cumsum(lf, seg)
    row = lax.broadcasted_iota(jnp.int32, (GATE_ROWS, lt), 0)

    def pick(x, r):
        return jnp.sum(jnp.where(row == r, x, 0.0), axis=0, keepdims=True)

    i_row, lf_row, b_row = pick(gt, hd), pick(lf, M_HEADS + hd), pick(bcs, M_HEADS + hd)
    t_idx = lax.broadcasted_iota(jnp.int32, (lt, lt), 0)
    s_idx = lax.broadcasted_iota(jnp.int32, (lt, lt), 1)
    causal = (s_idx <= t_idx) & ((s_idx | (seg - 1)) == (t_idx | (seg - 1)))
    last = t_idx == (s_idx | (seg - 1))

    q = q_ref[...]
    kt = kt_ref[...]
    v = v_ref[...]
    m_col = m_ref[0, 0]
    n_old = [n_in[0, i, pl.ds(hd, 1), :] for i in range(nseg)]
    n_tok = jnp.concatenate([jnp.broadcast_to(n, (seg, M_DQK)) for n in n_old], axis=0)

    a_mat = jnp.where(causal, i_row - b_row, -jnp.inf)
    g_col = jnp.maximum(jnp.max(a_mat, axis=1, keepdims=True), m_col)
    b_col = jnp.sum(jnp.where(causal, lf_row, 0.0), axis=1, keepdims=True)
    w = jnp.exp(a_mat - g_col)
    w_int = jnp.exp(m_col - g_col)
    s = _dot(q, kt) * w
    inter = jnp.concatenate(
        [_dot(q[i * seg:(i + 1) * seg], c_in[0, i, 0].astype(BF16)) for i in range(nseg)], axis=0)
    qn = jnp.sum(q.astype(F32) * n_tok, axis=1, keepdims=True)
    num = w_int * inter + _dot(s.astype(BF16), v)
    den = w_int * qn + jnp.sum(s, axis=1, keepdims=True)
    m_row = b_col + g_col
    inv = 1.0 / jnp.maximum(jnp.abs(den), jnp.exp(-m_row))

    wk_row = jnp.sum(jnp.where(last, w, 0.0), axis=0, keepdims=True)
    ktw = kt.astype(F32) * wk_row
    lane_seg = lax.broadcasted_iota(jnp.int32, (1, lt), 1) | (seg - 1)
    lhs = jnp.concatenate(
        [jnp.where(lane_seg == i * seg + seg - 1, ktw, 0.0).astype(BF16) for i in range(nseg)], axis=0)
    c_upd = _dot(lhs, v)
    seg_row = lax.broadcasted_iota(jnp.int32, (lt, 1), 0) * seg + (seg - 1)
    n_upd = _dot_nt(jnp.where(lane_seg == seg_row, wk_row, 0.0).astype(BF16), kt)
    new_m = []
    for i in range(nseg):
        r = i * seg + seg - 1
        decay = w_int[r:r + 1]
        c_out[0, i, 0] = decay * c_in[0, i, 0] + c_upd[i * M_DQK:(i + 1) * M_DQK]
        n_out[0, i, pl.ds(hd, 1), :] = decay * n_old[i] + n_upd[i:i + 1]
        new_m.append(m_row[r:r + 1])
    m_out[0, 0] = jnp.concatenate(new_m, axis=0)
    h_ref[...] = _mlstm_finish(num, inv, _dot_nt(xn_ref[...], wo_ref[...]), nw_ref[...])


def _mlstm_sample(xn, w_mo, mv, mq, kt, gt, nw, m_tok, state_c, state_n, carried, layer, dec_batch, seg):
    lt = SAMPLE_TILE
    nseg = lt // seg
    depth = state_c.shape[0]
    any_spec = pl.BlockSpec(memory_space=pl.ANY)
    c_spec = pl.BlockSpec((1, nseg, 1, M_DQK, M_DV), lambda t, h: (layer, t, h, 0, 0))
    n_spec = pl.BlockSpec((1, nseg, M_HEADS, M_DQK), lambda t, h: (layer, t, 0, 0))
    operands = [xn, w_mo, mq, kt, mv, gt, nw, m_tok, state_c, state_n]
    in_specs = [
        pl.BlockSpec((lt, D_MODEL), lambda t, h: (t, 0)),
        pl.BlockSpec((M_DV, D_MODEL), lambda t, h: (h, 0)),
        pl.BlockSpec((lt, M_DQK), lambda t, h: (t, h)),
        pl.BlockSpec((M_DQK, lt), lambda t, h: (h, t)),
        pl.BlockSpec((lt, M_DV), lambda t, h: (t, h)),
        pl.BlockSpec((GATE_ROWS, lt), lambda t, h: (0, t)),
        pl.BlockSpec((1, M_DV), lambda t, h: (0, h)),
        pl.BlockSpec((1, 1, lt, 1), lambda t, h: (layer, h, t, 0)),
        c_spec, n_spec,
    ]
    aliases = {}
    if carried is not None:
        aliases = {len(operands) + k: 1 + k for k in range(3)}
        operands += list(carried)
        in_specs += [any_spec] * 3
    return pl.pallas_call(
        functools.partial(_mlstm_sample_kernel, seg=seg),
        grid=(dec_batch // nseg, M_HEADS),
        in_specs=in_specs,
        out_specs=[
            pl.BlockSpec((lt, M_DV), lambda t, h: (t, h)),
            c_spec, n_spec,
            pl.BlockSpec((1, 1, nseg, 1), lambda t, h: (layer, h, t, 0)),
        ],
        out_shape=[
            jax.ShapeDtypeStruct((dec_batch * seg, D_MODEL), F32),
            jax.ShapeDtypeStruct(state_c.shape, F32),
            jax.ShapeDtypeStruct(state_n.shape, F32),
            jax.ShapeDtypeStruct((depth, M_HEADS, dec_batch, 1), F32),
        ],
        input_output_aliases=aliases,
        compiler_params=_params("parallel", "arbitrary"),
        name="mlstm_sample",
    )(*operands)


def _swa_merge_prompt_kernel(sink_ref, bias_ref, qt_ref, kp_ref, kc_ref, vp_ref, vc_ref,
                             x_ref, xn_ref, ba_ref, wg_ref, w_ref,
                             o_ref, pk_ref, pv_ref, kb_scr, vt_scr, bb_scr):
    i = pl.program_id(1)
    nsub = qt_ref.shape[1] // WINDOW
    lanes = A_GROUP * WINDOW
    all_lanes = A_HEADS * WINDOW
    gates = _dot_nt(xn_ref[...], wg_ref[...])
    gated_a = _sigmoid(gates[:, :D_MODEL]) * ba_ref[...]
    gate_b = _sigmoid(gates[:, D_MODEL:])

    kb_scr[0:WINDOW] = kp_ref[...].astype(BF16)
    kb_scr[WINDOW:] = kc_ref[...].astype(BF16)
    vt_scr[:, 0:WINDOW] = vp_ref[...].T.astype(BF16)
    vt_scr[:, WINDOW:] = vc_ref[...].T.astype(BF16)

    lane_head = lax.broadcasted_iota(jnp.int32, (1, all_lanes), 1) >> (WINDOW.bit_length() - 1)
    sink = jnp.zeros((1, all_lanes), F32)
    for h in range(A_HEADS):
        sink = jnp.where(lane_head == h, sink_ref[h] * LOG2E, sink)
    has_prev = (lax.broadcasted_iota(jnp.int32, (2 * WINDOW, 1), 0) >= WINDOW) | (i > 0)
    zeros = jnp.zeros((A_HD, lanes), BF16)
    ones = jnp.ones((16, 2 * WINDOW), BF16)

    for j in range(nsub):
        qt = qt_ref[:, j * WINDOW:(j + 1) * WINDOW]
        qt4 = [jnp.concatenate([qt[(kv * A_GROUP + g) * A_HD:(kv * A_GROUP + g + 1) * A_HD]
                                for g in range(A_GROUP)], axis=1) for kv in range(A_KV)]
        st = []
        for pair in range(A_KV // 2):
            rhs = jnp.concatenate([jnp.concatenate([qt4[2 * pair], zeros], axis=1),
                                   jnp.concatenate([zeros, qt4[2 * pair + 1]], axis=1)], axis=0)
            kpair = kb_scr[j * WINDOW:(j + 2) * WINDOW, pair * 128:(pair + 1) * 128]
            st.append(_dot(kpair, rhs))
        st = jnp.concatenate(st, axis=1) + bias_ref[...]
        if j == 0:
            st = jnp.where(has_prev, st, -jnp.inf)
        mx = jnp.maximum(jnp.max(st, axis=0, keepdims=True), sink)
        p = jnp.exp2(st - mx).astype(BF16)
        sink_p = jnp.exp2(sink - mx)
        pieces = []
        for kv in range(A_KV):
            vt = jnp.concatenate([vt_scr[kv * A_HD:(kv + 1) * A_HD, j * WINDOW:(j + 2) * WINDOW], ones], axis=0)
            ot = _dot(vt, p[:, kv * lanes:(kv + 1) * lanes])
            den = ot[A_HD:A_HD + 1] + sink_p[:, kv * lanes:(kv + 1) * lanes]
            ot = ot[:A_HD] * (1.0 / den)
            pieces.extend(ot[:, g * WINDOW:(g + 1) * WINDOW] for g in range(A_GROUP))
        bb_scr[j * WINDOW:(j + 1) * WINDOW, :] = jnp.concatenate(pieces, axis=0).T

    merged = gated_a + gate_b * bb_scr[...]
    o_ref[...] = x_ref[...] + _dot(merged.astype(BF16), w_ref[...])

    @pl.when(i == pl.num_programs(1) - 1)
    def _():
        rows = kc_ref.shape[0]
        pk_ref[0] = kc_ref[rows - WINDOW:rows, :].T
        pv_ref[0] = vc_ref[rows - WINDOW:rows, :].T


def _band_bias():
    c = jnp.arange(2 * WINDOW, dtype=jnp.int32)[:, None]
    t = jnp.arange(A_HEADS * WINDOW, dtype=jnp.int32)[None, :] % WINDOW
    return jnp.where((c > t) & (c <= t + WINDOW), 0.0, -jnp.inf).astype(F32)


def _swa_merge_prompt(aq_t, pf, sinks, bias, x, xn, ba, w_gates, w_out, batch, seq):
    tq = _pick(seq, (SWA_STEP, 256, 128))
    ns = seq // tq
    per = tq // WINDOW
    nb = seq // WINDOW
    cur = lambda col: (lambda b, i: (b * ns + i, col))
    prev = lambda col: (lambda b, i: (b * nb + jnp.maximum(i * per - 1, 0), col))
    const = lambda b, i: (0, 0)
    row = pl.BlockSpec((tq, D_MODEL), cur(0))
    return pl.pallas_call(
        _swa_merge_prompt_kernel,
        grid=(batch, ns),
        in_specs=[
            pl.BlockSpec(memory_space=pltpu.SMEM),
            pl.BlockSpec(bias.shape, const),
            pl.BlockSpec((D_MODEL, tq), lambda b, i: (0, b * ns + i)),
            pl.BlockSpec((WINDOW, KV_W), prev(0)),
            pl.BlockSpec((tq, KV_W), cur(0)),
            pl.BlockSpec((WINDOW, KV_W), prev(1)),
            pl.BlockSpec((tq, KV_W), cur(1)),
            row, row, row,
            pl.BlockSpec((2 * D_MODEL, D_MODEL), const),
            pl.BlockSpec((D_MODEL, D_MODEL), const),
        ],
        out_specs=[
            pl.BlockSpec((tq, D_MODEL), lambda b, i: (b * ns + i, 0)),
            pl.BlockSpec((1, KV_W, WINDOW), lambda b, i: (b, 0, 0)),
            pl.BlockSpec((1, KV_W, WINDOW), lambda b, i: (b, 0, 0)),
        ],
        out_shape=[
            jax.ShapeDtypeStruct((batch * seq, D_MODEL), F32),
            jax.ShapeDtypeStruct((batch, KV_W, WINDOW), F32),
            jax.ShapeDtypeStruct((batch, KV_W, WINDOW), F32),
        ],
        scratch_shapes=[pltpu.VMEM((WINDOW + tq, KV_W), BF16), pltpu.VMEM((KV_W, WINDOW + tq), BF16),
                        pltpu.VMEM((tq, D_MODEL), F32)],
        compiler_params=_params("parallel", "arbitrary"),
        name="swa_merge_prompt",
    )(sinks, bias, aq_t, pf, pf, pf, pf, x, xn, ba, w_gates, w_out)


def _bdot(a, b, contract_b):
    return lax.dot_general(a, b, (((2,), (contract_b,)), ((0,), (0,))), preferred_element_type=F32)


def _swa_sample_kernel(*refs, tq):
    sink_ref, q_ref, kn_ref, vn_ref, ck_ref, cv_ref = refs[:6]
    o_ref, sk_ref, sv_ref = refs[-3:]
    nb = ck_ref.shape[1]
    wc = ck_ref.shape[3]
    keep = wc - tq
    q3 = q_ref[...].astype(F32).reshape(nb, tq, D_MODEL)
    ck = ck_ref[0]
    cv = cv_ref[0]

    lane = lax.broadcasted_iota(jnp.int32, (nb, KV_W, wc), 2)

    def appended(cache, new_rows):
        new_t = new_rows.T
        placed = jnp.stack([pltpu.roll(new_t, (keep - b * tq) % wc, axis=1) for b in range(nb)])
        return jnp.where(lane >= keep, placed, pltpu.roll(cache, keep, axis=2))

    sk = appended(ck, kn_ref[...])
    sv = appended(cv, vn_ref[...])
    sk_ref[0] = sk
    sv_ref[0] = sv

    rows = A_GROUP * tq
    t_idx = lax.broadcasted_iota(jnp.int32, (1, rows, 1), 1) & (tq - 1)
    c_idx = lax.broadcasted_iota(jnp.int32, (1, 1, wc), 2)
    mask_old = c_idx > t_idx
    mask_new = (c_idx >= keep) & (c_idx - keep <= t_idx)
    g_idx = lax.broadcasted_iota(jnp.int32, (1, rows, 1), 1) >> (tq.bit_length() - 1)
    pieces = []
    for kv in range(A_KV):
        heads = [kv * A_GROUP + g for g in range(A_GROUP)]
        q4 = jnp.concatenate([q3[:, :, h * A_HD:(h + 1) * A_HD] for h in heads], axis=1).astype(BF16)
        sink = jnp.zeros((1, rows, 1), F32)
        for g in range(A_GROUP):
            sink = jnp.where(g_idx == g, sink_ref[heads[g]] * LOG2E, sink)
        sl = slice(kv * A_HD, (kv + 1) * A_HD)
        s1 = jnp.where(mask_old, _bdot(q4, ck[:, sl, :].astype(BF16), 1), -jnp.inf)
        s2 = jnp.where(mask_new, _bdot(q4, sk[:, sl, :].astype(BF16), 1), -jnp.inf)
        mx = jnp.maximum(jnp.maximum(jnp.max(s1, axis=2, keepdims=True),
                                     jnp.max(s2, axis=2, keepdims=True)), sink)
        p1 = jnp.exp2(s1 - mx)
        p2 = jnp.exp2(s2 - mx)
        den = (jnp.sum(p1, axis=2, keepdims=True) + jnp.sum(p2, axis=2, keepdims=True)
               + jnp.exp2(sink - mx))
        r = 1.0 / den
        o = (_bdot((p1 * r).astype(BF16), cv[:, sl, :].astype(BF16), 2)
             + _bdot((p2 * r).astype(BF16), sv[:, sl, :].astype(BF16), 2))
        pieces.extend(o[:, g * tq:(g + 1) * tq, :] for g in range(A_GROUP))
    o_ref[...] = jnp.concatenate(pieces, axis=2).reshape(nb * tq, D_MODEL)


def _swa_sample(aq, pf, sinks, cache_kt, cache_vt, carried, layer, dec_batch, tq):
    lt = SAMPLE_TILE
    nb = lt // tq
    wc = cache_kt.shape[3]
    assert wc == lt, "the appended keys are placed with lane rolls over one cache row"
    any_spec = pl.BlockSpec(memory_space=pl.ANY)
    cache_spec = pl.BlockSpec((1, nb, KV_W, wc), lambda i: (layer, i, 0, 0))
    operands = [sinks, aq, pf, pf, cache_kt, cache_vt]
    in_specs = [
        pl.BlockSpec(memory_space=pltpu.SMEM),
        pl.BlockSpec((lt, D_MODEL), lambda i: (i, 0)),
        pl.BlockSpec((lt, KV_W), lambda i: (i, 0)),
        pl.BlockSpec((lt, KV_W), lambda i: (i, 1)),
        cache_spec, cache_spec,
    ]
    aliases = {}
    if carried is not None:
        aliases = {len(operands) + k: 1 + k for k in range(2)}
        operands += list(carried)
        in_specs += [any_spec] * 2
    return pl.pallas_call(
        functools.partial(_swa_sample_kernel, tq=tq),
        grid=(dec_batch // nb,),
        in_specs=in_specs,
        out_specs=[
            pl.BlockSpec((lt, D_MODEL), lambda i: (i, 0)),
            cache_spec, cache_spec,
        ],
        out_shape=[
            jax.ShapeDtypeStruct((dec_batch * tq, D_MODEL), F32),
            jax.ShapeDtypeStruct(cache_kt.shape, F32),
            jax.ShapeDtypeStruct(cache_vt.shape, F32),
        ],
        input_output_aliases=aliases,
        compiler_params=_params("parallel"),
        name="swa_sample",
    )(*operands)


def _merge_kernel(x_ref, xn_ref, wg_ref, ba_ref, bb_ref, w_ref, o_ref):
    gates = _dot_nt(xn_ref[...], wg_ref[...])
    merged = _sigmoid(gates[:, :D_MODEL]) * ba_ref[...] + _sigmoid(gates[:, D_MODEL:]) * bb_ref[...]
    o_ref[...] = x_ref[...] + _dot(merged.astype(BF16), w_ref[...])


def _merge(x, xn, w_gates, ba, bb, w_out):
    n = x.shape[0]
    tm = _pick(n, (512, 256, 128))
    row = pl.BlockSpec((tm, D_MODEL), lambda i: (i, 0))
    const = lambda shape: pl.BlockSpec(shape, lambda i: (0, 0))
    return pl.pallas_call(
        _merge_kernel,
        grid=(n // tm,),
        in_specs=[row, row, const((2 * D_MODEL, D_MODEL)), row, row, const((D_MODEL, D_MODEL))],
        out_specs=row,
        out_shape=jax.ShapeDtypeStruct((n, D_MODEL), F32),
        compiler_params=_params("parallel"),
        name="merge_out_proj",
    )(x, xn, w_gates, ba, bb, w_out)


def _mlp_kernel(x_ref, g_ref, wu_ref, wd_ref, gf_ref, o_ref, xn_ref, *, final_norm):
    j = pl.program_id(1)

    @pl.when(j == 0)
    def _():
        x = x_ref[...]
        xn_ref[...] = _rms(x, g_ref[...]).astype(BF16)
        o_ref[...] = x

    h = jnp.square(jnp.maximum(_dot(xn_ref[...], wu_ref[...]), 0.0))
    o_ref[...] += _dot(h.astype(BF16), wd_ref[...])

    if final_norm:
        @pl.when(j == pl.num_programs(1) - 1)
        def _():
            o_ref[...] = _rms(o_ref[...], gf_ref[...])


def _mlp(x, g, w_up, w_down, g_final, final_norm):
    n = x.shape[0]
    tm = _pick(n, (1024, 512, 256, 128))
    tf = 2048
    return pl.pallas_call(
        functools.partial(_mlp_kernel, final_norm=final_norm),
        grid=(n // tm, D_FF // tf),
        in_specs=[
            pl.BlockSpec((tm, D_MODEL), lambda i, j: (i, 0)),
            pl.BlockSpec((1, D_MODEL), lambda i, j: (0, 0)),
            pl.BlockSpec((D_MODEL, tf), lambda i, j: (0, j)),
            pl.BlockSpec((tf, D_MODEL), lambda i, j: (j, 0)),
            pl.BlockSpec((1, D_MODEL), lambda i, j: (0, 0)),
        ],
        out_specs=pl.BlockSpec((tm, D_MODEL), lambda i, j: (i, 0)),
        out_shape=jax.ShapeDtypeStruct((n, D_MODEL), F32),
        scratch_shapes=[pltpu.VMEM((tm, D_MODEL), BF16)],
        compiler_params=_params("parallel", "arbitrary"),
        name="mlp",
    )(x, g, w_up, w_down, g_final)


def _prep_in_proj_weights(w_in, b_gate):
    sizes = (MQ_W, MQ_W, D_MODEL, D_MODEL, M_HEADS, M_HEADS, D_MODEL, KV_W, KV_W, D_MODEL, D_MODEL)
    offs = [0]
    for s in sizes:
        offs.append(offs[-1] + s)
    w_t = jnp.swapaxes(w_in, 1, 2)
    mq, mk, mv, mo, mi, mf, aq, ak, av, ga, gb = (w_t[:, offs[i]:offs[i + 1]] for i in range(11))
    depth = w_in.shape[0]
    w_proj = jnp.concatenate([mv, aq, mq, ak, av], axis=1).astype(BF16)
    pad = jnp.zeros((depth, GATE_ROWS - 2 * M_HEADS, D_MODEL), F32)
    w_gate = jnp.concatenate([mi, mf, pad], axis=1).astype(BF16)
    b_gate_col = jnp.zeros((depth, GATE_ROWS, 1), F32).at[:, :2 * M_HEADS, 0].set(b_gate.astype(F32))
    w_merge = jnp.concatenate([ga, gb], axis=1).astype(BF16)
    mk = mk.astype(BF16)
    mo = mo.astype(BF16)
    scale = jnp.ones((1, D_PROJ), F32)
    scale = scale.at[:, PROJ_BLOCK:2 * PROJ_BLOCK].set(A_HD ** -0.5 * LOG2E)
    scale = scale.at[:, 2 * PROJ_BLOCK:2 * PROJ_BLOCK + MQ_W].set(M_DQK ** -0.5)
    return w_proj, mk, w_gate, b_gate_col, scale, mo, w_merge


def kernel(x_prompt, x_sample, state_C, state_n, state_m, cache_k, cache_v, norm_attn, w_in, b_gate,
           mlstm_norm, sinks, w_out, norm_mlp, w_up, w_down, norm_final):
    batch, seq, _ = x_prompt.shape
    dec_batch, dec_seq, _ = x_sample.shape
    depth = w_in.shape[0]
    wc = cache_k.shape[2]

    w_proj, w_mk, w_gate, b_gate_col, scale, w_mo, w_merge = _prep_in_proj_weights(w_in, b_gate)
    scale_t = scale.reshape(D_PROJ, 1)
    w_out_b = w_out.astype(BF16)
    w_up_b = w_up.astype(BF16)
    w_down_b = w_down.astype(BF16)
    bias = _band_bias()

    xp = x_prompt.reshape(batch * seq, D_MODEL)
    xs = x_sample.reshape(dec_batch * dec_seq, D_MODEL)
    m_tok = jnp.repeat(jnp.swapaxes(state_m, 1, 2), dec_seq, axis=2)[..., None]
    cache_kt = jnp.transpose(cache_k, (0, 1, 3, 4, 2)).reshape(depth, dec_batch, KV_W, wc)
    cache_vt = jnp.transpose(cache_v, (0, 1, 3, 4, 2)).reshape(depth, dec_batch, KV_W, wc)

    states = None
    caches = None
    p_c, p_n, p_m, p_k, p_v = [], [], [], [], []
    g_final = norm_final[None]

    for l in range(depth):
        g_attn = norm_attn[l][None]
        nw = mlstm_norm[l][None]
        g_mlp = norm_mlp[l][None]
        last = l == depth - 1
        proj_w = (w_proj[l], w_mk[l], w_gate[l], b_gate_col[l], scale, scale_t)

        xn, mv, mq, pf, kt, gt, aq_t = _in_proj(xp, g_attn, *proj_w, q_transposed=True)
        ba, pc, pn, pm = _mlstm_prompt(xn, w_mo[l], mv, mq, kt, gt, nw, batch, seq)
        xp, pk, pv = _swa_merge_prompt(aq_t, pf, sinks[l], bias, xp, xn, ba, w_merge[l], w_out_b[l], batch, seq)
        xp = _mlp(xp, g_mlp, w_up_b[l], w_down_b[l], g_final, last)
        p_c.append(pc)
        p_n.append(pn)
        p_m.append(pm[:, :M_HEADS, 0])
        p_k.append(pk)
        p_v.append(pv)

        xn, mv, mq, pf, kt, gt, aq = _in_proj(xs, g_attn, *proj_w, q_transposed=False)
        ba, *states = _mlstm_sample(xn, w_mo[l], mv, mq, kt, gt, nw, m_tok, state_C, state_n, states,
                                    l, dec_batch, dec_seq)
        bb, *caches = _swa_sample(aq, pf, sinks[l], cache_kt, cache_vt, caches, l, dec_batch, dec_seq)
        xs = _merge(xs, xn, w_merge[l], ba, bb, w_out_b[l])
        xs = _mlp(xs, g_mlp, w_up_b[l], w_down_b[l], g_final, last)

    s_c, s_n, s_m = states
    s_k, s_v = caches

    def positions_major(t, lead):
        t = t.reshape(*lead, A_KV, A_HD, wc)
        return jnp.moveaxis(t, -1, -3)

    return (xp.reshape(batch, seq, D_MODEL), xs.reshape(dec_batch, dec_seq, D_MODEL),
            jnp.stack(p_c), jnp.stack(p_n), jnp.stack(p_m),
            positions_major(jnp.stack(p_k), (depth, batch)), positions_major(jnp.stack(p_v), (depth, batch)),
            s_c, s_n, jnp.swapaxes(s_m[..., 0], 1, 2),
            positions_major(s_k, (depth, dec_batch)), positions_major(s_v, (depth, dec_batch)))
```

```python
import functools

import jax
import jax.numpy as jnp
from jax import lax
from jax.experimental import pallas as pl
from jax.experimental.pallas import tpu as pltpu

F32 = jnp.float32
BF16 = jnp.bfloat16

D_MODEL = 1024
M_HEADS = 4
M_DQK = 128
M_DV = D_MODEL // M_HEADS
MQ_W = M_HEADS * M_DQK
A_HEADS = 16
A_KV = 4
A_GROUP = A_HEADS // A_KV
A_HD = D_MODEL // A_HEADS
KV_W = A_KV * A_HD
WINDOW = 128
D_FF = 4 * D_MODEL
EPS = 1e-6
LOG2E = 1.4426950408889634

ROW_MQ = 0
ROW_MK = ROW_MQ + MQ_W
ROW_MV = ROW_MK + MQ_W
ROW_MO = ROW_MV + D_MODEL
ROW_GATES = ROW_MO + D_MODEL
ROW_AQ = ROW_GATES + 2 * M_HEADS
ROW_MERGE = ROW_AQ + D_MODEL + 2 * KV_W
D_IN = ROW_MERGE + 2 * D_MODEL
GATE_ROWS = 16

MLSTM_CHUNK = 256
SAMPLE_TILE = 128
SWA_STEP = 512
VMEM_LIMIT = 56 * 1024 * 1024


def _pick(n, candidates):
    for c in candidates:
        if n % c == 0:
            return c
    raise ValueError(f"no block size for {n}")


def _sigmoid(x):
    return 1.0 / (1.0 + jnp.exp(-x))


def _log_sigmoid(x):
    return jnp.minimum(x, 0.0) - jnp.log1p(jnp.exp(-jnp.abs(x)))


def _rms(x, g):
    y = x * lax.rsqrt(jnp.mean(x * x, axis=-1, keepdims=True) + EPS)
    return y * g


def _dot(a, b):
    return jnp.dot(a, b, preferred_element_type=F32)


def _dot_nt(a, b):
    return lax.dot_general(a, b, (((1,), (1,)), ((), ())), preferred_element_type=F32)


def _params(*sem):
    return pltpu.CompilerParams(dimension_semantics=sem, vmem_limit_bytes=VMEM_LIMIT)


def _in_proj_kernel(x_ref, g_ref, wmq_ref, wmk_ref, wmv_ref, wg_ref, waq_ref, wkv_ref, bg_ref,
                    xn_out, mv_ref, mq_ref, pf_ref, kt_ref, gt_ref, aq_ref, *, q_transposed):
    xn = _rms(x_ref[...], g_ref[...]).astype(BF16)
    xn_out[...] = xn
    kt_ref[...] = _dot_nt(wmk_ref[0], xn).astype(BF16)
    gt_ref[...] = _dot_nt(wg_ref[0], xn) + bg_ref[...]
    mv_ref[...] = _dot_nt(xn, wmv_ref[0]).astype(BF16)
    if q_transposed:
        aq_ref[...] = (_dot_nt(waq_ref[0], xn) * (A_HD ** -0.5 * LOG2E)).astype(BF16)
    else:
        aq_ref[...] = (_dot_nt(xn, waq_ref[0]) * (A_HD ** -0.5 * LOG2E)).astype(BF16)
    mq_ref[...] = (_dot_nt(xn, wmq_ref[0]) * (M_DQK ** -0.5)).astype(BF16)
    pf_ref[...] = _dot_nt(xn, wkv_ref[0])


def _in_proj(x, g, w_all, w_att, bg, layer, q_transposed):
    n = x.shape[0]
    tm = _pick(n, (512, 256, 128))
    const = lambda i: (0, 0)
    row = lambda i: (i, 0)
    col = lambda i: (0, i)
    wblk = lambda rows, off: pl.BlockSpec((1, rows, D_MODEL), lambda i: (layer, off // rows, 0))
    if q_transposed:
        aq_spec, aq_shape = pl.BlockSpec((D_MODEL, tm), col), (D_MODEL, n)
    else:
        aq_spec, aq_shape = pl.BlockSpec((tm, D_MODEL), row), (n, D_MODEL)
    return pl.pallas_call(
        functools.partial(_in_proj_kernel, q_transposed=q_transposed),
        grid=(n // tm,),
        in_specs=[
            pl.BlockSpec((tm, D_MODEL), row),
            pl.BlockSpec((1, D_MODEL), const),
            wblk(MQ_W, ROW_MQ), wblk(MQ_W, ROW_MK), wblk(D_MODEL, ROW_MV), wblk(GATE_ROWS, ROW_GATES),
            wblk(D_MODEL, 0), wblk(2 * KV_W, D_MODEL),
            pl.BlockSpec((GATE_ROWS, 1), const),
        ],
        out_specs=[
            pl.BlockSpec((tm, D_MODEL), row),
            pl.BlockSpec((tm, D_MODEL), row),
            pl.BlockSpec((tm, MQ_W), row),
            pl.BlockSpec((tm, 2 * KV_W), row),
            pl.BlockSpec((MQ_W, tm), col),
            pl.BlockSpec((GATE_ROWS, tm), col),
            aq_spec,
        ],
        out_shape=[
            jax.ShapeDtypeStruct((n, D_MODEL), BF16),
            jax.ShapeDtypeStruct((n, D_MODEL), BF16),
            jax.ShapeDtypeStruct((n, MQ_W), BF16),
            jax.ShapeDtypeStruct((n, 2 * KV_W), F32),
            jax.ShapeDtypeStruct((MQ_W, n), BF16),
            jax.ShapeDtypeStruct((GATE_ROWS, n), F32),
            jax.ShapeDtypeStruct(aq_shape, BF16),
        ],
        compiler_params=_params("parallel"),
        name="in_proj",
    )(x, g, w_all, w_all, w_all, w_all, w_att, w_att, bg)


def _seg_cumsum(x, seg):
    pos = lax.broadcasted_iota(jnp.int32, x.shape, 1) & (seg - 1)
    sh = 1
    while sh < seg:
        x = x + jnp.where(pos >= sh, pltpu.roll(x, sh, axis=1), 0.0)
        sh *= 2
    return x


def _mlstm_finish(num, inv, o_pre, nw):
    msq = jnp.mean(num * num, axis=-1, keepdims=True)
    r = inv * lax.rsqrt(inv * inv * msq + EPS)
    return _sigmoid(o_pre) * (num * r) * nw


def _mlstm_prompt_kernel(*refs, nseq):
    xn_ref, wo_ref, q_ref, v_ref, nw_ref = refs[:5]
    kt_refs = refs[5:5 + nseq]
    gt_refs = refs[5 + nseq:5 + 2 * nseq]
    h_ref, c_out, n_out, m_out, c_scr, n_scr, m_scr = refs[5 + 2 * nseq:]
    c = pl.program_id(0)
    lt = q_ref.shape[1]
    units = [(b, h) for b in range(nseq) for h in range(M_HEADS)]
    idx = range(len(units))

    @pl.when(c == 0)
    def _():
        c_scr[...] = jnp.zeros_like(c_scr)
        n_scr[...] = jnp.zeros_like(n_scr)
        m_scr[...] = jnp.zeros_like(m_scr)

    i2, b2, b2_t = [], [], []
    for b in range(nseq):
        gt = gt_refs[b][...]
        i2.append(gt * LOG2E)
        b2.append(_seg_cumsum(_log_sigmoid(gt), lt) * LOG2E)
        b2_t.append(jnp.concatenate([b2[b], jnp.zeros((128 - GATE_ROWS, lt), F32)], axis=0).T)
    t_idx = lax.broadcasted_iota(jnp.int32, (lt, lt), 0)
    s_idx = lax.broadcasted_iota(jnp.int32, (lt, lt), 1)
    causal = s_idx <= t_idx

    q = [q_ref[b, :, h * M_DQK:(h + 1) * M_DQK] for b, h in units]
    kt = [kt_refs[b][h * M_DQK:(h + 1) * M_DQK, :] for b, h in units]
    v = [v_ref[b, :, h * M_DV:(h + 1) * M_DV] for b, h in units]
    m2_col = [jnp.broadcast_to(m_scr[8 * b + h:8 * b + h + 1, 0:1] * LOG2E, (lt, 1)) for b, h in units]
    n_old = [n_scr[8 * b + h:8 * b + h + 1, :] for b, h in units]
    c_old = [c_scr[M_HEADS * b + h] for b, h in units]

    a_mat = [jnp.where(causal, i2[b][h:h + 1] - b2[b][M_HEADS + h:M_HEADS + h + 1], -jnp.inf)
             for b, h in units]
    g_col = [jnp.maximum(jnp.max(a_mat[u], axis=1, keepdims=True), m2_col[u]) for u in idx]
    w = [jnp.exp2(a_mat[u] - g_col[u]) for u in idx]
    w_int = [jnp.exp2(m2_col[u] - g_col[u]) for u in idx]
    s = [_dot(q[u], kt[u]) * w[u] for u in idx]
    inter = [_dot(q[u], c_old[u].astype(BF16)) for u in idx]
    qn = [_dot_nt(q[u], jnp.broadcast_to(n_old[u], (8, M_DQK)).astype(BF16))[:, 0:1] for u in idx]
    num = [w_int[u] * inter[u] + _dot(s[u].astype(BF16), v[u]) for u in idx]
    den = [w_int[u] * qn[u] + jnp.sum(s[u], axis=1, keepdims=True) for u in idx]
    m2_row = [b2_t[b][:, M_HEADS + h:M_HEADS + h + 1] + g_col[u] for u, (b, h) in enumerate(units)]
    inv = [1.0 / jnp.maximum(jnp.abs(den[u]), jnp.exp2(-m2_row[u])) for u in idx]

    wk_row = [w[u][lt - 1:lt, :] for u in idx]
    c_upd = [_dot((kt[u].astype(F32) * wk_row[u]).astype(BF16), v[u]) for u in idx]
    n_upd = [_dot_nt(jnp.broadcast_to(wk_row[u], (8, lt)).astype(BF16), kt[u]) for u in idx]
    for u, (b, h) in enumerate(units):
        decay = w_int[u][lt - 1:lt]
        c_scr[M_HEADS * b + h] = decay * c_old[u] + c_upd[u]
        n_scr[8 * b + h:8 * b + h + 1, :] = decay * n_old[u] + n_upd[u][0:1]
        m_scr[8 * b + h:8 * b + h + 1, :] = jnp.broadcast_to(
            m2_row[u][lt - 1:lt] * (1.0 / LOG2E), (1, m_scr.shape[1]))

    o_pre = _dot_nt(xn_ref[...].reshape(nseq * lt, D_MODEL), wo_ref[0])
    for u, (b, h) in enumerate(units):
        sl = slice(h * M_DV, (h + 1) * M_DV)
        h_ref[b, :, sl] = _mlstm_finish(num[u], inv[u], o_pre[b * lt:(b + 1) * lt, sl], nw_ref[:, sl])

    @pl.when(c == pl.num_programs(0) - 1)
    def _():
        c_out[...] = c_scr[...].reshape(c_out.shape)
        for b in range(nseq):
            n_out[b] = n_scr[8 * b:8 * b + M_HEADS]
        m_out[...] = m_scr[...].reshape(m_out.shape)


def _mlstm_prompt(xn, w_all, mv, mq, kt, gt, nw, layer, batch, seq):
    lt = _pick(seq, (MLSTM_CHUNK, 128))
    nc = seq // lt
    tok = lambda width: pl.BlockSpec((batch, lt, width), lambda c: (0, c, 0))
    const = lambda shape: pl.BlockSpec(shape, lambda c: (0,) * len(shape))
    lanes = lambda rows: [pl.BlockSpec((rows, lt), lambda c, b=b: (0, b * nc + c)) for b in range(batch)]
    per_seq = lambda a: a.reshape(batch, seq, a.shape[-1])
    w_mo = pl.BlockSpec((1, D_MODEL, D_MODEL), lambda c: (layer, ROW_MO // D_MODEL, 0))
    ba, pc, pn, pm = pl.pallas_call(
        functools.partial(_mlstm_prompt_kernel, nseq=batch),
        grid=(nc,),
        in_specs=[tok(D_MODEL), w_mo, tok(MQ_W), tok(D_MODEL), const((1, D_MODEL))]
        + lanes(MQ_W) + lanes(GATE_ROWS),
        out_specs=[
            tok(D_MODEL),
            const((batch, M_HEADS, M_DQK, M_DV)),
            const((batch, M_HEADS, M_DQK)),
            const((batch, 8, 128)),
        ],
        out_shape=[
            jax.ShapeDtypeStruct((batch, seq, D_MODEL), F32),
            jax.ShapeDtypeStruct((batch, M_HEADS, M_DQK, M_DV), F32),
            jax.ShapeDtypeStruct((batch, M_HEADS, M_DQK), F32),
            jax.ShapeDtypeStruct((batch, 8, 128), F32),
        ],
        scratch_shapes=[pltpu.VMEM((batch * M_HEADS, M_DQK, M_DV), F32), pltpu.VMEM((batch * 8, M_DQK), F32),
                        pltpu.VMEM((batch * 8, 128), F32)],
        compiler_params=_params("arbitrary"),
        name="mlstm_prompt",
    )(per_seq(xn), w_all, per_seq(mq), per_seq(mv), nw, *([kt] * batch), *([gt] * batch))
    return ba.reshape(batch * seq, D_MODEL), pc, pn, pm


def _mlstm_sample_kernel(*refs, seg):
    xn_ref, wo_ref, q_ref, kt_ref, v_ref, gt_ref, nw_ref, m_ref, c_in, n_in = refs[:10]
    h_ref, c_out, n_out, m_out = refs[-4:]
    hd = pl.program_id(1)
    lt = q_ref.shape[0]
    nseg = lt // seg
    gt = gt_ref[...]
    lf = _log_sigmoid(gt)
    bcs = _seg_cumsum(lf, seg)
    row = lax.broadcasted_iota(jnp.int32, (GATE_ROWS, lt), 0)

    def pick(x, r):
        return jnp.sum(jnp.where(row == r, x, 0.0), axis=0, keepdims=True)

    i_row, lf_row, b_row = pick(gt, hd), pick(lf, M_HEADS + hd), pick(bcs, M_HEADS + hd)
    t_idx = lax.broadcasted_iota(jnp.int32, (lt, lt), 0)
    s_idx = lax.broadcasted_iota(jnp.int32, (lt, lt), 1)
    causal = (s_idx <= t_idx) & ((s_idx | (seg - 1)) == (t_idx | (seg - 1)))
    last = t_idx == (s_idx | (seg - 1))

    q = q_ref[...]
    kt = kt_ref[...]
    v = v_ref[...]
    m_col = m_ref[0, 0]
    n_old = [n_in[0, i, pl.ds(hd, 1), :] for i in range(nseg)]
    n_tok = jnp.concatenate([jnp.broadcast_to(n, (seg, M_DQK)) for n in n_old], axis=0)

    a_mat = jnp.where(causal, i_row - b_row, -jnp.inf)
    g_col = jnp.maximum(jnp.max(a_mat, axis=1, keepdims=True), m_col)
    b_col = jnp.sum(jnp.where(causal, lf_row, 0.0), axis=1, keepdims=True)
    w = jnp.exp(a_mat - g_col)
    w_int = jnp.exp(m_col - g_col)
    s = _dot(q, kt) * w
    inter = jnp.concatenate(
        [_dot(q[i * seg:(i + 1) * seg], c_in[0, i, 0].astype(BF16)) for i in range(nseg)], axis=0)
    qn = jnp.sum(q.astype(F32) * n_tok, axis=1, keepdims=True)
    num = w_int * inter + _dot(s.astype(BF16), v)
    den = w_int * qn + jnp.sum(s, axis=1, keepdims=True)
    m_row = b_col + g_col
    inv = 1.0 / jnp.maximum(jnp.abs(den), jnp.exp(-m_row))

    wk_row = jnp.sum(jnp.where(last, w, 0.0), axis=0, keepdims=True)
    ktw = kt.astype(F32) * wk_row
    lane_seg = lax.broadcasted_iota(jnp.int32, (1, lt), 1) | (seg - 1)
    lhs = jnp.concatenate(
        [jnp.where(lane_seg == i * seg + seg - 1, ktw, 0.0).astype(BF16) for i in range(nseg)], axis=0)
    c_upd = _dot(lhs, v)
    seg_row = lax.broadcasted_iota(jnp.int32, (lt, 1), 0) * seg + (seg - 1)
    n_upd = _dot_nt(jnp.where(lane_seg == seg_row, wk_row, 0.0).astype(BF16), kt)
    new_m = []
    for i in range(nseg):
        r = i * seg + seg - 1
        decay = w_int[r:r + 1]
        c_out[0, i, 0] = decay * c_in[0, i, 0] + c_upd[i * M_DQK:(i + 1) * M_DQK]
        n_out[0, i, pl.ds(hd, 1), :] = decay * n_old[i] + n_upd[i:i + 1]
        new_m.append(m_row[r:r + 1])
    m_out[0, 0] = jnp.concatenate(new_m, axis=0)
    h_ref[...] = _mlstm_finish(num, inv, _dot_nt(xn_ref[...], wo_ref[0]), nw_ref[...])


def _mlstm_sample(xn, w_all, mv, mq, kt, gt, nw, m_tok, state_c, state_n, carried, layer, dec_batch, seg):
    lt = SAMPLE_TILE
    nseg = lt // seg
    depth = state_c.shape[0]
    any_spec = pl.BlockSpec(memory_space=pl.ANY)
    c_spec = pl.BlockSpec((1, nseg, 1, M_DQK, M_DV), lambda t, h: (layer, t, h, 0, 0))
    n_spec = pl.BlockSpec((1, nseg, M_HEADS, M_DQK), lambda t, h: (layer, t, 0, 0))
    operands = [xn, w_all, mq, kt, mv, gt, nw, m_tok, state_c, state_n]
    in_specs = [
        pl.BlockSpec((lt, D_MODEL), lambda t, h: (t, 0)),
        pl.BlockSpec((1, M_DV, D_MODEL), lambda t, h: (layer, ROW_MO // M_DV + h, 0)),
        pl.BlockSpec((lt, M_DQK), lambda t, h: (t, h)),
        pl.BlockSpec((M_DQK, lt), lambda t, h: (h, t)),
        pl.BlockSpec((lt, M_DV), lambda t, h: (t, h)),
        pl.BlockSpec((GATE_ROWS, lt), lambda t, h: (0, t)),
        pl.BlockSpec((1, M_DV), lambda t, h: (0, h)),
        pl.BlockSpec((1, 1, lt, 1), lambda t, h: (layer, h, t, 0)),
        c_spec, n_spec,
    ]
    aliases = {}
    if carried is not None:
        aliases = {len(operands) + k: 1 + k for k in range(3)}
        operands += list(carried)
        in_specs += [any_spec] * 3
    return pl.pallas_call(
        functools.partial(_mlstm_sample_kernel, seg=seg),
        grid=(dec_batch // nseg, M_HEADS),
        in_specs=in_specs,
        out_specs=[
            pl.BlockSpec((lt, M_DV), lambda t, h: (t, h)),
            c_spec, n_spec,
            pl.BlockSpec((1, 1, nseg, 1), lambda t, h: (layer, h, t, 0)),
        ],
        out_shape=[
            jax.ShapeDtypeStruct((dec_batch * seg, D_MODEL), F32),
            jax.ShapeDtypeStruct(state_c.shape, F32),
            jax.ShapeDtypeStruct(state_n.shape, F32),
            jax.ShapeDtypeStruct((depth, M_HEADS, dec_batch, 1), F32),
        ],
        input_output_aliases=aliases,
        compiler_params=_params("parallel", "arbitrary"),
        name="mlstm_sample",
    )(*operands)


def _swa_merge_prompt_kernel(sink_ref, bias_ref, qt_ref, kp_ref, kc_ref, vp_ref, vc_ref,
                             x_ref, xn_ref, ba_ref, wg_ref, w_ref,
                             o_ref, pk_ref, pv_ref, kb_scr, vt_scr, bb_scr):
    i = pl.program_id(1)
    nsub = qt_ref.shape[1] // WINDOW
    lanes = A_GROUP * WINDOW
    all_lanes = A_HEADS * WINDOW
    gates = _dot_nt(xn_ref[...], wg_ref[0])
    gated_a = _sigmoid(gates[:, :D_MODEL]) * ba_ref[...]
    gate_b = _sigmoid(gates[:, D_MODEL:])

    kb_scr[0:WINDOW] = kp_ref[...].astype(BF16)
    kb_scr[WINDOW:] = kc_ref[...].astype(BF16)
    vt_scr[:, 0:WINDOW] = vp_ref[...].T.astype(BF16)
    vt_scr[:, WINDOW:] = vc_ref[...].T.astype(BF16)

    lane_head = lax.broadcasted_iota(jnp.int32, (1, all_lanes), 1) >> (WINDOW.bit_length() - 1)
    sink = jnp.zeros((1, all_lanes), F32)
    for h in range(A_HEADS):
        sink = jnp.where(lane_head == h, sink_ref[h] * LOG2E, sink)
    has_prev = (lax.broadcasted_iota(jnp.int32, (2 * WINDOW, 1), 0) >= WINDOW) | (i > 0)
    zeros = jnp.zeros((A_HD, lanes), BF16)
    ones = jnp.ones((16, 2 * WINDOW), BF16)

    for j in range(nsub):
        qt = qt_ref[:, j * WINDOW:(j + 1) * WINDOW]
        qt4 = [jnp.concatenate([qt[(kv * A_GROUP + g) * A_HD:(kv * A_GROUP + g + 1) * A_HD]
                                for g in range(A_GROUP)], axis=1) for kv in range(A_KV)]
        st = []
        for pair in range(A_KV // 2):
            rhs = jnp.concatenate([jnp.concatenate([qt4[2 * pair], zeros], axis=1),
                                   jnp.concatenate([zeros, qt4[2 * pair + 1]], axis=1)], axis=0)
            kpair = kb_scr[j * WINDOW:(j + 2) * WINDOW, pair * 128:(pair + 1) * 128]
            st.append(_dot(kpair, rhs))
        st = jnp.concatenate(st, axis=1) + bias_ref[...]
        if j == 0:
            st = jnp.where(has_prev, st, -jnp.inf)
        mx = jnp.maximum(jnp.max(st, axis=0, keepdims=True), sink)
        p = jnp.exp2(st - mx).astype(BF16)
        sink_p = jnp.exp2(sink - mx)
        pieces = []
        for kv in range(A_KV):
            vt = jnp.concatenate([vt_scr[kv * A_HD:(kv + 1) * A_HD, j * WINDOW:(j + 2) * WINDOW], ones], axis=0)
            ot = _dot(vt, p[:, kv * lanes:(kv + 1) * lanes])
            den = ot[A_HD:A_HD + 1] + sink_p[:, kv * lanes:(kv + 1) * lanes]
            ot = ot[:A_HD] * (1.0 / den)
            pieces.extend(ot[:, g * WINDOW:(g + 1) * WINDOW] for g in range(A_GROUP))
        bb_scr[j * WINDOW:(j + 1) * WINDOW, :] = jnp.concatenate(pieces, axis=0).T

    merged = gated_a + gate_b * bb_scr[...]
    o_ref[...] = x_ref[...] + _dot(merged.astype(BF16), w_ref[0])

    @pl.when(i == pl.num_programs(1) - 1)
    def _():
        rows = kc_ref.shape[0]
        pk_ref[0] = kc_ref[rows - WINDOW:rows, :].T
        pv_ref[0] = vc_ref[rows - WINDOW:rows, :].T


def _band_bias():
    c = jnp.arange(2 * WINDOW, dtype=jnp.int32)[:, None]
    t = jnp.arange(A_HEADS * WINDOW, dtype=jnp.int32)[None, :] % WINDOW
    return jnp.where((c > t) & (c <= t + WINDOW), 0.0, -jnp.inf).astype(F32)


def _swa_merge_prompt(aq_t, pf, sinks, bias, x, xn, ba, w_gates, w_out, layer, batch, seq):
    tq = _pick(seq, (SWA_STEP, 256, 128))
    ns = seq // tq
    per = tq // WINDOW
    nb = seq // WINDOW
    cur = lambda col: (lambda b, i: (b * ns + i, col))
    prev = lambda col: (lambda b, i: (b * nb + jnp.maximum(i * per - 1, 0), col))
    const = lambda b, i: (0, 0)
    row = pl.BlockSpec((tq, D_MODEL), cur(0))
    return pl.pallas_call(
        _swa_merge_prompt_kernel,
        grid=(batch, ns),
        in_specs=[
            pl.BlockSpec(memory_space=pltpu.SMEM),
            pl.BlockSpec(bias.shape, const),
            pl.BlockSpec((D_MODEL, tq), lambda b, i: (0, b * ns + i)),
            pl.BlockSpec((WINDOW, KV_W), prev(0)),
            pl.BlockSpec((tq, KV_W), cur(0)),
            pl.BlockSpec((WINDOW, KV_W), prev(1)),
            pl.BlockSpec((tq, KV_W), cur(1)),
            row, row, row,
            pl.BlockSpec((1, 2 * D_MODEL, D_MODEL), lambda b, i: (layer, 0, 0)),
            pl.BlockSpec((1, D_MODEL, D_MODEL), lambda b, i: (layer, 0, 0)),
        ],
        out_specs=[
            pl.BlockSpec((tq, D_MODEL), lambda b, i: (b * ns + i, 0)),
            pl.BlockSpec((1, KV_W, WINDOW), lambda b, i: (b, 0, 0)),
            pl.BlockSpec((1, KV_W, WINDOW), lambda b, i: (b, 0, 0)),
        ],
        out_shape=[
            jax.ShapeDtypeStruct((batch * seq, D_MODEL), F32),
            jax.ShapeDtypeStruct((batch, KV_W, WINDOW), F32),
            jax.ShapeDtypeStruct((batch, KV_W, WINDOW), F32),
        ],
        scratch_shapes=[pltpu.VMEM((WINDOW + tq, KV_W), BF16), pltpu.VMEM((KV_W, WINDOW + tq), BF16),
                        pltpu.VMEM((tq, D_MODEL), F32)],
        compiler_params=_params("parallel", "arbitrary"),
        name="swa_merge_prompt",
    )(sinks, bias, aq_t, pf, pf, pf, pf, x, xn, ba, w_gates, w_out)


def _bdot(a, b, contract_b):
    return lax.dot_general(a, b, (((2,), (contract_b,)), ((0,), (0,))), preferred_element_type=F32)


def _swa_sample_kernel(*refs, tq):
    sink_ref, q_ref, kn_ref, vn_ref, ck_ref, cv_ref = refs[:6]
    o_ref, sk_ref, sv_ref = refs[-3:]
    nb = ck_ref.shape[1]
    wc = ck_ref.shape[3]
    keep = wc - tq
    q3 = q_ref[...].astype(F32).reshape(nb, tq, D_MODEL)
    ck = ck_ref[0]
    cv = cv_ref[0]

    lane = lax.broadcasted_iota(jnp.int32, (nb, KV_W, wc), 2)

    def appended(cache, new_rows):
        new_t = new_rows.T
        placed = jnp.stack([pltpu.roll(new_t, (keep - b * tq) % wc, axis=1) for b in range(nb)])
        return jnp.where(lane >= keep, placed, pltpu.roll(cache, keep, axis=2))

    sk = appended(ck, kn_ref[...])
    sv = appended(cv, vn_ref[...])
    sk_ref[0] = sk
    sv_ref[0] = sv

    rows = A_GROUP * tq
    t_idx = lax.broadcasted_iota(jnp.int32, (1, rows, 1), 1) & (tq - 1)
    c_idx = lax.broadcasted_iota(jnp.int32, (1, 1, wc), 2)
    mask_old = c_idx > t_idx
    mask_new = (c_idx >= keep) & (c_idx - keep <= t_idx)
    g_idx = lax.broadcasted_iota(jnp.int32, (1, rows, 1), 1) >> (tq.bit_length() - 1)
    pieces = []
    for kv in range(A_KV):
        heads = [kv * A_GROUP + g for g in range(A_GROUP)]
        q4 = jnp.concatenate([q3[:, :, h * A_HD:(h + 1) * A_HD] for h in heads], axis=1).astype(BF16)
        sink = jnp.zeros((1, rows, 1), F32)
        for g in range(A_GROUP):
            sink = jnp.where(g_idx == g, sink_ref[heads[g]] * LOG2E, sink)
        sl = slice(kv * A_HD, (kv + 1) * A_HD)
        s1 = jnp.where(mask_old, _bdot(q4, ck[:, sl, :].astype(BF16), 1), -jnp.inf)
        s2 = jnp.where(mask_new, _bdot(q4, sk[:, sl, :].astype(BF16), 1), -jnp.inf)
        mx = jnp.maximum(jnp.maximum(jnp.max(s1, axis=2, keepdims=True),
                                     jnp.max(s2, axis=2, keepdims=True)), sink)
        p1 = jnp.exp2(s1 - mx)
        p2 = jnp.exp2(s2 - mx)
        den = (jnp.sum(p1, axis=2, keepdims=True) + jnp.sum(p2, axis=2, keepdims=True)
               + jnp.exp2(sink - mx))
        r = 1.0 / den
        o = (_bdot((p1 * r).astype(BF16), cv[:, sl, :].astype(BF16), 2)
             + _bdot((p2 * r).astype(BF16), sv[:, sl, :].astype(BF16), 2))
        pieces.extend(o[:, g * tq:(g + 1) * tq, :] for g in range(A_GROUP))
    o_ref[...] = jnp.concatenate(pieces, axis=2).reshape(nb * tq, D_MODEL)


def _swa_sample(aq, pf, sinks, cache_kt, cache_vt, carried, layer, dec_batch, tq):
    lt = SAMPLE_TILE
    nb = lt // tq
    wc = cache_kt.shape[3]
    assert wc == lt, "the appended keys are placed with lane rolls over one cache row"
    any_spec = pl.BlockSpec(memory_space=pl.ANY)
    cache_spec = pl.BlockSpec((1, nb, KV_W, wc), lambda i: (layer, i, 0, 0))
    operands = [sinks, aq, pf, pf, cache_kt, cache_vt]
    in_specs = [
        pl.BlockSpec(memory_space=pltpu.SMEM),
        pl.BlockSpec((lt, D_MODEL), lambda i: (i, 0)),
        pl.BlockSpec((lt, KV_W), lambda i: (i, 0)),
        pl.BlockSpec((lt, KV_W), lambda i: (i, 1)),
        cache_spec, cache_spec,
    ]
    aliases = {}
    if carried is not None:
        aliases = {len(operands) + k: 1 + k for k in range(2)}
        operands += list(carried)
        in_specs += [any_spec] * 2
    return pl.pallas_call(
        functools.partial(_swa_sample_kernel, tq=tq),
        grid=(dec_batch // nb,),
        in_specs=in_specs,
        out_specs=[
            pl.BlockSpec((lt, D_MODEL), lambda i: (i, 0)),
            cache_spec, cache_spec,
        ],
        out_shape=[
            jax.ShapeDtypeStruct((dec_batch * tq, D_MODEL), F32),
            jax.ShapeDtypeStruct(cache_kt.shape, F32),
            jax.ShapeDtypeStruct(cache_vt.shape, F32),
        ],
        input_output_aliases=aliases,
        compiler_params=_params("parallel"),
        name="swa_sample",
    )(*operands)


def _merge_kernel(x_ref, xn_ref, wg_ref, ba_ref, bb_ref, w_ref, o_ref):
    gates = _dot_nt(xn_ref[...], wg_ref[0])
    merged = _sigmoid(gates[:, :D_MODEL]) * ba_ref[...] + _sigmoid(gates[:, D_MODEL:]) * bb_ref[...]
    o_ref[...] = x_ref[...] + _dot(merged.astype(BF16), w_ref[0])


def _merge(x, xn, w_gates, ba, bb, w_out, layer):
    n = x.shape[0]
    tm = _pick(n, (512, 256, 128))
    row = pl.BlockSpec((tm, D_MODEL), lambda i: (i, 0))
    weight = lambda rows: pl.BlockSpec((1, rows, D_MODEL), lambda i: (layer, 0, 0))
    return pl.pallas_call(
        _merge_kernel,
        grid=(n // tm,),
        in_specs=[row, row, weight(2 * D_MODEL), row, row, weight(D_MODEL)],
        out_specs=row,
        out_shape=jax.ShapeDtypeStruct((n, D_MODEL), F32),
        compiler_params=_params("parallel"),
        name="merge_out_proj",
    )(x, xn, w_gates, ba, bb, w_out)


def _mlp_kernel(x_ref, g_ref, wu_ref, wd_ref, gf_ref, o_ref, xn_ref, *, final_norm):
    j = pl.program_id(1)

    @pl.when(j == 0)
    def _():
        x = x_ref[...]
        xn_ref[...] = _rms(x, g_ref[...]).astype(BF16)
        o_ref[...] = x

    h = jnp.square(jnp.maximum(_dot(xn_ref[...], wu_ref[0]), 0.0))
    o_ref[...] += _dot(h.astype(BF16), wd_ref[0])

    if final_norm:
        @pl.when(j == pl.num_programs(1) - 1)
        def _():
            o_ref[...] = _rms(o_ref[...], gf_ref[...])


def _mlp(x, g, w_up, w_down, g_final, layer, final_norm):
    n = x.shape[0]
    tm = _pick(n, (1024, 512, 256, 128))
    tf = 2048
    return pl.pallas_call(
        functools.partial(_mlp_kernel, final_norm=final_norm),
        grid=(n // tm, D_FF // tf),
        in_specs=[
            pl.BlockSpec((tm, D_MODEL), lambda i, j: (i, 0)),
            pl.BlockSpec((1, D_MODEL), lambda i, j: (0, 0)),
            pl.BlockSpec((1, D_MODEL, tf), lambda i, j: (layer, 0, j)),
            pl.BlockSpec((1, tf, D_MODEL), lambda i, j: (layer, j, 0)),
            pl.BlockSpec((1, D_MODEL), lambda i, j: (0, 0)),
        ],
        out_specs=pl.BlockSpec((tm, D_MODEL), lambda i, j: (i, 0)),
        out_shape=jax.ShapeDtypeStruct((n, D_MODEL), F32),
        scratch_shapes=[pltpu.VMEM((tm, D_MODEL), BF16)],
        compiler_params=_params("parallel", "arbitrary"),
        name="mlp",
    )(x, g, w_up, w_down, g_final)


def _prep_in_proj_weights(w_in):
    assert w_in.shape[2] == D_IN
    w_all = jnp.swapaxes(w_in, 1, 2).astype(BF16)
    return w_all, w_all[:, ROW_AQ:ROW_MERGE], w_all[:, ROW_MERGE:D_IN]


def kernel(x_prompt, x_sample, state_C, state_n, state_m, cache_k, cache_v, norm_attn, w_in, b_gate,
           mlstm_norm, sinks, w_out, norm_mlp, w_up, w_down, norm_final):
    batch, seq, _ = x_prompt.shape
    dec_batch, dec_seq, _ = x_sample.shape
    depth = w_in.shape[0]
    wc = cache_k.shape[2]

    w_all, w_att, w_merge = _prep_in_proj_weights(w_in)
    b_gate_col = jnp.zeros((depth, GATE_ROWS, 1), F32).at[:, :2 * M_HEADS, 0].set(b_gate.astype(F32))
    w_out_b = w_out.astype(BF16)
    w_up_b = w_up.astype(BF16)
    w_down_b = w_down.astype(BF16)
    bias = _band_bias()

    xp = x_prompt.reshape(batch * seq, D_MODEL)
    xs = x_sample.reshape(dec_batch * dec_seq, D_MODEL)
    m_tok = jnp.repeat(jnp.swapaxes(state_m, 1, 2), dec_seq, axis=2)[..., None]
    cache_kt = jnp.transpose(cache_k, (0, 1, 3, 4, 2)).reshape(depth, dec_batch, KV_W, wc)
    cache_vt = jnp.transpose(cache_v, (0, 1, 3, 4, 2)).reshape(depth, dec_batch, KV_W, wc)

    states = None
    caches = None
    p_c, p_n, p_m, p_k, p_v = [], [], [], [], []
    g_final = norm_final[None]

    for l in range(depth):
        g_attn = norm_attn[l][None]
        nw = mlstm_norm[l][None]
        g_mlp = norm_mlp[l][None]
        last = l == depth - 1
        proj_w = (w_all, w_att, b_gate_col[l], l)

        xn, mv, mq, pf, kt, gt, aq_t = _in_proj(xp, g_attn, *proj_w, q_transposed=True)
        ba, pc, pn, pm = _mlstm_prompt(xn, w_all, mv, mq, kt, gt, nw, l, batch, seq)
        xp, pk, pv = _swa_merge_prompt(aq_t, pf, sinks[l], bias, xp, xn, ba, w_merge, w_out_b, l, batch, seq)
        xp = _mlp(xp, g_mlp, w_up_b, w_down_b, g_final, l, last)
        p_c.append(pc)
        p_n.append(pn)
        p_m.append(pm[:, :M_HEADS, 0])
        p_k.append(pk)
        p_v.append(pv)

        xn, mv, mq, pf, kt, gt, aq = _in_proj(xs, g_attn, *proj_w, q_transposed=False)
        ba, *states = _mlstm_sample(xn, w_all, mv, mq, kt, gt, nw, m_tok, state_C, state_n, states,
                                    l, dec_batch, dec_seq)
        bb, *caches = _swa_sample(aq, pf, sinks[l], cache_kt, cache_vt, caches, l, dec_batch, dec_seq)
        xs = _merge(xs, xn, w_merge, ba, bb, w_out_b, l)
        xs = _mlp(xs, g_mlp, w_up_b, w_down_b, g_final, l, last)

    s_c, s_n, s_m = states
    s_k, s_v = caches

    def positions_major(t, lead):
        t = t.reshape(*lead, A_KV, A_HD, wc)
        return jnp.moveaxis(t, -1, -3)

    return (xp.reshape(batch, seq, D_MODEL), xs.reshape(dec_batch, dec_seq, D_MODEL),
            jnp.stack(p_c), jnp.stack(p_n), jnp.stack(p_m),
            positions_major(jnp.stack(p_k), (depth, batch)), positions_major(jnp.stack(p_v), (depth, batch)),
            s_c, s_n, jnp.swapaxes(s_m[..., 0], 1, 2),
            positions_major(s_k, (depth, dec_batch)), positions_major(s_v, (depth, dec_batch)))
```

```python
import functools

import jax
import jax.numpy as jnp
from jax import lax
from jax.experimental import pallas as pl
from jax.experimental.pallas import tpu as pltpu

F32 = jnp.float32
BF16 = jnp.bfloat16

D_MODEL = 1024
M_HEADS = 4
M_DQK = 128
M_DV = D_MODEL // M_HEADS
MQ_W = M_HEADS * M_DQK
A_HEADS = 16
A_KV = 4
A_GROUP = A_HEADS // A_KV
A_HD = D_MODEL // A_HEADS
KV_W = A_KV * A_HD
WINDOW = 128
D_FF = 4 * D_MODEL
EPS = 1e-6
LOG2E = 1.4426950408889634

ROW_MQ = 0
ROW_MK = ROW_MQ + MQ_W
ROW_MV = ROW_MK + MQ_W
ROW_MO = ROW_MV + D_MODEL
ROW_GATES = ROW_MO + D_MODEL
ROW_AQ = ROW_GATES + 2 * M_HEADS
ROW_MERGE = ROW_AQ + D_MODEL + 2 * KV_W
D_IN = ROW_MERGE + 2 * D_MODEL
GATE_ROWS = 16

MLSTM_CHUNK = 256
SAMPLE_TILE = 128
SWA_STEP = 512
VMEM_LIMIT = 56 * 1024 * 1024


def _pick(n, candidates):
    for c in candidates:
        if n % c == 0:
            return c
    raise ValueError(f"no block size for {n}")


def _sigmoid(x):
    return 0.5 * jnp.tanh(0.5 * x) + 0.5


def _log_sigmoid(x):
    return jnp.minimum(x, 0.0) - jnp.log1p(jnp.exp(-jnp.abs(x)))


def _rms(x, g):
    y = x * lax.rsqrt(jnp.mean(x * x, axis=-1, keepdims=True) + EPS)
    return y * g


def _dot(a, b):
    return jnp.dot(a, b, preferred_element_type=F32)


def _dot_nt(a, b):
    return lax.dot_general(a, b, (((1,), (1,)), ((), ())), preferred_element_type=F32)


def _params(*sem):
    return pltpu.CompilerParams(dimension_semantics=sem, vmem_limit_bytes=VMEM_LIMIT)


def _in_proj_kernel(x_ref, g_ref, wmq_ref, wmk_ref, wmv_ref, wg_ref, waq_ref, wkv_ref, bg_ref,
                    xn_out, mv_ref, mq_ref, pf_ref, kt_ref, gt_ref, aq_ref, *, q_transposed):
    xn = _rms(x_ref[...], g_ref[...]).astype(BF16)
    xn_out[...] = xn
    kt_ref[...] = _dot_nt(wmk_ref[0], xn).astype(BF16)
    gt_ref[...] = _dot_nt(wg_ref[0], xn) + bg_ref[...]
    mv_ref[...] = _dot_nt(xn, wmv_ref[0]).astype(BF16)
    if q_transposed:
        aq_ref[...] = (_dot_nt(waq_ref[0], xn) * (A_HD ** -0.5 * LOG2E)).astype(BF16)
    else:
        aq_ref[...] = (_dot_nt(xn, waq_ref[0]) * (A_HD ** -0.5 * LOG2E)).astype(BF16)
    mq_ref[...] = (_dot_nt(xn, wmq_ref[0]) * (M_DQK ** -0.5)).astype(BF16)
    pf_ref[...] = _dot_nt(xn, wkv_ref[0])


def _in_proj(x, g, w_all, w_att, bg, layer, q_transposed):
    n = x.shape[0]
    tm = _pick(n, (1024, 512, 256, 128))
    const = lambda i: (0, 0)
    row = lambda i: (i, 0)
    col = lambda i: (0, i)
    wblk = lambda rows, off: pl.BlockSpec((1, rows, D_MODEL), lambda i: (layer, off // rows, 0))
    if q_transposed:
        aq_spec, aq_shape = pl.BlockSpec((D_MODEL, tm), col), (D_MODEL, n)
    else:
        aq_spec, aq_shape = pl.BlockSpec((tm, D_MODEL), row), (n, D_MODEL)
    return pl.pallas_call(
        functools.partial(_in_proj_kernel, q_transposed=q_transposed),
        grid=(n // tm,),
        in_specs=[
            pl.BlockSpec((tm, D_MODEL), row),
            pl.BlockSpec((1, D_MODEL), const),
            wblk(MQ_W, ROW_MQ), wblk(MQ_W, ROW_MK), wblk(D_MODEL, ROW_MV), wblk(GATE_ROWS, ROW_GATES),
            wblk(D_MODEL, 0), wblk(2 * KV_W, D_MODEL),
            pl.BlockSpec((GATE_ROWS, 1), const),
        ],
        out_specs=[
            pl.BlockSpec((tm, D_MODEL), row),
            pl.BlockSpec((tm, D_MODEL), row),
            pl.BlockSpec((tm, MQ_W), row),
            pl.BlockSpec((tm, 2 * KV_W), row),
            pl.BlockSpec((MQ_W, tm), col),
            pl.BlockSpec((GATE_ROWS, tm), col),
            aq_spec,
        ],
        out_shape=[
            jax.ShapeDtypeStruct((n, D_MODEL), BF16),
            jax.ShapeDtypeStruct((n, D_MODEL), BF16),
            jax.ShapeDtypeStruct((n, MQ_W), BF16),
            jax.ShapeDtypeStruct((n, 2 * KV_W), F32),
            jax.ShapeDtypeStruct((MQ_W, n), BF16),
            jax.ShapeDtypeStruct((GATE_ROWS, n), F32),
            jax.ShapeDtypeStruct(aq_shape, BF16),
        ],
        compiler_params=_params("parallel"),
        name="in_proj",
    )(x, g, w_all, w_all, w_all, w_all, w_att, w_att, bg)


def _seg_cumsum(x, seg):
    pos = lax.broadcasted_iota(jnp.int32, x.shape, 1) & (seg - 1)
    sh = 1
    while sh < seg:
        x = x + jnp.where(pos >= sh, pltpu.roll(x, sh, axis=1), 0.0)
        sh *= 2
    return x


def _mlstm_finish(num, inv, o_pre, nw):
    msq = jnp.mean(num * num, axis=-1, keepdims=True)
    r = inv * lax.rsqrt(inv * inv * msq + EPS)
    return _sigmoid(o_pre) * (num * r) * nw


def _mlstm_prompt_kernel(*refs, nseq):
    xn_ref, wo_ref, q_ref, v_ref, nw_ref = refs[:5]
    kt_refs = refs[5:5 + nseq]
    gt_refs = refs[5 + nseq:5 + 2 * nseq]
    h_ref, c_out, n_out, m_out, c_scr, n_scr, m_scr = refs[5 + 2 * nseq:]
    c = pl.program_id(0)
    lt = q_ref.shape[1]
    units = [(b, h) for b in range(nseq) for h in range(M_HEADS)]
    idx = range(len(units))

    @pl.when(c == 0)
    def _():
        c_scr[...] = jnp.zeros_like(c_scr)
        n_scr[...] = jnp.zeros_like(n_scr)
        m_scr[...] = jnp.zeros_like(m_scr)

    i2, b2, b2_t = [], [], []
    for b in range(nseq):
        gt = gt_refs[b][...]
        i2.append(gt * LOG2E)
        b2.append(_seg_cumsum(_log_sigmoid(gt), lt) * LOG2E)
        b2_t.append(jnp.concatenate([b2[b], jnp.zeros((128 - GATE_ROWS, lt), F32)], axis=0).T)
    t_idx = lax.broadcasted_iota(jnp.int32, (lt, lt), 0)
    s_idx = lax.broadcasted_iota(jnp.int32, (lt, lt), 1)
    causal = s_idx <= t_idx

    q = [q_ref[b, :, h * M_DQK:(h + 1) * M_DQK] for b, h in units]
    kt = [kt_refs[b][h * M_DQK:(h + 1) * M_DQK, :] for b, h in units]
    v = [v_ref[b, :, h * M_DV:(h + 1) * M_DV] for b, h in units]
    m2_col = [jnp.broadcast_to(m_scr[8 * b + h:8 * b + h + 1, 0:1] * LOG2E, (lt, 1)) for b, h in units]
    n_old = [n_scr[8 * b + h:8 * b + h + 1, :] for b, h in units]
    c_old = [c_scr[M_HEADS * b + h] for b, h in units]

    a_mat = [jnp.where(causal, i2[b][h:h + 1] - b2[b][M_HEADS + h:M_HEADS + h + 1], -jnp.inf)
             for b, h in units]
    g_col = [jnp.maximum(jnp.max(a_mat[u], axis=1, keepdims=True), m2_col[u]) for u in idx]
    w = [jnp.exp2(a_mat[u] - g_col[u]) for u in idx]
    w_int = [jnp.exp2(m2_col[u] - g_col[u]) for u in idx]
    s = [_dot(q[u], kt[u]) * w[u] for u in idx]
    inter = [_dot(q[u], c_old[u].astype(BF16)) for u in idx]
    qn = [_dot_nt(q[u], jnp.broadcast_to(n_old[u], (8, M_DQK)).astype(BF16))[:, 0:1] for u in idx]
    num = [w_int[u] * inter[u] + _dot(s[u].astype(BF16), v[u]) for u in idx]
    den = [w_int[u] * qn[u] + jnp.sum(s[u], axis=1, keepdims=True) for u in idx]
    m2_row = [b2_t[b][:, M_HEADS + h:M_HEADS + h + 1] + g_col[u] for u, (b, h) in enumerate(units)]
    inv = [1.0 / jnp.maximum(jnp.abs(den[u]), jnp.exp2(-m2_row[u])) for u in idx]

    wk_row = [w[u][lt - 1:lt, :] for u in idx]
    c_upd = [_dot((kt[u].astype(F32) * wk_row[u]).astype(BF16), v[u]) for u in idx]
    n_upd = [_dot_nt(jnp.broadcast_to(wk_row[u], (8, lt)).astype(BF16), kt[u]) for u in idx]
    for u, (b, h) in enumerate(units):
        decay = w_int[u][lt - 1:lt]
        c_scr[M_HEADS * b + h] = decay * c_old[u] + c_upd[u]
        n_scr[8 * b + h:8 * b + h + 1, :] = decay * n_old[u] + n_upd[u][0:1]
        m_scr[8 * b + h:8 * b + h + 1, :] = jnp.broadcast_to(
            m2_row[u][lt - 1:lt] * (1.0 / LOG2E), (1, m_scr.shape[1]))

    o_pre = _dot_nt(xn_ref[...].reshape(nseq * lt, D_MODEL), wo_ref[0])
    for u, (b, h) in enumerate(units):
        sl = slice(h * M_DV, (h + 1) * M_DV)
        h_ref[b, :, sl] = _mlstm_finish(num[u], inv[u], o_pre[b * lt:(b + 1) * lt, sl], nw_ref[:, sl])

    @pl.when(c == pl.num_programs(0) - 1)
    def _():
        c_out[...] = c_scr[...].reshape(c_out.shape)
        for b in range(nseq):
            n_out[b] = n_scr[8 * b:8 * b + M_HEADS]
        m_out[...] = m_scr[...].reshape(m_out.shape)


def _mlstm_prompt(xn, w_all, mv, mq, kt, gt, nw, layer, batch, seq):
    lt = _pick(seq, (MLSTM_CHUNK, 128))
    nc = seq // lt
    tok = lambda width: pl.BlockSpec((batch, lt, width), lambda c: (0, c, 0))
    const = lambda shape: pl.BlockSpec(shape, lambda c: (0,) * len(shape))
    lanes = lambda rows: [pl.BlockSpec((rows, lt), lambda c, b=b: (0, b * nc + c)) for b in range(batch)]
    per_seq = lambda a: a.reshape(batch, seq, a.shape[-1])
    w_mo = pl.BlockSpec((1, D_MODEL, D_MODEL), lambda c: (layer, ROW_MO // D_MODEL, 0))
    ba, pc, pn, pm = pl.pallas_call(
        functools.partial(_mlstm_prompt_kernel, nseq=batch),
        grid=(nc,),
        in_specs=[tok(D_MODEL), w_mo, tok(MQ_W), tok(D_MODEL), const((1, D_MODEL))]
        + lanes(MQ_W) + lanes(GATE_ROWS),
        out_specs=[
            tok(D_MODEL),
            const((batch, M_HEADS, M_DQK, M_DV)),
            const((batch, M_HEADS, M_DQK)),
            const((batch, 8, 128)),
        ],
        out_shape=[
            jax.ShapeDtypeStruct((batch, seq, D_MODEL), F32),
            jax.ShapeDtypeStruct((batch, M_HEADS, M_DQK, M_DV), F32),
            jax.ShapeDtypeStruct((batch, M_HEADS, M_DQK), F32),
            jax.ShapeDtypeStruct((batch, 8, 128), F32),
        ],
        scratch_shapes=[pltpu.VMEM((batch * M_HEADS, M_DQK, M_DV), F32), pltpu.VMEM((batch * 8, M_DQK), F32),
                        pltpu.VMEM((batch * 8, 128), F32)],
        compiler_params=_params("arbitrary"),
        name="mlstm_prompt",
    )(per_seq(xn), w_all, per_seq(mq), per_seq(mv), nw, *([kt] * batch), *([gt] * batch))
    return ba.reshape(batch * seq, D_MODEL), pc, pn, pm


def _mlstm_sample_kernel(*refs, seg):
    xn_ref, wo_ref, q_ref, kt_ref, v_ref, gt_ref, nw_ref, m_ref, c_in, n_in = refs[:10]
    h_ref, c_out, n_out, m_out = refs[-4:]
    hd = pl.program_id(1)
    lt = q_ref.shape[0]
    nseg = lt // seg
    gt = gt_ref[...]
    lf = _log_sigmoid(gt)
    bcs = _seg_cumsum(lf, seg)
    row = lax.broadcasted_iota(jnp.int32, (GATE_ROWS, lt), 0)

    def pick(x, r):
        return jnp.sum(jnp.where(row == r, x, 0.0), axis=0, keepdims=True)

    i_row, lf_row, b_row = pick(gt, hd), pick(lf, M_HEADS + hd), pick(bcs, M_HEADS + hd)
    t_idx = lax.broadcasted_iota(jnp.int32, (lt, lt), 0)
    s_idx = lax.broadcasted_iota(jnp.int32, (lt, lt), 1)
    causal = (s_idx <= t_idx) & ((s_idx | (seg - 1)) == (t_idx | (seg - 1)))
    last = t_idx == (s_idx | (seg - 1))

    q = q_ref[...]
    kt = kt_ref[...]
    v = v_ref[...]
    m_col = m_ref[0, 0]
    n_old = [n_in[0, i, pl.ds(hd, 1), :] for i in range(nseg)]
    n_tok = jnp.concatenate([jnp.broadcast_to(n, (seg, M_DQK)) for n in n_old], axis=0)

    a_mat = jnp.where(causal, i_row - b_row, -jnp.inf)
    g_col = jnp.maximum(jnp.max(a_mat, axis=1, keepdims=True), m_col)
    b_col = jnp.sum(jnp.where(causal, lf_row, 0.0), axis=1, keepdims=True)
    w = jnp.exp(a_mat - g_col)
    w_int = jnp.exp(m_col - g_col)
    s = _dot(q, kt) * w
    inter = jnp.concatenate(
        [_dot(q[i * seg:(i + 1) * seg], c_in[0, i, 0].astype(BF16)) for i in range(nseg)], axis=0)
    qn = jnp.sum(q.astype(F32) * n_tok, axis=1, keepdims=True)
    num = w_int * inter + _dot(s.astype(BF16), v)
    den = w_int * qn + jnp.sum(s, axis=1, keepdims=True)
    m_row = b_col + g_col
    inv = 1.0 / jnp.maximum(jnp.abs(den), jnp.exp(-m_row))

    wk_row = jnp.sum(jnp.where(last, w, 0.0), axis=0, keepdims=True)
    ktw = kt.astype(F32) * wk_row
    lane_seg = lax.broadcasted_iota(jnp.int32, (1, lt), 1) | (seg - 1)
    lhs = jnp.concatenate(
        [jnp.where(lane_seg == i * seg + seg - 1, ktw, 0.0).astype(BF16) for i in range(nseg)], axis=0)
    c_upd = _dot(lhs, v)
    seg_row = lax.broadcasted_iota(jnp.int32, (lt, 1), 0) * seg + (seg - 1)
    n_upd = _dot_nt(jnp.where(lane_seg == seg_row, wk_row, 0.0).astype(BF16), kt)
    new_m = []
    for i in range(nseg):
        r = i * seg + seg - 1
        decay = w_int[r:r + 1]
        c_out[0, i, 0] = decay * c_in[0, i, 0] + c_upd[i * M_DQK:(i + 1) * M_DQK]
        n_out[0, i, pl.ds(hd, 1), :] = decay * n_old[i] + n_upd[i:i + 1]
        new_m.append(m_row[r:r + 1])
    m_out[0, 0] = jnp.concatenate(new_m, axis=0)
    h_ref[...] = _mlstm_finish(num, inv, _dot_nt(xn_ref[...], wo_ref[0]), nw_ref[...])


def _mlstm_sample(xn, w_all, mv, mq, kt, gt, nw, m_tok, state_c, state_n, carried, layer, dec_batch, seg):
    lt = SAMPLE_TILE
    nseg = lt // seg
    depth = state_c.shape[0]
    any_spec = pl.BlockSpec(memory_space=pl.ANY)
    c_spec = pl.BlockSpec((1, nseg, 1, M_DQK, M_DV), lambda t, h: (layer, t, h, 0, 0))
    n_spec = pl.BlockSpec((1, nseg, M_HEADS, M_DQK), lambda t, h: (layer, t, 0, 0))
    operands = [xn, w_all, mq, kt, mv, gt, nw, m_tok, state_c, state_n]
    in_specs = [
        pl.BlockSpec((lt, D_MODEL), lambda t, h: (t, 0)),
        pl.BlockSpec((1, M_DV, D_MODEL), lambda t, h: (layer, ROW_MO // M_DV + h, 0)),
        pl.BlockSpec((lt, M_DQK), lambda t, h: (t, h)),
        pl.BlockSpec((M_DQK, lt), lambda t, h: (h, t)),
        pl.BlockSpec((lt, M_DV), lambda t, h: (t, h)),
        pl.BlockSpec((GATE_ROWS, lt), lambda t, h: (0, t)),
        pl.BlockSpec((1, M_DV), lambda t, h: (0, h)),
        pl.BlockSpec((1, 1, lt, 1), lambda t, h: (layer, h, t, 0)),
        c_spec, n_spec,
    ]
    aliases = {}
    if carried is not None:
        aliases = {len(operands) + k: 1 + k for k in range(3)}
        operands += list(carried)
        in_specs += [any_spec] * 3
    return pl.pallas_call(
        functools.partial(_mlstm_sample_kernel, seg=seg),
        grid=(dec_batch // nseg, M_HEADS),
        in_specs=in_specs,
        out_specs=[
            pl.BlockSpec((lt, M_DV), lambda t, h: (t, h)),
            c_spec, n_spec,
            pl.BlockSpec((1, 1, nseg, 1), lambda t, h: (layer, h, t, 0)),
        ],
        out_shape=[
            jax.ShapeDtypeStruct((dec_batch * seg, D_MODEL), F32),
            jax.ShapeDtypeStruct(state_c.shape, F32),
            jax.ShapeDtypeStruct(state_n.shape, F32),
            jax.ShapeDtypeStruct((depth, M_HEADS, dec_batch, 1), F32),
        ],
        input_output_aliases=aliases,
        compiler_params=_params("parallel", "arbitrary"),
        name="mlstm_sample",
    )(*operands)


def _swa_merge_prompt_kernel(sink_ref, bias_ref, qt_ref, kp_ref, kc_ref, vp_ref, vc_ref,
                             x_ref, xn_ref, ba_ref, wg_ref, w_ref,
                             o_ref, pk_ref, pv_ref, kb_scr, vt_scr, bb_scr):
    i = pl.program_id(1)
    nsub = qt_ref.shape[1] // WINDOW
    lanes = A_GROUP * WINDOW
    all_lanes = A_HEADS * WINDOW
    gates = _dot_nt(xn_ref[...], wg_ref[0])
    gated_a = _sigmoid(gates[:, :D_MODEL]) * ba_ref[...]
    gate_b = _sigmoid(gates[:, D_MODEL:])

    kb_scr[0:WINDOW] = kp_ref[...].astype(BF16)
    kb_scr[WINDOW:] = kc_ref[...].astype(BF16)
    vt_scr[:, 0:WINDOW] = vp_ref[...].T.astype(BF16)
    vt_scr[:, WINDOW:] = vc_ref[...].T.astype(BF16)

    lane_head = lax.broadcasted_iota(jnp.int32, (1, all_lanes), 1) >> (WINDOW.bit_length() - 1)
    sink = jnp.zeros((1, all_lanes), F32)
    for h in range(A_HEADS):
        sink = jnp.where(lane_head == h, sink_ref[h] * LOG2E, sink)
    has_prev = (lax.broadcasted_iota(jnp.int32, (2 * WINDOW, 1), 0) >= WINDOW) | (i > 0)
    zeros = jnp.zeros((A_HD, lanes), BF16)
    ones = jnp.ones((16, 2 * WINDOW), BF16)

    for j in range(nsub):
        qt = qt_ref[:, j * WINDOW:(j + 1) * WINDOW]
        qt4 = [jnp.concatenate([qt[(kv * A_GROUP + g) * A_HD:(kv * A_GROUP + g + 1) * A_HD]
                                for g in range(A_GROUP)], axis=1) for kv in range(A_KV)]
        st = []
        for pair in range(A_KV // 2):
            rhs = jnp.concatenate([jnp.concatenate([qt4[2 * pair], zeros], axis=1),
                                   jnp.concatenate([zeros, qt4[2 * pair + 1]], axis=1)], axis=0)
            kpair = kb_scr[j * WINDOW:(j + 2) * WINDOW, pair * 128:(pair + 1) * 128]
            st.append(_dot(kpair, rhs))
        st = jnp.concatenate(st, axis=1) + bias_ref[...]
        if j == 0:
            st = jnp.where(has_prev, st, -jnp.inf)
        mx = jnp.maximum(jnp.max(st, axis=0, keepdims=True), sink)
        p = jnp.exp2(st - mx).astype(BF16)
        sink_p = jnp.exp2(sink - mx)
        pieces = []
        for kv in range(A_KV):
            vt = jnp.concatenate([vt_scr[kv * A_HD:(kv + 1) * A_HD, j * WINDOW:(j + 2) * WINDOW], ones], axis=0)
            ot = _dot(vt, p[:, kv * lanes:(kv + 1) * lanes])
            den = ot[A_HD:A_HD + 1] + sink_p[:, kv * lanes:(kv + 1) * lanes]
            ot = ot[:A_HD] * (1.0 / den)
            pieces.extend(ot[:, g * WINDOW:(g + 1) * WINDOW] for g in range(A_GROUP))
        bb_scr[j * WINDOW:(j + 1) * WINDOW, :] = jnp.concatenate(pieces, axis=0).T

    merged = gated_a + gate_b * bb_scr[...]
    o_ref[...] = x_ref[...] + _dot(merged.astype(BF16), w_ref[0])

    @pl.when(i == pl.num_programs(1) - 1)
    def _():
        rows = kc_ref.shape[0]
        pk_ref[0] = kc_ref[rows - WINDOW:rows, :].T
        pv_ref[0] = vc_ref[rows - WINDOW:rows, :].T


def _band_bias():
    c = jnp.arange(2 * WINDOW, dtype=jnp.int32)[:, None]
    t = jnp.arange(A_HEADS * WINDOW, dtype=jnp.int32)[None, :] % WINDOW
    return jnp.where((c > t) & (c <= t + WINDOW), 0.0, -jnp.inf).astype(F32)


def _swa_merge_prompt(aq_t, pf, sinks, bias, x, xn, ba, w_gates, w_out, layer, batch, seq):
    tq = _pick(seq, (SWA_STEP, 256, 128))
    ns = seq // tq
    per = tq // WINDOW
    nb = seq // WINDOW
    cur = lambda col: (lambda b, i: (b * ns + i, col))
    prev = lambda col: (lambda b, i: (b * nb + jnp.maximum(i * per - 1, 0), col))
    const = lambda b, i: (0, 0)
    row = pl.BlockSpec((tq, D_MODEL), cur(0))
    return pl.pallas_call(
        _swa_merge_prompt_kernel,
        grid=(batch, ns),
        in_specs=[
            pl.BlockSpec(memory_space=pltpu.SMEM),
            pl.BlockSpec(bias.shape, const),
            pl.BlockSpec((D_MODEL, tq), lambda b, i: (0, b * ns + i)),
            pl.BlockSpec((WINDOW, KV_W), prev(0)),
            pl.BlockSpec((tq, KV_W), cur(0)),
            pl.BlockSpec((WINDOW, KV_W), prev(1)),
            pl.BlockSpec((tq, KV_W), cur(1)),
            row, row, row,
            pl.BlockSpec((1, 2 * D_MODEL, D_MODEL), lambda b, i: (layer, 0, 0)),
            pl.BlockSpec((1, D_MODEL, D_MODEL), lambda b, i: (layer, 0, 0)),
        ],
        out_specs=[
            pl.BlockSpec((tq, D_MODEL), lambda b, i: (b * ns + i, 0)),
            pl.BlockSpec((1, KV_W, WINDOW), lambda b, i: (b, 0, 0)),
            pl.BlockSpec((1, KV_W, WINDOW), lambda b, i: (b, 0, 0)),
        ],
        out_shape=[
            jax.ShapeDtypeStruct((batch * seq, D_MODEL), F32),
            jax.ShapeDtypeStruct((batch, KV_W, WINDOW), F32),
            jax.ShapeDtypeStruct((batch, KV_W, WINDOW), F32),
        ],
        scratch_shapes=[pltpu.VMEM((WINDOW + tq, KV_W), BF16), pltpu.VMEM((KV_W, WINDOW + tq), BF16),
                        pltpu.VMEM((tq, D_MODEL), F32)],
        compiler_params=_params("parallel", "arbitrary"),
        name="swa_merge_prompt",
    )(sinks, bias, aq_t, pf, pf, pf, pf, x, xn, ba, w_gates, w_out)


def _bdot(a, b, contract_b):
    return lax.dot_general(a, b, (((2,), (contract_b,)), ((0,), (0,))), preferred_element_type=F32)


def _swa_sample_kernel(*refs, tq):
    sink_ref, q_ref, kn_ref, vn_ref, ck_ref, cv_ref = refs[:6]
    o_ref, sk_ref, sv_ref = refs[-3:]
    nb = ck_ref.shape[1]
    wc = ck_ref.shape[3]
    keep = wc - tq
    q3 = q_ref[...].astype(F32).reshape(nb, tq, D_MODEL)
    ck = ck_ref[0]
    cv = cv_ref[0]

    lane = lax.broadcasted_iota(jnp.int32, (nb, KV_W, wc), 2)

    def appended(cache, new_rows):
        new_t = new_rows.T
        placed = jnp.stack([pltpu.roll(new_t, (keep - b * tq) % wc, axis=1) for b in range(nb)])
        return jnp.where(lane >= keep, placed, pltpu.roll(cache, keep, axis=2))

    sk = appended(ck, kn_ref[...])
    sv = appended(cv, vn_ref[...])
    sk_ref[0] = sk
    sv_ref[0] = sv

    rows = A_GROUP * tq
    t_idx = lax.broadcasted_iota(jnp.int32, (1, rows, 1), 1) & (tq - 1)
    c_idx = lax.broadcasted_iota(jnp.int32, (1, 1, wc), 2)
    mask_old = c_idx > t_idx
    mask_new = (c_idx >= keep) & (c_idx - keep <= t_idx)
    g_idx = lax.broadcasted_iota(jnp.int32, (1, rows, 1), 1) >> (tq.bit_length() - 1)
    pieces = []
    for kv in range(A_KV):
        heads = [kv * A_GROUP + g for g in range(A_GROUP)]
        q4 = jnp.concatenate([q3[:, :, h * A_HD:(h + 1) * A_HD] for h in heads], axis=1).astype(BF16)
        sink = jnp.zeros((1, rows, 1), F32)
        for g in range(A_GROUP):
            sink = jnp.where(g_idx == g, sink_ref[heads[g]] * LOG2E, sink)
        sl = slice(kv * A_HD, (kv + 1) * A_HD)
        s1 = jnp.where(mask_old, _bdot(q4, ck[:, sl, :].astype(BF16), 1), -jnp.inf)
        s2 = jnp.where(mask_new, _bdot(q4, sk[:, sl, :].astype(BF16), 1), -jnp.inf)
        mx = jnp.maximum(jnp.maximum(jnp.max(s1, axis=2, keepdims=True),
                                     jnp.max(s2, axis=2, keepdims=True)), sink)
        p1 = jnp.exp2(s1 - mx)
        p2 = jnp.exp2(s2 - mx)
        den = (jnp.sum(p1, axis=2, keepdims=True) + jnp.sum(p2, axis=2, keepdims=True)
               + jnp.exp2(sink - mx))
        r = 1.0 / den
        o = (_bdot((p1 * r).astype(BF16), cv[:, sl, :].astype(BF16), 2)
             + _bdot((p2 * r).astype(BF16), sv[:, sl, :].astype(BF16), 2))
        pieces.extend(o[:, g * tq:(g + 1) * tq, :] for g in range(A_GROUP))
    o_ref[...] = jnp.concatenate(pieces, axis=2).reshape(nb * tq, D_MODEL)


def _swa_sample(aq, pf, sinks, cache_kt, cache_vt, carried, layer, dec_batch, tq):
    lt = SAMPLE_TILE
    nb = lt // tq
    wc = cache_kt.shape[3]
    assert wc == lt, "the appended keys are placed with lane rolls over one cache row"
    any_spec = pl.BlockSpec(memory_space=pl.ANY)
    cache_spec = pl.BlockSpec((1, nb, KV_W, wc), lambda i: (layer, i, 0, 0))
    operands = [sinks, aq, pf, pf, cache_kt, cache_vt]
    in_specs = [
        pl.BlockSpec(memory_space=pltpu.SMEM),
        pl.BlockSpec((lt, D_MODEL), lambda i: (i, 0)),
        pl.BlockSpec((lt, KV_W), lambda i: (i, 0)),
        pl.BlockSpec((lt, KV_W), lambda i: (i, 1)),
        cache_spec, cache_spec,
    ]
    aliases = {}
    if carried is not None:
        aliases = {len(operands) + k: 1 + k for k in range(2)}
        operands += list(carried)
        in_specs += [any_spec] * 2
    return pl.pallas_call(
        functools.partial(_swa_sample_kernel, tq=tq),
        grid=(dec_batch // nb,),
        in_specs=in_specs,
        out_specs=[
            pl.BlockSpec((lt, D_MODEL), lambda i: (i, 0)),
            cache_spec, cache_spec,
        ],
        out_shape=[
            jax.ShapeDtypeStruct((dec_batch * tq, D_MODEL), F32),
            jax.ShapeDtypeStruct(cache_kt.shape, F32),
            jax.ShapeDtypeStruct(cache_vt.shape, F32),
        ],
        input_output_aliases=aliases,
        compiler_params=_params("parallel"),
        name="swa_sample",
    )(*operands)


def _merge_kernel(x_ref, xn_ref, wg_ref, ba_ref, bb_ref, w_ref, o_ref):
    gates = _dot_nt(xn_ref[...], wg_ref[0])
    merged = _sigmoid(gates[:, :D_MODEL]) * ba_ref[...] + _sigmoid(gates[:, D_MODEL:]) * bb_ref[...]
    o_ref[...] = x_ref[...] + _dot(merged.astype(BF16), w_ref[0])


def _merge(x, xn, w_gates, ba, bb, w_out, layer):
    n = x.shape[0]
    tm = _pick(n, (512, 256, 128))
    row = pl.BlockSpec((tm, D_MODEL), lambda i: (i, 0))
    weight = lambda rows: pl.BlockSpec((1, rows, D_MODEL), lambda i: (layer, 0, 0))
    return pl.pallas_call(
        _merge_kernel,
        grid=(n // tm,),
        in_specs=[row, row, weight(2 * D_MODEL), row, row, weight(D_MODEL)],
        out_specs=row,
        out_shape=jax.ShapeDtypeStruct((n, D_MODEL), F32),
        compiler_params=_params("parallel"),
        name="merge_out_proj",
    )(x, xn, w_gates, ba, bb, w_out)


def _mlp_kernel(x_ref, g_ref, wu_ref, wd_ref, gf_ref, o_ref, xn_ref, *, final_norm):
    j = pl.program_id(1)

    @pl.when(j == 0)
    def _():
        x = x_ref[...]
        xn_ref[...] = _rms(x, g_ref[...]).astype(BF16)
        o_ref[...] = x

    h = jnp.square(jnp.maximum(_dot(xn_ref[...], wu_ref[0]), 0.0))
    o_ref[...] += _dot(h.astype(BF16), wd_ref[0])

    if final_norm:
        @pl.when(j == pl.num_programs(1) - 1)
        def _():
            o_ref[...] = _rms(o_ref[...], gf_ref[...])


def _mlp(x, g, w_up, w_down, g_final, layer, final_norm):
    n = x.shape[0]
    tm = _pick(n, (1024, 512, 256, 128))
    tf = 2048
    return pl.pallas_call(
        functools.partial(_mlp_kernel, final_norm=final_norm),
        grid=(n // tm, D_FF // tf),
        in_specs=[
            pl.BlockSpec((tm, D_MODEL), lambda i, j: (i, 0)),
            pl.BlockSpec((1, D_MODEL), lambda i, j: (0, 0)),
            pl.BlockSpec((1, D_MODEL, tf), lambda i, j: (layer, 0, j)),
            pl.BlockSpec((1, tf, D_MODEL), lambda i, j: (layer, j, 0)),
            pl.BlockSpec((1, D_MODEL), lambda i, j: (0, 0)),
        ],
        out_specs=pl.BlockSpec((tm, D_MODEL), lambda i, j: (i, 0)),
        out_shape=jax.ShapeDtypeStruct((n, D_MODEL), F32),
        scratch_shapes=[pltpu.VMEM((tm, D_MODEL), BF16)],
        compiler_params=_params("parallel", "arbitrary"),
        name="mlp",
    )(x, g, w_up, w_down, g_final)


def _prep_in_proj_weights(w_in):
    assert w_in.shape[2] == D_IN
    w_all = jnp.swapaxes(w_in, 1, 2).astype(BF16)
    return w_all, w_all[:, ROW_AQ:ROW_MERGE], w_all[:, ROW_MERGE:D_IN]


def kernel(x_prompt, x_sample, state_C, state_n, state_m, cache_k, cache_v, norm_attn, w_in, b_gate,
           mlstm_norm, sinks, w_out, norm_mlp, w_up, w_down, norm_final):
    batch, seq, _ = x_prompt.shape
    dec_batch, dec_seq, _ = x_sample.shape
    depth = w_in.shape[0]
    wc = cache_k.shape[2]

    w_all, w_att, w_merge = _prep_in_proj_weights(w_in)
    b_gate_col = jnp.zeros((depth, GATE_ROWS, 1), F32).at[:, :2 * M_HEADS, 0].set(b_gate.astype(F32))
    w_out_b = w_out.astype(BF16)
    w_up_b = w_up.astype(BF16)
    w_down_b = w_down.astype(BF16)
    bias = _band_bias()

    xp = x_prompt.reshape(batch * seq, D_MODEL)
    xs = x_sample.reshape(dec_batch * dec_seq, D_MODEL)
    m_tok = jnp.repeat(jnp.swapaxes(state_m, 1, 2), dec_seq, axis=2)[..., None]
    cache_kt = jnp.transpose(cache_k, (0, 1, 3, 4, 2)).reshape(depth, dec_batch, KV_W, wc)
    cache_vt = jnp.transpose(cache_v, (0, 1, 3, 4, 2)).reshape(depth, dec_batch, KV_W, wc)

    states = None
    caches = None
    p_c, p_n, p_m, p_k, p_v = [], [], [], [], []
    g_final = norm_final[None]

    for l in range(depth):
        g_attn = norm_attn[l][None]
        nw = mlstm_norm[l][None]
        g_mlp = norm_mlp[l][None]
        last = l == depth - 1
        proj_w = (w_all, w_att, b_gate_col[l], l)

        xn, mv, mq, pf, kt, gt, aq_t = _in_proj(xp, g_attn, *proj_w, q_transposed=True)
        ba, pc, pn, pm = _mlstm_prompt(xn, w_all, mv, mq, kt, gt, nw, l, batch, seq)
        xp, pk, pv = _swa_merge_prompt(aq_t, pf, sinks[l], bias, xp, xn, ba, w_merge, w_out_b, l, batch, seq)
        xp = _mlp(xp, g_mlp, w_up_b, w_down_b, g_final, l, last)
        p_c.append(pc)
        p_n.append(pn)
        p_m.append(pm[:, :M_HEADS, 0])
        p_k.append(pk)
        p_v.append(pv)

        xn, mv, mq, pf, kt, gt, aq = _in_proj(xs, g_attn, *proj_w, q_transposed=False)
        ba, *states = _mlstm_sample(xn, w_all, mv, mq, kt, gt, nw, m_tok, state_C, state_n, states,
                                    l, dec_batch, dec_seq)
        bb, *caches = _swa_sample(aq, pf, sinks[l], cache_kt, cache_vt, caches, l, dec_batch, dec_seq)
        xs = _merge(xs, xn, w_merge, ba, bb, w_out_b, l)
        xs = _mlp(xs, g_mlp, w_up_b, w_down_b, g_final, l, last)

    s_c, s_n, s_m = states
    s_k, s_v = caches

    def positions_major(t, lead):
        t = t.reshape(*lead, A_KV, A_HD, wc)
        return jnp.moveaxis(t, -1, -3)

    return (xp.reshape(batch, seq, D_MODEL), xs.reshape(dec_batch, dec_seq, D_MODEL),
            jnp.stack(p_c), jnp.stack(p_n), jnp.stack(p_m),
            positions_major(jnp.stack(p_k), (depth, batch)), positions_major(jnp.stack(p_v), (depth, batch)),
            s_c, s_n, jnp.swapaxes(s_m[..., 0], 1, 2),
            positions_major(s_k, (depth, dec_batch)), positions_major(s_v, (depth, dec_batch)))
```

```python
import functools

import jax
import jax.numpy as jnp
from jax import lax
from jax.experimental import pallas as pl
from jax.experimental.pallas import tpu as pltpu

F32 = jnp.float32
BF16 = jnp.bfloat16

D_MODEL = 1024
M_HEADS = 4
M_DQK = 128
M_DV = D_MODEL // M_HEADS
MQ_W = M_HEADS * M_DQK
A_HEADS = 16
A_KV = 4
A_GROUP = A_HEADS // A_KV
A_HD = D_MODEL // A_HEADS
KV_W = A_KV * A_HD
WINDOW = 128
D_FF = 4 * D_MODEL
EPS = 1e-6
LOG2E = 1.4426950408889634

ROW_MQ = 0
ROW_MK = ROW_MQ + MQ_W
ROW_MV = ROW_MK + MQ_W
ROW_MO = ROW_MV + D_MODEL
ROW_GATES = ROW_MO + D_MODEL
ROW_AQ = ROW_GATES + 2 * M_HEADS
ROW_MERGE = ROW_AQ + D_MODEL + 2 * KV_W
D_IN = ROW_MERGE + 2 * D_MODEL
GATE_ROWS = 16

MLSTM_CHUNK = 256
SAMPLE_TILE = 128
SWA_STEP = 512
VMEM_LIMIT = 56 * 1024 * 1024


def _pick(n, candidates):
    for c in candidates:
        if n % c == 0:
            return c
    raise ValueError(f"no block size for {n}")


def _sigmoid(x):
    return 0.5 * jnp.tanh(0.5 * x) + 0.5


def _log_sigmoid(x):
    return jnp.minimum(x, 0.0) - jnp.log1p(jnp.exp(-jnp.abs(x)))


def _rms(x, g):
    y = x * lax.rsqrt(jnp.mean(x * x, axis=-1, keepdims=True) + EPS)
    return y * g


def _dot(a, b):
    return jnp.dot(a, b, preferred_element_type=F32)


def _dot_nt(a, b):
    return lax.dot_general(a, b, (((1,), (1,)), ((), ())), preferred_element_type=F32)


def _params(*sem):
    return pltpu.CompilerParams(dimension_semantics=sem, vmem_limit_bytes=VMEM_LIMIT)


def _in_proj_kernel(x_ref, g_ref, wmq_ref, wmk_ref, wmv_ref, wg_ref, waq_ref, wkv_ref, bg_ref,
                    xn_out, mv_ref, mq_ref, pf_ref, kt_ref, gt_ref, aq_ref, *, q_transposed):
    xn = _rms(x_ref[...], g_ref[...]).astype(BF16)
    xn_out[...] = xn
    kt_ref[...] = _dot_nt(wmk_ref[0], xn).astype(BF16)
    gt_ref[...] = _dot_nt(wg_ref[0], xn) + bg_ref[...]
    mv_ref[...] = _dot_nt(xn, wmv_ref[0]).astype(BF16)
    if q_transposed:
        aq_ref[...] = (_dot_nt(waq_ref[0], xn) * (A_HD ** -0.5 * LOG2E)).astype(BF16)
    else:
        aq_ref[...] = (_dot_nt(xn, waq_ref[0]) * (A_HD ** -0.5 * LOG2E)).astype(BF16)
    mq_ref[...] = (_dot_nt(xn, wmq_ref[0]) * (M_DQK ** -0.5)).astype(BF16)
    pf_ref[...] = _dot_nt(xn, wkv_ref[0])


def _in_proj(x, g, w_all, w_att, bg, layer, q_transposed):
    n = x.shape[0]
    tm = _pick(n, (1024, 512, 256, 128))
    const = lambda i: (0, 0)
    row = lambda i: (i, 0)
    col = lambda i: (0, i)
    wblk = lambda rows, off: pl.BlockSpec((1, rows, D_MODEL), lambda i: (layer, off // rows, 0))
    if q_transposed:
        aq_spec, aq_shape = pl.BlockSpec((D_MODEL, tm), col), (D_MODEL, n)
    else:
        aq_spec, aq_shape = pl.BlockSpec((tm, D_MODEL), row), (n, D_MODEL)
    return pl.pallas_call(
        functools.partial(_in_proj_kernel, q_transposed=q_transposed),
        grid=(n // tm,),
        in_specs=[
            pl.BlockSpec((tm, D_MODEL), row),
            pl.BlockSpec((1, D_MODEL), const),
            wblk(MQ_W, ROW_MQ), wblk(MQ_W, ROW_MK), wblk(D_MODEL, ROW_MV), wblk(GATE_ROWS, ROW_GATES),
            wblk(D_MODEL, 0), wblk(2 * KV_W, D_MODEL),
            pl.BlockSpec((GATE_ROWS, 1), const),
        ],
        out_specs=[
            pl.BlockSpec((tm, D_MODEL), row),
            pl.BlockSpec((tm, D_MODEL), row),
            pl.BlockSpec((tm, MQ_W), row),
            pl.BlockSpec((tm, 2 * KV_W), row),
            pl.BlockSpec((MQ_W, tm), col),
            pl.BlockSpec((GATE_ROWS, tm), col),
            aq_spec,
        ],
        out_shape=[
            jax.ShapeDtypeStruct((n, D_MODEL), BF16),
            jax.ShapeDtypeStruct((n, D_MODEL), BF16),
            jax.ShapeDtypeStruct((n, MQ_W), BF16),
            jax.ShapeDtypeStruct((n, 2 * KV_W), F32),
            jax.ShapeDtypeStruct((MQ_W, n), BF16),
            jax.ShapeDtypeStruct((GATE_ROWS, n), F32),
            jax.ShapeDtypeStruct(aq_shape, BF16),
        ],
        compiler_params=_params("parallel"),
        name="in_proj",
    )(x, g, w_all, w_all, w_all, w_all, w_att, w_att, bg)


def _seg_cumsum(x, seg):
    pos = lax.broadcasted_iota(jnp.int32, x.shape, 1) & (seg - 1)
    sh = 1
    while sh < seg:
        x = x + jnp.where(pos >= sh, pltpu.roll(x, sh, axis=1), 0.0)
        sh *= 2
    return x


def _mlstm_finish(num, d, o_pre, nw):
    msq = jnp.mean(num * num, axis=-1, keepdims=True)
    return _sigmoid(o_pre) * (num * lax.rsqrt(msq + EPS * d * d)) * nw


def _mlstm_prompt_kernel(*refs, nseq):
    xn_ref, wo_ref, q_ref, v_ref, nw_ref = refs[:5]
    kt_refs = refs[5:5 + nseq]
    gt_refs = refs[5 + nseq:5 + 2 * nseq]
    h_ref, c_out, n_out, m_out, c_scr, n_scr, m_scr = refs[5 + 2 * nseq:]
    c = pl.program_id(0)
    lt = q_ref.shape[1]
    units = [(b, h) for b in range(nseq) for h in range(M_HEADS)]
    idx = range(len(units))

    @pl.when(c == 0)
    def _():
        c_scr[...] = jnp.zeros_like(c_scr)
        n_scr[...] = jnp.zeros_like(n_scr)
        m_scr[...] = jnp.zeros_like(m_scr)

    i2, b2, b2_t = [], [], []
    for b in range(nseq):
        gt = gt_refs[b][...]
        i2.append(gt * LOG2E)
        b2.append(_seg_cumsum(_log_sigmoid(gt), lt) * LOG2E)
        b2_t.append(jnp.concatenate([b2[b], jnp.zeros((128 - GATE_ROWS, lt), F32)], axis=0).T)
    t_idx = lax.broadcasted_iota(jnp.int32, (lt, lt), 0)
    s_idx = lax.broadcasted_iota(jnp.int32, (lt, lt), 1)
    causal = s_idx <= t_idx

    q = [q_ref[b, :, h * M_DQK:(h + 1) * M_DQK] for b, h in units]
    kt = [kt_refs[b][h * M_DQK:(h + 1) * M_DQK, :] for b, h in units]
    v = [v_ref[b, :, h * M_DV:(h + 1) * M_DV] for b, h in units]
    m2_col = [jnp.broadcast_to(m_scr[8 * b + h:8 * b + h + 1, 0:1] * LOG2E, (lt, 1)) for b, h in units]
    n_old = [n_scr[8 * b + h:8 * b + h + 1, :] for b, h in units]
    c_old = [c_scr[M_HEADS * b + h] for b, h in units]

    a_mat = [jnp.where(causal, i2[b][h:h + 1] - b2[b][M_HEADS + h:M_HEADS + h + 1], -jnp.inf)
             for b, h in units]
    g_col = [jnp.maximum(jnp.max(a_mat[u], axis=1, keepdims=True), m2_col[u]) for u in idx]
    w = [jnp.exp2(a_mat[u] - g_col[u]) for u in idx]
    w_int = [jnp.exp2(m2_col[u] - g_col[u]) for u in idx]
    s = [_dot(q[u], kt[u]) * w[u] for u in idx]
    inter = [_dot(q[u], c_old[u].astype(BF16)) for u in idx]
    qn = [_dot_nt(q[u], jnp.broadcast_to(n_old[u], (8, M_DQK)).astype(BF16))[:, 0:1] for u in idx]
    num = [w_int[u] * inter[u] + _dot(s[u].astype(BF16), v[u]) for u in idx]
    den = [w_int[u] * qn[u] + jnp.sum(s[u], axis=1, keepdims=True) for u in idx]
    m2_row = [b2_t[b][:, M_HEADS + h:M_HEADS + h + 1] + g_col[u] for u, (b, h) in enumerate(units)]
    floor = [jnp.maximum(jnp.abs(den[u]), jnp.exp2(-m2_row[u])) for u in idx]

    wk_row = [w[u][lt - 1:lt, :] for u in idx]
    c_upd = [_dot((kt[u].astype(F32) * wk_row[u]).astype(BF16), v[u]) for u in idx]
    n_upd = [_dot_nt(jnp.broadcast_to(wk_row[u], (8, lt)).astype(BF16), kt[u]) for u in idx]
    for u, (b, h) in enumerate(units):
        decay = w_int[u][lt - 1:lt]
        c_scr[M_HEADS * b + h] = decay * c_old[u] + c_upd[u]
        n_scr[8 * b + h:8 * b + h + 1, :] = decay * n_old[u] + n_upd[u][0:1]
        m_scr[8 * b + h:8 * b + h + 1, :] = jnp.broadcast_to(
            m2_row[u][lt - 1:lt] * (1.0 / LOG2E), (1, m_scr.shape[1]))

    o_pre = _dot_nt(xn_ref[...].reshape(nseq * lt, D_MODEL), wo_ref[0])
    for u, (b, h) in enumerate(units):
        sl = slice(h * M_DV, (h + 1) * M_DV)
        h_ref[b, :, sl] = _mlstm_finish(num[u], floor[u], o_pre[b * lt:(b + 1) * lt, sl], nw_ref[:, sl])

    @pl.when(c == pl.num_programs(0) - 1)
    def _():
        c_out[...] = c_scr[...].reshape(c_out.shape)
        for b in range(nseq):
            n_out[b] = n_scr[8 * b:8 * b + M_HEADS]
        m_out[...] = m_scr[...].reshape(m_out.shape)


def _mlstm_prompt(xn, w_all, mv, mq, kt, gt, nw, layer, batch, seq):
    lt = _pick(seq, (MLSTM_CHUNK, 128))
    nc = seq // lt
    tok = lambda width: pl.BlockSpec((batch, lt, width), lambda c: (0, c, 0))
    const = lambda shape: pl.BlockSpec(shape, lambda c: (0,) * len(shape))
    lanes = lambda rows: [pl.BlockSpec((rows, lt), lambda c, b=b: (0, b * nc + c)) for b in range(batch)]
    per_seq = lambda a: a.reshape(batch, seq, a.shape[-1])
    w_mo = pl.BlockSpec((1, D_MODEL, D_MODEL), lambda c: (layer, ROW_MO // D_MODEL, 0))
    ba, pc, pn, pm = pl.pallas_call(
        functools.partial(_mlstm_prompt_kernel, nseq=batch),
        grid=(nc,),
        in_specs=[tok(D_MODEL), w_mo, tok(MQ_W), tok(D_MODEL), const((1, D_MODEL))]
        + lanes(MQ_W) + lanes(GATE_ROWS),
        out_specs=[
            tok(D_MODEL),
            const((batch, M_HEADS, M_DQK, M_DV)),
            const((batch, M_HEADS, M_DQK)),
            const((batch, 8, 128)),
        ],
        out_shape=[
            jax.ShapeDtypeStruct((batch, seq, D_MODEL), F32),
            jax.ShapeDtypeStruct((batch, M_HEADS, M_DQK, M_DV), F32),
            jax.ShapeDtypeStruct((batch, M_HEADS, M_DQK), F32),
            jax.ShapeDtypeStruct((batch, 8, 128), F32),
        ],
        scratch_shapes=[pltpu.VMEM((batch * M_HEADS, M_DQK, M_DV), F32), pltpu.VMEM((batch * 8, M_DQK), F32),
                        pltpu.VMEM((batch * 8, 128), F32)],
        compiler_params=_params("arbitrary"),
        name="mlstm_prompt",
    )(per_seq(xn), w_all, per_seq(mq), per_seq(mv), nw, *([kt] * batch), *([gt] * batch))
    return ba.reshape(batch * seq, D_MODEL), pc, pn, pm


def _mlstm_sample_kernel(*refs, seg):
    xn_ref, wo_ref, q_ref, kt_ref, v_ref, gt_ref, nw_ref, m_ref, c_in, n_in = refs[:10]
    h_ref, c_out, n_out, m_out = refs[-4:]
    hd = pl.program_id(1)
    lt = q_ref.shape[0]
    nseg = lt // seg
    gt = gt_ref[...]
    lf = _log_sigmoid(gt)
    bcs = _seg_cumsum(lf, seg)
    row = lax.broadcasted_iota(jnp.int32, (GATE_ROWS, lt), 0)

    def pick(x, r):
        return jnp.sum(jnp.where(row == r, x, 0.0), axis=0, keepdims=True)

    i_row, lf_row, b_row = pick(gt, hd), pick(lf, M_HEADS + hd), pick(bcs, M_HEADS + hd)
    t_idx = lax.broadcasted_iota(jnp.int32, (lt, lt), 0)
    s_idx = lax.broadcasted_iota(jnp.int32, (lt, lt), 1)
    causal = (s_idx <= t_idx) & ((s_idx | (seg - 1)) == (t_idx | (seg - 1)))
    last = t_idx == (s_idx | (seg - 1))

    q = q_ref[...]
    kt = kt_ref[...]
    v = v_ref[...]
    m_col = m_ref[0, 0]
    n_old = [n_in[0, i, pl.ds(hd, 1), :] for i in range(nseg)]
    n_tok = jnp.concatenate([jnp.broadcast_to(n, (seg, M_DQK)) for n in n_old], axis=0)

    a_mat = jnp.where(causal, i_row - b_row, -jnp.inf)
    g_col = jnp.maximum(jnp.max(a_mat, axis=1, keepdims=True), m_col)
    b_col = jnp.sum(jnp.where(causal, lf_row, 0.0), axis=1, keepdims=True)
    w = jnp.exp(a_mat - g_col)
    w_int = jnp.exp(m_col - g_col)
    s = _dot(q, kt) * w
    inter = jnp.concatenate(
        [_dot(q[i * seg:(i + 1) * seg], c_in[0, i, 0].astype(BF16)) for i in range(nseg)], axis=0)
    qn = jnp.sum(q.astype(F32) * n_tok, axis=1, keepdims=True)
    num = w_int * inter + _dot(s.astype(BF16), v)
    den = w_int * qn + jnp.sum(s, axis=1, keepdims=True)
    m_row = b_col + g_col
    floor = jnp.maximum(jnp.abs(den), jnp.exp(-m_row))

    wk_row = jnp.sum(jnp.where(last, w, 0.0), axis=0, keepdims=True)
    ktw = kt.astype(F32) * wk_row
    lane_seg = lax.broadcasted_iota(jnp.int32, (1, lt), 1) | (seg - 1)
    lhs = jnp.concatenate(
        [jnp.where(lane_seg == i * seg + seg - 1, ktw, 0.0).astype(BF16) for i in range(nseg)], axis=0)
    c_upd = _dot(lhs, v)
    seg_row = lax.broadcasted_iota(jnp.int32, (lt, 1), 0) * seg + (seg - 1)
    n_upd = _dot_nt(jnp.where(lane_seg == seg_row, wk_row, 0.0).astype(BF16), kt)
    new_m = []
    for i in range(nseg):
        r = i * seg + seg - 1
        decay = w_int[r:r + 1]
        c_out[0, i, 0] = decay * c_in[0, i, 0] + c_upd[i * M_DQK:(i + 1) * M_DQK]
        n_out[0, i, pl.ds(hd, 1), :] = decay * n_old[i] + n_upd[i:i + 1]
        new_m.append(m_row[r:r + 1])
    m_out[0, 0] = jnp.concatenate(new_m, axis=0)
    h_ref[...] = _mlstm_finish(num, floor, _dot_nt(xn_ref[...], wo_ref[0]), nw_ref[...])


def _mlstm_sample(xn, w_all, mv, mq, kt, gt, nw, m_tok, state_c, state_n, carried, layer, dec_batch, seg):
    lt = SAMPLE_TILE
    nseg = lt // seg
    depth = state_c.shape[0]
    any_spec = pl.BlockSpec(memory_space=pl.ANY)
    c_spec = pl.BlockSpec((1, nseg, 1, M_DQK, M_DV), lambda t, h: (layer, t, h, 0, 0))
    n_spec = pl.BlockSpec((1, nseg, M_HEADS, M_DQK), lambda t, h: (layer, t, 0, 0))
    operands = [xn, w_all, mq, kt, mv, gt, nw, m_tok, state_c, state_n]
    in_specs = [
        pl.BlockSpec((lt, D_MODEL), lambda t, h: (t, 0)),
        pl.BlockSpec((1, M_DV, D_MODEL), lambda t, h: (layer, ROW_MO // M_DV + h, 0)),
        pl.BlockSpec((lt, M_DQK), lambda t, h: (t, h)),
        pl.BlockSpec((M_DQK, lt), lambda t, h: (h, t)),
        pl.BlockSpec((lt, M_DV), lambda t, h: (t, h)),
        pl.BlockSpec((GATE_ROWS, lt), lambda t, h: (0, t)),
        pl.BlockSpec((1, M_DV), lambda t, h: (0, h)),
        pl.BlockSpec((1, 1, lt, 1), lambda t, h: (layer, h, t, 0)),
        c_spec, n_spec,
    ]
    aliases = {}
    if carried is not None:
        aliases = {len(operands) + k: 1 + k for k in range(3)}
        operands += list(carried)
        in_specs += [any_spec] * 3
    return pl.pallas_call(
        functools.partial(_mlstm_sample_kernel, seg=seg),
        grid=(dec_batch // nseg, M_HEADS),
        in_specs=in_specs,
        out_specs=[
            pl.BlockSpec((lt, M_DV), lambda t, h: (t, h)),
            c_spec, n_spec,
            pl.BlockSpec((1, 1, nseg, 1), lambda t, h: (layer, h, t, 0)),
        ],
        out_shape=[
            jax.ShapeDtypeStruct((dec_batch * seg, D_MODEL), F32),
            jax.ShapeDtypeStruct(state_c.shape, F32),
            jax.ShapeDtypeStruct(state_n.shape, F32),
            jax.ShapeDtypeStruct((depth, M_HEADS, dec_batch, 1), F32),
        ],
        input_output_aliases=aliases,
        compiler_params=_params("parallel", "arbitrary"),
        name="mlstm_sample",
    )(*operands)


def _swa_merge_prompt_kernel(sink_ref, bias_ref, qt_ref, kp_ref, kc_ref, vp_ref, vc_ref,
                             x_ref, xn_ref, ba_ref, wg_ref, w_ref,
                             o_ref, pk_ref, pv_ref, kb_scr, vt_scr, bb_scr):
    i = pl.program_id(1)
    nsub = qt_ref.shape[1] // WINDOW
    lanes = A_GROUP * WINDOW
    all_lanes = A_HEADS * WINDOW
    gates = _dot_nt(xn_ref[...], wg_ref[0])
    gated_a = _sigmoid(gates[:, :D_MODEL]) * ba_ref[...]
    gate_b = _sigmoid(gates[:, D_MODEL:])

    kb_scr[0:WINDOW] = kp_ref[...].astype(BF16)
    kb_scr[WINDOW:] = kc_ref[...].astype(BF16)
    vt_scr[:, 0:WINDOW] = vp_ref[...].T.astype(BF16)
    vt_scr[:, WINDOW:] = vc_ref[...].T.astype(BF16)

    lane_head = lax.broadcasted_iota(jnp.int32, (1, all_lanes), 1) >> (WINDOW.bit_length() - 1)
    sink = jnp.zeros((1, all_lanes), F32)
    for h in range(A_HEADS):
        sink = jnp.where(lane_head == h, sink_ref[h] * LOG2E, sink)
    has_prev = (lax.broadcasted_iota(jnp.int32, (2 * WINDOW, 1), 0) >= WINDOW) | (i > 0)
    zeros = jnp.zeros((A_HD, lanes), BF16)
    ones = jnp.ones((16, 2 * WINDOW), BF16)

    for j in range(nsub):
        qt = qt_ref[:, j * WINDOW:(j + 1) * WINDOW]
        qt4 = [jnp.concatenate([qt[(kv * A_GROUP + g) * A_HD:(kv * A_GROUP + g + 1) * A_HD]
                                for g in range(A_GROUP)], axis=1) for kv in range(A_KV)]
        st = []
        for pair in range(A_KV // 2):
            rhs = jnp.concatenate([jnp.concatenate([qt4[2 * pair], zeros], axis=1),
                                   jnp.concatenate([zeros, qt4[2 * pair + 1]], axis=1)], axis=0)
            kpair = kb_scr[j * WINDOW:(j + 2) * WINDOW, pair * 128:(pair + 1) * 128]
            st.append(_dot(kpair, rhs))
        st = jnp.concatenate(st, axis=1) + bias_ref[...]
        if j == 0:
            st = jnp.where(has_prev, st, -jnp.inf)
        mx = jnp.maximum(jnp.max(st, axis=0, keepdims=True), sink)
        p = jnp.exp2(st - mx).astype(BF16)
        sink_p = jnp.exp2(sink - mx)
        pieces = []
        for kv in range(A_KV):
            vt = jnp.concatenate([vt_scr[kv * A_HD:(kv + 1) * A_HD, j * WINDOW:(j + 2) * WINDOW], ones], axis=0)
            ot = _dot(vt, p[:, kv * lanes:(kv + 1) * lanes])
            den = ot[A_HD:A_HD + 1] + sink_p[:, kv * lanes:(kv + 1) * lanes]
            ot = ot[:A_HD] * (1.0 / den)
            pieces.extend(ot[:, g * WINDOW:(g + 1) * WINDOW] for g in range(A_GROUP))
        bb_scr[j * WINDOW:(j + 1) * WINDOW, :] = jnp.concatenate(pieces, axis=0).T

    merged = gated_a + gate_b * bb_scr[...]
    o_ref[...] = x_ref[...] + _dot(merged.astype(BF16), w_ref[0])

    @pl.when(i == pl.num_programs(1) - 1)
    def _():
        rows = kc_ref.shape[0]
        pk_ref[0] = kc_ref[rows - WINDOW:rows, :].T
        pv_ref[0] = vc_ref[rows - WINDOW:rows, :].T


def _band_bias():
    c = jnp.arange(2 * WINDOW, dtype=jnp.int32)[:, None]
    t = jnp.arange(A_HEADS * WINDOW, dtype=jnp.int32)[None, :] % WINDOW
    return jnp.where((c > t) & (c <= t + WINDOW), 0.0, -jnp.inf).astype(F32)


def _swa_merge_prompt(aq_t, pf, sinks, bias, x, xn, ba, w_gates, w_out, layer, batch, seq):
    tq = _pick(seq, (SWA_STEP, 256, 128))
    ns = seq // tq
    per = tq // WINDOW
    nb = seq // WINDOW
    cur = lambda col: (lambda b, i: (b * ns + i, col))
    prev = lambda col: (lambda b, i: (b * nb + jnp.maximum(i * per - 1, 0), col))
    const = lambda b, i: (0, 0)
    row = pl.BlockSpec((tq, D_MODEL), cur(0))
    return pl.pallas_call(
        _swa_merge_prompt_kernel,
        grid=(batch, ns),
        in_specs=[
            pl.BlockSpec(memory_space=pltpu.SMEM),
            pl.BlockSpec(bias.shape, const),
            pl.BlockSpec((D_MODEL, tq), lambda b, i: (0, b * ns + i)),
            pl.BlockSpec((WINDOW, KV_W), prev(0)),
            pl.BlockSpec((tq, KV_W), cur(0)),
            pl.BlockSpec((WINDOW, KV_W), prev(1)),
            pl.BlockSpec((tq, KV_W), cur(1)),
            row, row, row,
            pl.BlockSpec((1, 2 * D_MODEL, D_MODEL), lambda b, i: (layer, 0, 0)),
            pl.BlockSpec((1, D_MODEL, D_MODEL), lambda b, i: (layer, 0, 0)),
        ],
        out_specs=[
            pl.BlockSpec((tq, D_MODEL), lambda b, i: (b * ns + i, 0)),
            pl.BlockSpec((1, KV_W, WINDOW), lambda b, i: (b, 0, 0)),
            pl.BlockSpec((1, KV_W, WINDOW), lambda b, i: (b, 0, 0)),
        ],
        out_shape=[
            jax.ShapeDtypeStruct((batch * seq, D_MODEL), F32),
            jax.ShapeDtypeStruct((batch, KV_W, WINDOW), F32),
            jax.ShapeDtypeStruct((batch, KV_W, WINDOW), F32),
        ],
        scratch_shapes=[pltpu.VMEM((WINDOW + tq, KV_W), BF16), pltpu.VMEM((KV_W, WINDOW + tq), BF16),
                        pltpu.VMEM((tq, D_MODEL), F32)],
        compiler_params=_params("parallel", "arbitrary"),
        name="swa_merge_prompt",
    )(sinks, bias, aq_t, pf, pf, pf, pf, x, xn, ba, w_gates, w_out)


def _bdot(a, b, contract_b):
    return lax.dot_general(a, b, (((2,), (contract_b,)), ((0,), (0,))), preferred_element_type=F32)


def _swa_sample_kernel(*refs, tq):
    sink_ref, q_ref, kn_ref, vn_ref, ck_ref, cv_ref = refs[:6]
    o_ref, sk_ref, sv_ref = refs[-3:]
    nb = ck_ref.shape[1]
    wc = ck_ref.shape[3]
    keep = wc - tq
    q3 = q_ref[...].astype(F32).reshape(nb, tq, D_MODEL)
    ck = ck_ref[0]
    cv = cv_ref[0]

    lane = lax.broadcasted_iota(jnp.int32, (nb, KV_W, wc), 2)

    def appended(cache, new_rows):
        new_t = new_rows.T
        placed = jnp.stack([pltpu.roll(new_t, (keep - b * tq) % wc, axis=1) for b in range(nb)])
        return jnp.where(lane >= keep, placed, pltpu.roll(cache, keep, axis=2))

    sk = appended(ck, kn_ref[...])
    sv = appended(cv, vn_ref[...])
    sk_ref[0] = sk
    sv_ref[0] = sv

    rows = A_GROUP * tq
    t_idx = lax.broadcasted_iota(jnp.int32, (1, rows, 1), 1) & (tq - 1)
    c_idx = lax.broadcasted_iota(jnp.int32, (1, 1, wc), 2)
    mask_old = c_idx > t_idx
    mask_new = (c_idx >= keep) & (c_idx - keep <= t_idx)
    g_idx = lax.broadcasted_iota(jnp.int32, (1, rows, 1), 1) >> (tq.bit_length() - 1)
    pieces = []
    for kv in range(A_KV):
        heads = [kv * A_GROUP + g for g in range(A_GROUP)]
        q4 = jnp.concatenate([q3[:, :, h * A_HD:(h + 1) * A_HD] for h in heads], axis=1).astype(BF16)
        sink = jnp.zeros((1, rows, 1), F32)
        for g in range(A_GROUP):
            sink = jnp.where(g_idx == g, sink_ref[heads[g]] * LOG2E, sink)
        sl = slice(kv * A_HD, (kv + 1) * A_HD)
        s1 = jnp.where(mask_old, _bdot(q4, ck[:, sl, :].astype(BF16), 1), -jnp.inf)
        s2 = jnp.where(mask_new, _bdot(q4, sk[:, sl, :].astype(BF16), 1), -jnp.inf)
        mx = jnp.maximum(jnp.maximum(jnp.max(s1, axis=2, keepdims=True),
                                     jnp.max(s2, axis=2, keepdims=True)), sink)
        p1 = jnp.exp2(s1 - mx)
        p2 = jnp.exp2(s2 - mx)
        den = (jnp.sum(p1, axis=2, keepdims=True) + jnp.sum(p2, axis=2, keepdims=True)
               + jnp.exp2(sink - mx))
        r = 1.0 / den
        o = (_bdot((p1 * r).astype(BF16), cv[:, sl, :].astype(BF16), 2)
             + _bdot((p2 * r).astype(BF16), sv[:, sl, :].astype(BF16), 2))
        pieces.extend(o[:, g * tq:(g + 1) * tq, :] for g in range(A_GROUP))
    o_ref[...] = jnp.concatenate(pieces, axis=2).reshape(nb * tq, D_MODEL)


def _swa_sample(aq, pf, sinks, cache_kt, cache_vt, carried, layer, dec_batch, tq):
    lt = SAMPLE_TILE
    nb = lt // tq
    wc = cache_kt.shape[3]
    assert wc == lt, "the appended keys are placed with lane rolls over one cache row"
    any_spec = pl.BlockSpec(memory_space=pl.ANY)
    cache_spec = pl.BlockSpec((1, nb, KV_W, wc), lambda i: (layer, i, 0, 0))
    operands = [sinks, aq, pf, pf, cache_kt, cache_vt]
    in_specs = [
        pl.BlockSpec(memory_space=pltpu.SMEM),
        pl.BlockSpec((lt, D_MODEL), lambda i: (i, 0)),
        pl.BlockSpec((lt, KV_W), lambda i: (i, 0)),
        pl.BlockSpec((lt, KV_W), lambda i: (i, 1)),
        cache_spec, cache_spec,
    ]
    aliases = {}
    if carried is not None:
        aliases = {len(operands) + k: 1 + k for k in range(2)}
        operands += list(carried)
        in_specs += [any_spec] * 2
    return pl.pallas_call(
        functools.partial(_swa_sample_kernel, tq=tq),
        grid=(dec_batch // nb,),
        in_specs=in_specs,
        out_specs=[
            pl.BlockSpec((lt, D_MODEL), lambda i: (i, 0)),
            cache_spec, cache_spec,
        ],
        out_shape=[
            jax.ShapeDtypeStruct((dec_batch * tq, D_MODEL), F32),
            jax.ShapeDtypeStruct(cache_kt.shape, F32),
            jax.ShapeDtypeStruct(cache_vt.shape, F32),
        ],
        input_output_aliases=aliases,
        compiler_params=_params("parallel"),
        name="swa_sample",
    )(*operands)


def _merge_kernel(x_ref, xn_ref, wg_ref, ba_ref, bb_ref, w_ref, o_ref):
    gates = _dot_nt(xn_ref[...], wg_ref[0])
    merged = _sigmoid(gates[:, :D_MODEL]) * ba_ref[...] + _sigmoid(gates[:, D_MODEL:]) * bb_ref[...]
    o_ref[...] = x_ref[...] + _dot(merged.astype(BF16), w_ref[0])


def _merge(x, xn, w_gates, ba, bb, w_out, layer):
    n = x.shape[0]
    tm = _pick(n, (512, 256, 128))
    row = pl.BlockSpec((tm, D_MODEL), lambda i: (i, 0))
    weight = lambda rows: pl.BlockSpec((1, rows, D_MODEL), lambda i: (layer, 0, 0))
    return pl.pallas_call(
        _merge_kernel,
        grid=(n // tm,),
        in_specs=[row, row, weight(2 * D_MODEL), row, row, weight(D_MODEL)],
        out_specs=row,
        out_shape=jax.ShapeDtypeStruct((n, D_MODEL), F32),
        compiler_params=_params("parallel"),
        name="merge_out_proj",
    )(x, xn, w_gates, ba, bb, w_out)


def _mlp_kernel(x_ref, g_ref, wu_ref, wd_ref, gf_ref, o_ref, xn_ref, *, final_norm):
    j = pl.program_id(1)

    @pl.when(j == 0)
    def _():
        x = x_ref[...]
        xn_ref[...] = _rms(x, g_ref[...]).astype(BF16)
        o_ref[...] = x

    h = jnp.square(jnp.maximum(_dot(xn_ref[...], wu_ref[0]), 0.0))
    o_ref[...] += _dot(h.astype(BF16), wd_ref[0])

    if final_norm:
        @pl.when(j == pl.num_programs(1) - 1)
        def _():
            o_ref[...] = _rms(o_ref[...], gf_ref[...])


def _mlp(x, g, w_up, w_down, g_final, layer, final_norm):
    n = x.shape[0]
    tm = _pick(n, (1024, 512, 256, 128))
    tf = 2048
    return pl.pallas_call(
        functools.partial(_mlp_kernel, final_norm=final_norm),
        grid=(n // tm, D_FF // tf),
        in_specs=[
            pl.BlockSpec((tm, D_MODEL), lambda i, j: (i, 0)),
            pl.BlockSpec((1, D_MODEL), lambda i, j: (0, 0)),
            pl.BlockSpec((1, D_MODEL, tf), lambda i, j: (layer, 0, j)),
            pl.BlockSpec((1, tf, D_MODEL), lambda i, j: (layer, j, 0)),
            pl.BlockSpec((1, D_MODEL), lambda i, j: (0, 0)),
        ],
        out_specs=pl.BlockSpec((tm, D_MODEL), lambda i, j: (i, 0)),
        out_shape=jax.ShapeDtypeStruct((n, D_MODEL), F32),
        scratch_shapes=[pltpu.VMEM((tm, D_MODEL), BF16)],
        compiler_params=_params("parallel", "arbitrary"),
        name="mlp",
    )(x, g, w_up, w_down, g_final)


def _prep_in_proj_weights(w_in):
    assert w_in.shape[2] == D_IN
    w_all = jnp.swapaxes(w_in, 1, 2).astype(BF16)
    return w_all, w_all[:, ROW_AQ:ROW_MERGE], w_all[:, ROW_MERGE:D_IN]


def kernel(x_prompt, x_sample, state_C, state_n, state_m, cache_k, cache_v, norm_attn, w_in, b_gate,
           mlstm_norm, sinks, w_out, norm_mlp, w_up, w_down, norm_final):
    batch, seq, _ = x_prompt.shape
    dec_batch, dec_seq, _ = x_sample.shape
    depth = w_in.shape[0]
    wc = cache_k.shape[2]

    w_all, w_att, w_merge = _prep_in_proj_weights(w_in)
    b_gate_col = jnp.zeros((depth, GATE_ROWS, 1), F32).at[:, :2 * M_HEADS, 0].set(b_gate.astype(F32))
    w_out_b = w_out.astype(BF16)
    w_up_b = w_up.astype(BF16)
    w_down_b = w_down.astype(BF16)
    bias = _band_bias()

    xp = x_prompt.reshape(batch * seq, D_MODEL)
    xs = x_sample.reshape(dec_batch * dec_seq, D_MODEL)
    m_tok = jnp.repeat(jnp.swapaxes(state_m, 1, 2), dec_seq, axis=2)[..., None]
    cache_kt = jnp.transpose(cache_k, (0, 1, 3, 4, 2)).reshape(depth, dec_batch, KV_W, wc)
    cache_vt = jnp.transpose(cache_v, (0, 1, 3, 4, 2)).reshape(depth, dec_batch, KV_W, wc)

    states = None
    caches = None
    p_c, p_n, p_m, p_k, p_v = [], [], [], [], []
    g_final = norm_final[None]

    for l in range(depth):
        g_attn = norm_attn[l][None]
        nw = mlstm_norm[l][None]
        g_mlp = norm_mlp[l][None]
        last = l == depth - 1
        proj_w = (w_all, w_att, b_gate_col[l], l)

        xn, mv, mq, pf, kt, gt, aq_t = _in_proj(xp, g_attn, *proj_w, q_transposed=True)
        ba, pc, pn, pm = _mlstm_prompt(xn, w_all, mv, mq, kt, gt, nw, l, batch, seq)
        xp, pk, pv = _swa_merge_prompt(aq_t, pf, sinks[l], bias, xp, xn, ba, w_merge, w_out_b, l, batch, seq)
        xp = _mlp(xp, g_mlp, w_up_b, w_down_b, g_final, l, last)
        p_c.append(pc)
        p_n.append(pn)
        p_m.append(pm[:, :M_HEADS, 0])
        p_k.append(pk)
        p_v.append(pv)

        xn, mv, mq, pf, kt, gt, aq = _in_proj(xs, g_attn, *proj_w, q_transposed=False)
        ba, *states = _mlstm_sample(xn, w_all, mv, mq, kt, gt, nw, m_tok, state_C, state_n, states,
                                    l, dec_batch, dec_seq)
        bb, *caches = _swa_sample(aq, pf, sinks[l], cache_kt, cache_vt, caches, l, dec_batch, dec_seq)
        xs = _merge(xs, xn, w_merge, ba, bb, w_out_b, l)
        xs = _mlp(xs, g_mlp, w_up_b, w_down_b, g_final, l, last)

    s_c, s_n, s_m = states
    s_k, s_v = caches

    def positions_major(t, lead):
        t = t.reshape(*lead, A_KV, A_HD, wc)
        return jnp.moveaxis(t, -1, -3)

    return (xp.reshape(batch, seq, D_MODEL), xs.reshape(dec_batch, dec_seq, D_MODEL),
            jnp.stack(p_c), jnp.stack(p_n), jnp.stack(p_m),
            positions_major(jnp.stack(p_k), (depth, batch)), positions_major(jnp.stack(p_v), (depth, batch)),
            s_c, s_n, jnp.swapaxes(s_m[..., 0], 1, 2),
            positions_major(s_k, (depth, dec_batch)), positions_major(s_v, (depth, dec_batch)))
```

```python
import functools

import jax
import jax.numpy as jnp
from jax import lax
from jax.experimental import pallas as pl
from jax.experimental.pallas import tpu as pltpu

F32 = jnp.float32
BF16 = jnp.bfloat16

D_MODEL = 1024
M_HEADS = 4
M_DQK = 128
M_DV = D_MODEL // M_HEADS
MQ_W = M_HEADS * M_DQK
A_HEADS = 16
A_KV = 4
A_GROUP = A_HEADS // A_KV
A_HD = D_MODEL // A_HEADS
KV_W = A_KV * A_HD
WINDOW = 128
D_FF = 4 * D_MODEL
EPS = 1e-6
LOG2E = 1.4426950408889634

ROW_MQ = 0
ROW_MK = ROW_MQ + MQ_W
ROW_MV = ROW_MK + MQ_W
ROW_MO = ROW_MV + D_MODEL
ROW_GATES = ROW_MO + D_MODEL
ROW_AQ = ROW_GATES + 2 * M_HEADS
ROW_MERGE = ROW_AQ + D_MODEL + 2 * KV_W
D_IN = ROW_MERGE + 2 * D_MODEL
GATE_ROWS = 16

LANES = 128
SUBLANES = 8
BF16_SUBLANES = 16
VMEM_BYTES = 64 * 1024 * 1024
VMEM_LIMIT = VMEM_BYTES * 7 // 8

MLSTM_CHUNK = 256
SAMPLE_TILE = 128
SWA_STEP = 512


def _pick(n, candidates):
    for c in candidates:
        if n % c == 0:
            return c
    raise ValueError(f"no block size for {n}")


def _sigmoid(x):
    return 0.5 * jnp.tanh(0.5 * x) + 0.5


def _log_sigmoid(x):
    return jnp.minimum(x, 0.0) - jnp.log1p(jnp.exp(-jnp.abs(x)))


def _rms(x, g):
    y = x * lax.rsqrt(jnp.mean(x * x, axis=-1, keepdims=True) + EPS)
    return y * g


def _dot(a, b):
    return jnp.dot(a, b, preferred_element_type=F32)


def _dot_nt(a, b):
    return lax.dot_general(a, b, (((1,), (1,)), ((), ())), preferred_element_type=F32)


def _params(*sem):
    return pltpu.CompilerParams(dimension_semantics=sem, vmem_limit_bytes=VMEM_LIMIT)


def _in_proj_kernel(x_ref, g_ref, wmq_ref, wmk_ref, wmv_ref, wg_ref, waq_ref, wkv_ref, bg_ref,
                    xn_out, mv_ref, mq_ref, pf_ref, kt_ref, gt_ref, aq_ref, *, q_transposed):
    xn = _rms(x_ref[...], g_ref[...]).astype(BF16)
    xn_out[...] = xn
    kt_ref[...] = _dot_nt(wmk_ref[0], xn).astype(BF16)
    gt_ref[...] = _dot_nt(wg_ref[0], xn) + bg_ref[...]
    mv_ref[...] = _dot_nt(xn, wmv_ref[0]).astype(BF16)
    if q_transposed:
        aq_ref[...] = (_dot_nt(waq_ref[0], xn) * (A_HD ** -0.5 * LOG2E)).astype(BF16)
    else:
        aq_ref[...] = (_dot_nt(xn, waq_ref[0]) * (A_HD ** -0.5 * LOG2E)).astype(BF16)
    mq_ref[...] = (_dot_nt(xn, wmq_ref[0]) * (M_DQK ** -0.5)).astype(BF16)
    pf_ref[...] = _dot_nt(xn, wkv_ref[0])


def _in_proj(x, g, w_all, w_att, bg, layer, q_transposed):
    n = x.shape[0]
    tm = _pick(n, (1024, 512, 256, 128))
    const = lambda i: (0, 0)
    row = lambda i: (i, 0)
    col = lambda i: (0, i)
    wblk = lambda rows, off: pl.BlockSpec((1, rows, D_MODEL), lambda i: (layer, off // rows, 0))
    if q_transposed:
        aq_spec, aq_shape = pl.BlockSpec((D_MODEL, tm), col), (D_MODEL, n)
    else:
        aq_spec, aq_shape = pl.BlockSpec((tm, D_MODEL), row), (n, D_MODEL)
    return pl.pallas_call(
        functools.partial(_in_proj_kernel, q_transposed=q_transposed),
        grid=(n // tm,),
        in_specs=[
            pl.BlockSpec((tm, D_MODEL), row),
            pl.BlockSpec((1, D_MODEL), const),
            wblk(MQ_W, ROW_MQ), wblk(MQ_W, ROW_MK), wblk(D_MODEL, ROW_MV), wblk(GATE_ROWS, ROW_GATES),
            wblk(D_MODEL, 0), wblk(2 * KV_W, D_MODEL),
            pl.BlockSpec((GATE_ROWS, 1), const),
        ],
        out_specs=[
            pl.BlockSpec((tm, D_MODEL), row),
            pl.BlockSpec((tm, D_MODEL), row),
            pl.BlockSpec((tm, MQ_W), row),
            pl.BlockSpec((tm, 2 * KV_W), row),
            pl.BlockSpec((MQ_W, tm), col),
            pl.BlockSpec((GATE_ROWS, tm), col),
            aq_spec,
        ],
        out_shape=[
            jax.ShapeDtypeStruct((n, D_MODEL), BF16),
            jax.ShapeDtypeStruct((n, D_MODEL), BF16),
            jax.ShapeDtypeStruct((n, MQ_W), BF16),
            jax.ShapeDtypeStruct((n, 2 * KV_W), F32),
            jax.ShapeDtypeStruct((MQ_W, n), BF16),
            jax.ShapeDtypeStruct((GATE_ROWS, n), F32),
            jax.ShapeDtypeStruct(aq_shape, BF16),
        ],
        compiler_params=_params("parallel"),
        name="in_proj",
    )(x, g, w_all, w_all, w_all, w_all, w_att, w_att, bg)


def _seg_cumsum(x, seg):
    pos = lax.broadcasted_iota(jnp.int32, x.shape, 1) & (seg - 1)
    sh = 1
    while sh < seg:
        x = x + jnp.where(pos >= sh, pltpu.roll(x, sh, axis=1), 0.0)
        sh *= 2
    return x


def _mlstm_finish(num, d, o_pre, nw):
    msq = jnp.mean(num * num, axis=-1, keepdims=True)
    return _sigmoid(o_pre) * (num * lax.rsqrt(msq + EPS * d * d)) * nw


def _mlstm_prompt_kernel(*refs, nseq):
    xn_ref, wo_ref, q_ref, v_ref, nw_ref = refs[:5]
    kt_refs = refs[5:5 + nseq]
    gt_refs = refs[5 + nseq:5 + 2 * nseq]
    h_ref, c_out, n_out, m_out, c_scr, n_scr, m_scr = refs[5 + 2 * nseq:]
    c = pl.program_id(0)
    lt = q_ref.shape[1]
    units = [(b, h) for b in range(nseq) for h in range(M_HEADS)]
    idx = range(len(units))

    @pl.when(c == 0)
    def _():
        c_scr[...] = jnp.zeros_like(c_scr)
        n_scr[...] = jnp.zeros_like(n_scr)
        m_scr[...] = jnp.zeros_like(m_scr)

    i2, b2, b2_t = [], [], []
    for b in range(nseq):
        gt = gt_refs[b][...]
        i2.append(gt * LOG2E)
        b2.append(_seg_cumsum(_log_sigmoid(gt), lt) * LOG2E)
        b2_t.append(jnp.concatenate([b2[b], jnp.zeros((LANES - GATE_ROWS, lt), F32)], axis=0).T)
    t_idx = lax.broadcasted_iota(jnp.int32, (lt, lt), 0)
    s_idx = lax.broadcasted_iota(jnp.int32, (lt, lt), 1)
    causal = s_idx <= t_idx

    q = [q_ref[b, :, h * M_DQK:(h + 1) * M_DQK] for b, h in units]
    kt = [kt_refs[b][h * M_DQK:(h + 1) * M_DQK, :] for b, h in units]
    v = [v_ref[b, :, h * M_DV:(h + 1) * M_DV] for b, h in units]
    m2_col = [jnp.broadcast_to(m_scr[SUBLANES * b + h:SUBLANES * b + h + 1, 0:1] * LOG2E, (lt, 1)) for b, h in units]
    n_old = [n_scr[SUBLANES * b + h:SUBLANES * b + h + 1, :] for b, h in units]
    c_old = [c_scr[M_HEADS * b + h] for b, h in units]

    a_mat = [jnp.where(causal, i2[b][h:h + 1] - b2[b][M_HEADS + h:M_HEADS + h + 1], -jnp.inf)
             for b, h in units]
    g_col = [jnp.maximum(jnp.max(a_mat[u], axis=1, keepdims=True), m2_col[u]) for u in idx]
    w = [jnp.exp2(a_mat[u] - g_col[u]) for u in idx]
    w_int = [jnp.exp2(m2_col[u] - g_col[u]) for u in idx]
    s = [_dot(q[u], kt[u]) * w[u] for u in idx]
    inter = [_dot(q[u], c_old[u].astype(BF16)) for u in idx]
    qn = [_dot_nt(q[u], jnp.broadcast_to(n_old[u], (SUBLANES, M_DQK)).astype(BF16))[:, 0:1] for u in idx]
    num = [w_int[u] * inter[u] + _dot(s[u].astype(BF16), v[u]) for u in idx]
    den = [w_int[u] * qn[u] + jnp.sum(s[u], axis=1, keepdims=True) for u in idx]
    m2_row = [b2_t[b][:, M_HEADS + h:M_HEADS + h + 1] + g_col[u] for u, (b, h) in enumerate(units)]
    floor = [jnp.maximum(jnp.abs(den[u]), jnp.exp2(-m2_row[u])) for u in idx]

    wk_row = [w[u][lt - 1:lt, :] for u in idx]
    c_upd = [_dot((kt[u].astype(F32) * wk_row[u]).astype(BF16), v[u]) for u in idx]
    n_upd = [_dot_nt(jnp.broadcast_to(wk_row[u], (SUBLANES, lt)).astype(BF16), kt[u]) for u in idx]
    for u, (b, h) in enumerate(units):
        decay = w_int[u][lt - 1:lt]
        c_scr[M_HEADS * b + h] = decay * c_old[u] + c_upd[u]
        n_scr[SUBLANES * b + h:SUBLANES * b + h + 1, :] = decay * n_old[u] + n_upd[u][0:1]
        m_scr[SUBLANES * b + h:SUBLANES * b + h + 1, :] = jnp.broadcast_to(
            m2_row[u][lt - 1:lt] * (1.0 / LOG2E), (1, m_scr.shape[1]))

    o_pre = _dot_nt(xn_ref[...].reshape(nseq * lt, D_MODEL), wo_ref[0])
    for u, (b, h) in enumerate(units):
        sl = slice(h * M_DV, (h + 1) * M_DV)
        h_ref[b, :, sl] = _mlstm_finish(num[u], floor[u], o_pre[b * lt:(b + 1) * lt, sl], nw_ref[:, sl])

    @pl.when(c == pl.num_programs(0) - 1)
    def _():
        c_out[...] = c_scr[...].reshape(c_out.shape)
        for b in range(nseq):
            n_out[b] = n_scr[SUBLANES * b:SUBLANES * b + M_HEADS]
        m_out[...] = m_scr[...].reshape(m_out.shape)


def _mlstm_prompt(xn, w_all, mv, mq, kt, gt, nw, layer, batch, seq):
    lt = _pick(seq, (MLSTM_CHUNK, 128))
    nc = seq // lt
    tok = lambda width: pl.BlockSpec((batch, lt, width), lambda c: (0, c, 0))
    const = lambda shape: pl.BlockSpec(shape, lambda c: (0,) * len(shape))
    lanes = lambda rows: [pl.BlockSpec((rows, lt), lambda c, b=b: (0, b * nc + c)) for b in range(batch)]
    per_seq = lambda a: a.reshape(batch, seq, a.shape[-1])
    w_mo = pl.BlockSpec((1, D_MODEL, D_MODEL), lambda c: (layer, ROW_MO // D_MODEL, 0))
    ba, pc, pn, pm = pl.pallas_call(
        functools.partial(_mlstm_prompt_kernel, nseq=batch),
        grid=(nc,),
        in_specs=[tok(D_MODEL), w_mo, tok(MQ_W), tok(D_MODEL), const((1, D_MODEL))]
        + lanes(MQ_W) + lanes(GATE_ROWS),
        out_specs=[
            tok(D_MODEL),
            const((batch, M_HEADS, M_DQK, M_DV)),
            const((batch, M_HEADS, M_DQK)),
            const((batch, SUBLANES, LANES)),
        ],
        out_shape=[
            jax.ShapeDtypeStruct((batch, seq, D_MODEL), F32),
            jax.ShapeDtypeStruct((batch, M_HEADS, M_DQK, M_DV), F32),
            jax.ShapeDtypeStruct((batch, M_HEADS, M_DQK), F32),
            jax.ShapeDtypeStruct((batch, SUBLANES, LANES), F32),
        ],
        scratch_shapes=[pltpu.VMEM((batch * M_HEADS, M_DQK, M_DV), F32), pltpu.VMEM((batch * SUBLANES, M_DQK), F32),
                        pltpu.VMEM((batch * SUBLANES, LANES), F32)],
        compiler_params=_params("arbitrary"),
        name="mlstm_prompt",
    )(per_seq(xn), w_all, per_seq(mq), per_seq(mv), nw, *([kt] * batch), *([gt] * batch))
    return ba.reshape(batch * seq, D_MODEL), pc, pn, pm


def _mlstm_sample_kernel(*refs, seg):
    xn_ref, wo_ref, q_ref, kt_ref, v_ref, gt_ref, nw_ref, m_ref, c_in, n_in = refs[:10]
    h_ref, c_out, n_out, m_out = refs[-4:]
    hd = pl.program_id(1)
    lt = q_ref.shape[0]
    nseg = lt // seg
    gt = gt_ref[...]
    lf = _log_sigmoid(gt)
    bcs = _seg_cumsum(lf, seg)
    row = lax.broadcasted_iota(jnp.int32, (GATE_ROWS, lt), 0)

    def pick(x, r):
        return jnp.sum(jnp.where(row == r, x, 0.0), axis=0, keepdims=True)

    i_row, lf_row, b_row = pick(gt, hd), pick(lf, M_HEADS + hd), pick(bcs, M_HEADS + hd)
    t_idx = lax.broadcasted_iota(jnp.int32, (lt, lt), 0)
    s_idx = lax.broadcasted_iota(jnp.int32, (lt, lt), 1)
    causal = (s_idx <= t_idx) & ((s_idx | (seg - 1)) == (t_idx | (seg - 1)))
    last = t_idx == (s_idx | (seg - 1))

    q = q_ref[...]
    kt = kt_ref[...]
    v = v_ref[...]
    m_col = m_ref[0, 0]
    n_old = [n_in[0, i, pl.ds(hd, 1), :] for i in range(nseg)]
    n_tok = jnp.concatenate([jnp.broadcast_to(n, (seg, M_DQK)) for n in n_old], axis=0)

    a_mat = jnp.where(causal, i_row - b_row, -jnp.inf)
    g_col = jnp.maximum(jnp.max(a_mat, axis=1, keepdims=True), m_col)
    b_col = jnp.sum(jnp.where(causal, lf_row, 0.0), axis=1, keepdims=True)
    w = jnp.exp(a_mat - g_col)
    w_int = jnp.exp(m_col - g_col)
    s = _dot(q, kt) * w
    inter = jnp.concatenate(
        [_dot(q[i * seg:(i + 1) * seg], c_in[0, i, 0].astype(BF16)) for i in range(nseg)], axis=0)
    qn = jnp.sum(q.astype(F32) * n_tok, axis=1, keepdims=True)
    num = w_int * inter + _dot(s.astype(BF16), v)
    den = w_int * qn + jnp.sum(s, axis=1, keepdims=True)
    m_row = b_col + g_col
    floor = jnp.maximum(jnp.abs(den), jnp.exp(-m_row))

    wk_row = jnp.sum(jnp.where(last, w, 0.0), axis=0, keepdims=True)
    ktw = kt.astype(F32) * wk_row
    lane_seg = lax.broadcasted_iota(jnp.int32, (1, lt), 1) | (seg - 1)
    lhs = jnp.concatenate(
        [jnp.where(lane_seg == i * seg + seg - 1, ktw, 0.0).astype(BF16) for i in range(nseg)], axis=0)
    c_upd = _dot(lhs, v)
    seg_row = lax.broadcasted_iota(jnp.int32, (lt, 1), 0) * seg + (seg - 1)
    n_upd = _dot_nt(jnp.where(lane_seg == seg_row, wk_row, 0.0).astype(BF16), kt)
    new_m = []
    for i in range(nseg):
        r = i * seg + seg - 1
        decay = w_int[r:r + 1]
        c_out[0, i, 0] = decay * c_in[0, i, 0] + c_upd[i * M_DQK:(i + 1) * M_DQK]
        n_out[0, i, pl.ds(hd, 1), :] = decay * n_old[i] + n_upd[i:i + 1]
        new_m.append(m_row[r:r + 1])
    m_out[0, 0] = jnp.concatenate(new_m, axis=0)
    h_ref[...] = _mlstm_finish(num, floor, _dot_nt(xn_ref[...], wo_ref[0]), nw_ref[...])


def _mlstm_sample(xn, w_all, mv, mq, kt, gt, nw, m_tok, state_c, state_n, carried, layer, dec_batch, seg):
    lt = SAMPLE_TILE
    nseg = lt // seg
    depth = state_c.shape[0]
    any_spec = pl.BlockSpec(memory_space=pl.ANY)
    c_spec = pl.BlockSpec((1, nseg, 1, M_DQK, M_DV), lambda t, h: (layer, t, h, 0, 0))
    n_spec = pl.BlockSpec((1, nseg, M_HEADS, M_DQK), lambda t, h: (layer, t, 0, 0))
    operands = [xn, w_all, mq, kt, mv, gt, nw, m_tok, state_c, state_n]
    in_specs = [
        pl.BlockSpec((lt, D_MODEL), lambda t, h: (t, 0)),
        pl.BlockSpec((1, M_DV, D_MODEL), lambda t, h: (layer, ROW_MO // M_DV + h, 0)),
        pl.BlockSpec((lt, M_DQK), lambda t, h: (t, h)),
        pl.BlockSpec((M_DQK, lt), lambda t, h: (h, t)),
        pl.BlockSpec((lt, M_DV), lambda t, h: (t, h)),
        pl.BlockSpec((GATE_ROWS, lt), lambda t, h: (0, t)),
        pl.BlockSpec((1, M_DV), lambda t, h: (0, h)),
        pl.BlockSpec((1, 1, lt, 1), lambda t, h: (layer, h, t, 0)),
        c_spec, n_spec,
    ]
    aliases = {}
    if carried is not None:
        aliases = {len(operands) + k: 1 + k for k in range(3)}
        operands += list(carried)
        in_specs += [any_spec] * 3
    return pl.pallas_call(
        functools.partial(_mlstm_sample_kernel, seg=seg),
        grid=(dec_batch // nseg, M_HEADS),
        in_specs=in_specs,
        out_specs=[
            pl.BlockSpec((lt, M_DV), lambda t, h: (t, h)),
            c_spec, n_spec,
            pl.BlockSpec((1, 1, nseg, 1), lambda t, h: (layer, h, t, 0)),
        ],
        out_shape=[
            jax.ShapeDtypeStruct((dec_batch * seg, D_MODEL), F32),
            jax.ShapeDtypeStruct(state_c.shape, F32),
            jax.ShapeDtypeStruct(state_n.shape, F32),
            jax.ShapeDtypeStruct((depth, M_HEADS, dec_batch, 1), F32),
        ],
        input_output_aliases=aliases,
        compiler_params=_params("parallel", "arbitrary"),
        name="mlstm_sample",
    )(*operands)


def _swa_merge_prompt_kernel(sink_ref, bias_ref, qt_ref, kp_ref, kc_ref, vp_ref, vc_ref,
                             x_ref, xn_ref, ba_ref, wg_ref, w_ref,
                             o_ref, pk_ref, pv_ref, kb_scr, vt_scr, bb_scr):
    i = pl.program_id(1)
    nsub = qt_ref.shape[1] // WINDOW
    lanes = A_GROUP * WINDOW
    all_lanes = A_HEADS * WINDOW
    gates = _dot_nt(xn_ref[...], wg_ref[0])
    gated_a = _sigmoid(gates[:, :D_MODEL]) * ba_ref[...]
    gate_b = _sigmoid(gates[:, D_MODEL:])

    kb_scr[0:WINDOW] = kp_ref[...].astype(BF16)
    kb_scr[WINDOW:] = kc_ref[...].astype(BF16)
    vt_scr[:, 0:WINDOW] = vp_ref[...].T.astype(BF16)
    vt_scr[:, WINDOW:] = vc_ref[...].T.astype(BF16)

    lane_head = lax.broadcasted_iota(jnp.int32, (1, all_lanes), 1) >> (WINDOW.bit_length() - 1)
    sink = jnp.zeros((1, all_lanes), F32)
    for h in range(A_HEADS):
        sink = jnp.where(lane_head == h, sink_ref[h] * LOG2E, sink)
    has_prev = (lax.broadcasted_iota(jnp.int32, (2 * WINDOW, 1), 0) >= WINDOW) | (i > 0)
    zeros = jnp.zeros((A_HD, lanes), BF16)
    ones = jnp.ones((BF16_SUBLANES, 2 * WINDOW), BF16)

    for j in range(nsub):
        qt = qt_ref[:, j * WINDOW:(j + 1) * WINDOW]
        qt4 = [jnp.concatenate([qt[(kv * A_GROUP + g) * A_HD:(kv * A_GROUP + g + 1) * A_HD]
                                for g in range(A_GROUP)], axis=1) for kv in range(A_KV)]
        st = []
        for pair in range(A_KV // 2):
            rhs = jnp.concatenate([jnp.concatenate([qt4[2 * pair], zeros], axis=1),
                                   jnp.concatenate([zeros, qt4[2 * pair + 1]], axis=1)], axis=0)
            kpair = kb_scr[j * WINDOW:(j + 2) * WINDOW, pair * LANES:(pair + 1) * LANES]
            st.append(_dot(kpair, rhs))
        st = jnp.concatenate(st, axis=1) + bias_ref[...]
        if j == 0:
            st = jnp.where(has_prev, st, -jnp.inf)
        mx = jnp.maximum(jnp.max(st, axis=0, keepdims=True), sink)
        p = jnp.exp2(st - mx).astype(BF16)
        sink_p = jnp.exp2(sink - mx)
        pieces = []
        for kv in range(A_KV):
            vt = jnp.concatenate([vt_scr[kv * A_HD:(kv + 1) * A_HD, j * WINDOW:(j + 2) * WINDOW], ones], axis=0)
            ot = _dot(vt, p[:, kv * lanes:(kv + 1) * lanes])
            den = ot[A_HD:A_HD + 1] + sink_p[:, kv * lanes:(kv + 1) * lanes]
            ot = ot[:A_HD] * (1.0 / den)
            pieces.extend(ot[:, g * WINDOW:(g + 1) * WINDOW] for g in range(A_GROUP))
        bb_scr[j * WINDOW:(j + 1) * WINDOW, :] = jnp.concatenate(pieces, axis=0).T

    merged = gated_a + gate_b * bb_scr[...]
    o_ref[...] = x_ref[...] + _dot(merged.astype(BF16), w_ref[0])

    @pl.when(i == pl.num_programs(1) - 1)
    def _():
        rows = kc_ref.shape[0]
        pk_ref[0] = kc_ref[rows - WINDOW:rows, :].T
        pv_ref[0] = vc_ref[rows - WINDOW:rows, :].T


def _band_bias():
    c = jnp.arange(2 * WINDOW, dtype=jnp.int32)[:, None]
    t = jnp.arange(A_HEADS * WINDOW, dtype=jnp.int32)[None, :] % WINDOW
    return jnp.where((c > t) & (c <= t + WINDOW), 0.0, -jnp.inf).astype(F32)


def _swa_merge_prompt(aq_t, pf, sinks, bias, x, xn, ba, w_gates, w_out, layer, batch, seq):
    tq = _pick(seq, (SWA_STEP, 256, 128))
    ns = seq // tq
    per = tq // WINDOW
    nb = seq // WINDOW
    cur = lambda col: (lambda b, i: (b * ns + i, col))
    prev = lambda col: (lambda b, i: (b * nb + jnp.maximum(i * per - 1, 0), col))
    const = lambda b, i: (0, 0)
    row = pl.BlockSpec((tq, D_MODEL), cur(0))
    return pl.pallas_call(
        _swa_merge_prompt_kernel,
        grid=(batch, ns),
        in_specs=[
            pl.BlockSpec(memory_space=pltpu.SMEM),
            pl.BlockSpec(bias.shape, const),
            pl.BlockSpec((D_MODEL, tq), lambda b, i: (0, b * ns + i)),
            pl.BlockSpec((WINDOW, KV_W), prev(0)),
            pl.BlockSpec((tq, KV_W), cur(0)),
            pl.BlockSpec((WINDOW, KV_W), prev(1)),
            pl.BlockSpec((tq, KV_W), cur(1)),
            row, row, row,
            pl.BlockSpec((1, 2 * D_MODEL, D_MODEL), lambda b, i: (layer, 0, 0)),
            pl.BlockSpec((1, D_MODEL, D_MODEL), lambda b, i: (layer, 0, 0)),
        ],
        out_specs=[
            pl.BlockSpec((tq, D_MODEL), lambda b, i: (b * ns + i, 0)),
            pl.BlockSpec((1, KV_W, WINDOW), lambda b, i: (b, 0, 0)),
            pl.BlockSpec((1, KV_W, WINDOW), lambda b, i: (b, 0, 0)),
        ],
        out_shape=[
            jax.ShapeDtypeStruct((batch * seq, D_MODEL), F32),
            jax.ShapeDtypeStruct((batch, KV_W, WINDOW), F32),
            jax.ShapeDtypeStruct((batch, KV_W, WINDOW), F32),
        ],
        scratch_shapes=[pltpu.VMEM((WINDOW + tq, KV_W), BF16), pltpu.VMEM((KV_W, WINDOW + tq), BF16),
                        pltpu.VMEM((tq, D_MODEL), F32)],
        compiler_params=_params("parallel", "arbitrary"),
        name="swa_merge_prompt",
    )(sinks, bias, aq_t, pf, pf, pf, pf, x, xn, ba, w_gates, w_out)


def _bdot(a, b, contract_b):
    return lax.dot_general(a, b, (((2,), (contract_b,)), ((0,), (0,))), preferred_element_type=F32)


def _swa_sample_kernel(*refs, tq):
    sink_ref, q_ref, kn_ref, vn_ref, ck_ref, cv_ref = refs[:6]
    o_ref, sk_ref, sv_ref = refs[-3:]
    nb = ck_ref.shape[1]
    wc = ck_ref.shape[3]
    keep = wc - tq
    q3 = q_ref[...].astype(F32).reshape(nb, tq, D_MODEL)
    ck = ck_ref[0]
    cv = cv_ref[0]

    lane = lax.broadcasted_iota(jnp.int32, (nb, KV_W, wc), 2)

    def appended(cache, new_rows):
        new_t = new_rows.T
        placed = jnp.stack([pltpu.roll(new_t, (keep - b * tq) % wc, axis=1) for b in range(nb)])
        return jnp.where(lane >= keep, placed, pltpu.roll(cache, keep, axis=2))

    sk = appended(ck, kn_ref[...])
    sv = appended(cv, vn_ref[...])
    sk_ref[0] = sk
    sv_ref[0] = sv

    rows = A_GROUP * tq
    t_idx = lax.broadcasted_iota(jnp.int32, (1, rows, 1), 1) & (tq - 1)
    c_idx = lax.broadcasted_iota(jnp.int32, (1, 1, wc), 2)
    mask_old = c_idx > t_idx
    mask_new = (c_idx >= keep) & (c_idx - keep <= t_idx)
    g_idx = lax.broadcasted_iota(jnp.int32, (1, rows, 1), 1) >> (tq.bit_length() - 1)
    pieces = []
    for kv in range(A_KV):
        heads = [kv * A_GROUP + g for g in range(A_GROUP)]
        q4 = jnp.concatenate([q3[:, :, h * A_HD:(h + 1) * A_HD] for h in heads], axis=1).astype(BF16)
        sink = jnp.zeros((1, rows, 1), F32)
        for g in range(A_GROUP):
            sink = jnp.where(g_idx == g, sink_ref[heads[g]] * LOG2E, sink)
        sl = slice(kv * A_HD, (kv + 1) * A_HD)
        s1 = jnp.where(mask_old, _bdot(q4, ck[:, sl, :].astype(BF16), 1), -jnp.inf)
        s2 = jnp.where(mask_new, _bdot(q4, sk[:, sl, :].astype(BF16), 1), -jnp.inf)
        mx = jnp.maximum(jnp.maximum(jnp.max(s1, axis=2, keepdims=True),
                                     jnp.max(s2, axis=2, keepdims=True)), sink)
        p1 = jnp.exp2(s1 - mx)
        p2 = jnp.exp2(s2 - mx)
        den = (jnp.sum(p1, axis=2, keepdims=True) + jnp.sum(p2, axis=2, keepdims=True)
               + jnp.exp2(sink - mx))
        r = 1.0 / den
        o = (_bdot((p1 * r).astype(BF16), cv[:, sl, :].astype(BF16), 2)
             + _bdot((p2 * r).astype(BF16), sv[:, sl, :].astype(BF16), 2))
        pieces.extend(o[:, g * tq:(g + 1) * tq, :] for g in range(A_GROUP))
    o_ref[...] = jnp.concatenate(pieces, axis=2).reshape(nb * tq, D_MODEL)


def _swa_sample(aq, pf, sinks, cache_kt, cache_vt, carried, layer, dec_batch, tq):
    lt = SAMPLE_TILE
    nb = lt // tq
    wc = cache_kt.shape[3]
    assert wc == lt, "the appended keys are placed with lane rolls over one cache row"
    any_spec = pl.BlockSpec(memory_space=pl.ANY)
    cache_spec = pl.BlockSpec((1, nb, KV_W, wc), lambda i: (layer, i, 0, 0))
    operands = [sinks, aq, pf, pf, cache_kt, cache_vt]
    in_specs = [
        pl.BlockSpec(memory_space=pltpu.SMEM),
        pl.BlockSpec((lt, D_MODEL), lambda i: (i, 0)),
        pl.BlockSpec((lt, KV_W), lambda i: (i, 0)),
        pl.BlockSpec((lt, KV_W), lambda i: (i, 1)),
        cache_spec, cache_spec,
    ]
    aliases = {}
    if carried is not None:
        aliases = {len(operands) + k: 1 + k for k in range(2)}
        operands += list(carried)
        in_specs += [any_spec] * 2
    return pl.pallas_call(
        functools.partial(_swa_sample_kernel, tq=tq),
        grid=(dec_batch // nb,),
        in_specs=in_specs,
        out_specs=[
            pl.BlockSpec((lt, D_MODEL), lambda i: (i, 0)),
            cache_spec, cache_spec,
        ],
        out_shape=[
            jax.ShapeDtypeStruct((dec_batch * tq, D_MODEL), F32),
            jax.ShapeDtypeStruct(cache_kt.shape, F32),
            jax.ShapeDtypeStruct(cache_vt.shape, F32),
        ],
        input_output_aliases=aliases,
        compiler_params=_params("parallel"),
        name="swa_sample",
    )(*operands)


def _merge_kernel(x_ref, xn_ref, wg_ref, ba_ref, bb_ref, w_ref, o_ref):
    gates = _dot_nt(xn_ref[...], wg_ref[0])
    merged = _sigmoid(gates[:, :D_MODEL]) * ba_ref[...] + _sigmoid(gates[:, D_MODEL:]) * bb_ref[...]
    o_ref[...] = x_ref[...] + _dot(merged.astype(BF16), w_ref[0])


def _merge(x, xn, w_gates, ba, bb, w_out, layer):
    n = x.shape[0]
    tm = _pick(n, (512, 256, 128))
    row = pl.BlockSpec((tm, D_MODEL), lambda i: (i, 0))
    weight = lambda rows: pl.BlockSpec((1, rows, D_MODEL), lambda i: (layer, 0, 0))
    return pl.pallas_call(
        _merge_kernel,
        grid=(n // tm,),
        in_specs=[row, row, weight(2 * D_MODEL), row, row, weight(D_MODEL)],
        out_specs=row,
        out_shape=jax.ShapeDtypeStruct((n, D_MODEL), F32),
        compiler_params=_params("parallel"),
        name="merge_out_proj",
    )(x, xn, w_gates, ba, bb, w_out)


def _mlp_kernel(x_ref, g_ref, wu_ref, wd_ref, gf_ref, o_ref, xn_ref, *, final_norm):
    j = pl.program_id(1)

    @pl.when(j == 0)
    def _():
        x = x_ref[...]
        xn_ref[...] = _rms(x, g_ref[...]).astype(BF16)
        o_ref[...] = x

    h = jnp.square(jnp.maximum(_dot(xn_ref[...], wu_ref[0]), 0.0))
    o_ref[...] += _dot(h.astype(BF16), wd_ref[0])

    if final_norm:
        @pl.when(j == pl.num_programs(1) - 1)
        def _():
            o_ref[...] = _rms(o_ref[...], gf_ref[...])


def _mlp(x, g, w_up, w_down, g_final, layer, final_norm):
    n = x.shape[0]
    tm = _pick(n, (1024, 512, 256, 128))
    tf = 2048
    return pl.pallas_call(
        functools.partial(_mlp_kernel, final_norm=final_norm),
        grid=(n // tm, D_FF // tf),
        in_specs=[
            pl.BlockSpec((tm, D_MODEL), lambda i, j: (i, 0)),
            pl.BlockSpec((1, D_MODEL), lambda i, j: (0, 0)),
            pl.BlockSpec((1, D_MODEL, tf), lambda i, j: (layer, 0, j)),
            pl.BlockSpec((1, tf, D_MODEL), lambda i, j: (layer, j, 0)),
            pl.BlockSpec((1, D_MODEL), lambda i, j: (0, 0)),
        ],
        out_specs=pl.BlockSpec((tm, D_MODEL), lambda i, j: (i, 0)),
        out_shape=jax.ShapeDtypeStruct((n, D_MODEL), F32),
        scratch_shapes=[pltpu.VMEM((tm, D_MODEL), BF16)],
        compiler_params=_params("parallel", "arbitrary"),
        name="mlp",
    )(x, g, w_up, w_down, g_final)


def _prep_in_proj_weights(w_in):
    assert w_in.shape[2] == D_IN
    w_all = jnp.swapaxes(w_in, 1, 2).astype(BF16)
    return w_all, w_all[:, ROW_AQ:ROW_MERGE], w_all[:, ROW_MERGE:D_IN]


def kernel(x_prompt, x_sample, state_C, state_n, state_m, cache_k, cache_v, norm_attn, w_in, b_gate,
           mlstm_norm, sinks, w_out, norm_mlp, w_up, w_down, norm_final):
    batch, seq, _ = x_prompt.shape
    dec_batch, dec_seq, _ = x_sample.shape
    depth = w_in.shape[0]
    wc = cache_k.shape[2]

    w_all, w_att, w_merge = _prep_in_proj_weights(w_in)
    b_gate_col = jnp.zeros((depth, GATE_ROWS, 1), F32).at[:, :2 * M_HEADS, 0].set(b_gate.astype(F32))
    w_out_b = w_out.astype(BF16)
    w_up_b = w_up.astype(BF16)
    w_down_b = w_down.astype(BF16)
    bias = _band_bias()

    xp = x_prompt.reshape(batch * seq, D_MODEL)
    xs = x_sample.reshape(dec_batch * dec_seq, D_MODEL)
    m_tok = jnp.repeat(jnp.swapaxes(state_m, 1, 2), dec_seq, axis=2)[..., None]
    cache_kt = jnp.transpose(cache_k, (0, 1, 3, 4, 2)).reshape(depth, dec_batch, KV_W, wc)
    cache_vt = jnp.transpose(cache_v, (0, 1, 3, 4, 2)).reshape(depth, dec_batch, KV_W, wc)

    states = None
    caches = None
    p_c, p_n, p_m, p_k, p_v = [], [], [], [], []
    g_final = norm_final[None]

    for l in range(depth):
        g_attn = norm_attn[l][None]
        nw = mlstm_norm[l][None]
        g_mlp = norm_mlp[l][None]
        last = l == depth - 1
        proj_w = (w_all, w_att, b_gate_col[l], l)

        xn, mv, mq, pf, kt, gt, aq_t = _in_proj(xp, g_attn, *proj_w, q_transposed=True)
        ba, pc, pn, pm = _mlstm_prompt(xn, w_all, mv, mq, kt, gt, nw, l, batch, seq)
        xp, pk, pv = _swa_merge_prompt(aq_t, pf, sinks[l], bias, xp, xn, ba, w_merge, w_out_b, l, batch, seq)
        xp = _mlp(xp, g_mlp, w_up_b, w_down_b, g_final, l, last)
        p_c.append(pc)
        p_n.append(pn)
        p_m.append(pm[:, :M_HEADS, 0])
        p_k.append(pk)
        p_v.append(pv)

        xn, mv, mq, pf, kt, gt, aq = _in_proj(xs, g_attn, *proj_w, q_transposed=False)
        ba, *states = _mlstm_sample(xn, w_all, mv, mq, kt, gt, nw, m_tok, state_C, state_n, states,
                                    l, dec_batch, dec_seq)
        bb, *caches = _swa_sample(aq, pf, sinks[l], cache_kt, cache_vt, caches, l, dec_batch, dec_seq)
        xs = _merge(xs, xn, w_merge, ba, bb, w_out_b, l)
        xs = _mlp(xs, g_mlp, w_up_b, w_down_b, g_final, l, last)

    s_c, s_n, s_m = states
    s_k, s_v = caches

    def positions_major(t, lead):
        t = t.reshape(*lead, A_KV, A_HD, wc)
        return jnp.moveaxis(t, -1, -3)

    return (xp.reshape(batch, seq, D_MODEL), xs.reshape(dec_batch, dec_seq, D_MODEL),
            jnp.stack(p_c), jnp.stack(p_n), jnp.stack(p_m),
            positions_major(jnp.stack(p_k), (depth, batch)), positions_major(jnp.stack(p_v), (depth, batch)),
            s_c, s_n, jnp.swapaxes(s_m[..., 0], 1, 2),
            positions_major(s_k, (depth, dec_batch)), positions_major(s_v, (depth, dec_batch)))
```

```python
import functools

import jax
import jax.numpy as jnp
from jax import lax
from jax.experimental import pallas as pl
from jax.experimental.pallas import tpu as pltpu

F32 = jnp.float32
BF16 = jnp.bfloat16

D_MODEL = 1024
M_HEADS = 4
M_DQK = 128
M_DV = D_MODEL // M_HEADS
MQ_W = M_HEADS * M_DQK
A_HEADS = 16
A_KV = 4
A_GROUP = A_HEADS // A_KV
A_HD = D_MODEL // A_HEADS
KV_W = A_KV * A_HD
WINDOW = 128
D_FF = 4 * D_MODEL
EPS = 1e-6
LOG2E = 1.4426950408889634

ROW_MQ = 0
ROW_MK = ROW_MQ + MQ_W
ROW_MV = ROW_MK + MQ_W
ROW_MO = ROW_MV + D_MODEL
ROW_GATES = ROW_MO + D_MODEL
ROW_AQ = ROW_GATES + 2 * M_HEADS
ROW_MERGE = ROW_AQ + D_MODEL + 2 * KV_W
D_IN = ROW_MERGE + 2 * D_MODEL
GATE_ROWS = 16

LANES = 128
SUBLANES = 8
BF16_SUBLANES = 16
VMEM_BYTES = 64 * 1024 * 1024
VMEM_LIMIT = VMEM_BYTES * 7 // 8

MLSTM_CHUNK = 256
SAMPLE_TILE = 128
SWA_STEP = 512


def _pick(n, candidates):
    for c in candidates:
        if n % c == 0:
            return c
    raise ValueError(f"no block size for {n}")


def _sigmoid(x):
    return 0.5 * jnp.tanh(0.5 * x) + 0.5


def _log_sigmoid(x):
    return jnp.minimum(x, 0.0) - jnp.log1p(jnp.exp(-jnp.abs(x)))


def _rms(x, g):
    y = x * lax.rsqrt(jnp.mean(x * x, axis=-1, keepdims=True) + EPS)
    return y * g


def _dot(a, b):
    return jnp.dot(a, b, preferred_element_type=F32)


def _dot_nt(a, b):
    return lax.dot_general(a, b, (((1,), (1,)), ((), ())), preferred_element_type=F32)


def _params(*sem):
    return pltpu.CompilerParams(dimension_semantics=sem, vmem_limit_bytes=VMEM_LIMIT)


def _in_proj_kernel(x_ref, g_ref, wmq_ref, wmk_ref, wmv_ref, wg_ref, waq_ref, wkv_ref, bg_ref,
                    xn_out, mv_ref, mq_ref, pf_ref, kt_ref, gt_ref, aq_ref, *, q_transposed):
    xn = _rms(x_ref[...], g_ref[...]).astype(BF16)
    xn_out[...] = xn
    kt_ref[...] = _dot_nt(wmk_ref[0], xn).astype(BF16)
    gt_ref[...] = _dot_nt(wg_ref[0], xn) + bg_ref[...]
    mv_ref[...] = _dot_nt(xn, wmv_ref[0]).astype(BF16)
    if q_transposed:
        aq_ref[...] = (_dot_nt(waq_ref[0], xn) * (A_HD ** -0.5 * LOG2E)).astype(BF16)
    else:
        aq_ref[...] = (_dot_nt(xn, waq_ref[0]) * (A_HD ** -0.5 * LOG2E)).astype(BF16)
    mq_ref[...] = (_dot_nt(xn, wmq_ref[0]) * (M_DQK ** -0.5)).astype(BF16)
    pf_ref[...] = _dot_nt(xn, wkv_ref[0])


def _in_proj(x, g, w_all, w_att, bg, layer, q_transposed):
    n = x.shape[0]
    tm = _pick(n, (1024, 512, 256, 128))
    const = lambda i: (0, 0)
    row = lambda i: (i, 0)
    col = lambda i: (0, i)
    wblk = lambda rows, off: pl.BlockSpec((1, rows, D_MODEL), lambda i: (layer, off // rows, 0))
    if q_transposed:
        aq_spec, aq_shape = pl.BlockSpec((D_MODEL, tm), col), (D_MODEL, n)
    else:
        aq_spec, aq_shape = pl.BlockSpec((tm, D_MODEL), row), (n, D_MODEL)
    return pl.pallas_call(
        functools.partial(_in_proj_kernel, q_transposed=q_transposed),
        grid=(n // tm,),
        in_specs=[
            pl.BlockSpec((tm, D_MODEL), row),
            pl.BlockSpec((1, D_MODEL), const),
            wblk(MQ_W, ROW_MQ), wblk(MQ_W, ROW_MK), wblk(D_MODEL, ROW_MV), wblk(GATE_ROWS, ROW_GATES),
            wblk(D_MODEL, 0), wblk(2 * KV_W, D_MODEL),
            pl.BlockSpec((GATE_ROWS, 1), const),
        ],
        out_specs=[
            pl.BlockSpec((tm, D_MODEL), row),
            pl.BlockSpec((tm, D_MODEL), row),
            pl.BlockSpec((tm, MQ_W), row),
            pl.BlockSpec((tm, 2 * KV_W), row),
            pl.BlockSpec((MQ_W, tm), col),
            pl.BlockSpec((GATE_ROWS, tm), col),
            aq_spec,
        ],
        out_shape=[
            jax.ShapeDtypeStruct((n, D_MODEL), BF16),
            jax.ShapeDtypeStruct((n, D_MODEL), BF16),
            jax.ShapeDtypeStruct((n, MQ_W), BF16),
            jax.ShapeDtypeStruct((n, 2 * KV_W), F32),
            jax.ShapeDtypeStruct((MQ_W, n), BF16),
            jax.ShapeDtypeStruct((GATE_ROWS, n), F32),
            jax.ShapeDtypeStruct(aq_shape, BF16),
        ],
        compiler_params=_params("parallel"),
        name="in_proj",
    )(x, g, w_all, w_all, w_all, w_all, w_att, w_att, bg)


def _seg_cumsum(x, seg):
    pos = lax.broadcasted_iota(jnp.int32, x.shape, 1) & (seg - 1)
    sh = 1
    while sh < seg:
        x = x + jnp.where(pos >= sh, pltpu.roll(x, sh, axis=1), 0.0)
        sh *= 2
    return x


def _mlstm_finish(num, d, o_pre, nw):
    msq = jnp.mean(num * num, axis=-1, keepdims=True)
    return _sigmoid(o_pre) * (num * lax.rsqrt(msq + EPS * d * d)) * nw


def _mlstm_prompt_kernel(*refs, nseq):
    xn_ref, wo_ref, q_ref, v_ref, nw_ref = refs[:5]
    kt_refs = refs[5:5 + nseq]
    gt_refs = refs[5 + nseq:5 + 2 * nseq]
    h_ref, c_out, n_out, m_out, c_scr, n_scr, m_scr = refs[5 + 2 * nseq:]
    c = pl.program_id(0)
    lt = q_ref.shape[1]
    units = [(b, h) for b in range(nseq) for h in range(M_HEADS)]
    idx = range(len(units))

    @pl.when(c == 0)
    def _():
        c_scr[...] = jnp.zeros_like(c_scr)
        n_scr[...] = jnp.zeros_like(n_scr)
        m_scr[...] = jnp.zeros_like(m_scr)

    i2, b2, b2_t = [], [], []
    for b in range(nseq):
        gt = gt_refs[b][...]
        i2.append(gt * LOG2E)
        b2.append(_seg_cumsum(_log_sigmoid(gt), lt) * LOG2E)
        b2_t.append(jnp.concatenate([b2[b], jnp.zeros((LANES - GATE_ROWS, lt), F32)], axis=0).T)
    t_idx = lax.broadcasted_iota(jnp.int32, (lt, lt), 0)
    s_idx = lax.broadcasted_iota(jnp.int32, (lt, lt), 1)
    causal = s_idx <= t_idx

    q = [q_ref[b, :, h * M_DQK:(h + 1) * M_DQK] for b, h in units]
    kt = [kt_refs[b][h * M_DQK:(h + 1) * M_DQK, :] for b, h in units]
    v = [v_ref[b, :, h * M_DV:(h + 1) * M_DV] for b, h in units]
    m2_col = [jnp.broadcast_to(m_scr[SUBLANES * b + h:SUBLANES * b + h + 1, 0:1] * LOG2E, (lt, 1)) for b, h in units]
    n_old = [n_scr[SUBLANES * b + h:SUBLANES * b + h + 1, :] for b, h in units]
    c_old = [c_scr[M_HEADS * b + h] for b, h in units]

    a_mat = [jnp.where(causal, i2[b][h:h + 1] - b2[b][M_HEADS + h:M_HEADS + h + 1], -jnp.inf)
             for b, h in units]
    g_col = [jnp.maximum(jnp.max(a_mat[u], axis=1, keepdims=True), m2_col[u]) for u in idx]
    w = [jnp.exp2(a_mat[u] - g_col[u]) for u in idx]
    w_int = [jnp.exp2(m2_col[u] - g_col[u]) for u in idx]
    s = [_dot(q[u], kt[u]) * w[u] for u in idx]
    inter = [_dot(q[u], c_old[u].astype(BF16)) for u in idx]
    qn = [_dot_nt(q[u], jnp.broadcast_to(n_old[u], (SUBLANES, M_DQK)).astype(BF16))[:, 0:1] for u in idx]
    num = [w_int[u] * inter[u] + _dot(s[u].astype(BF16), v[u]) for u in idx]
    den = [w_int[u] * qn[u] + jnp.sum(s[u], axis=1, keepdims=True) for u in idx]
    m2_row = [b2_t[b][:, M_HEADS + h:M_HEADS + h + 1] + g_col[u] for u, (b, h) in enumerate(units)]
    floor = [jnp.maximum(jnp.abs(den[u]), jnp.exp2(-m2_row[u])) for u in idx]

    wk_row = [w[u][lt - 1:lt, :] for u in idx]
    c_upd = [_dot((kt[u].astype(F32) * wk_row[u]).astype(BF16), v[u]) for u in idx]
    n_upd = [_dot_nt(jnp.broadcast_to(wk_row[u], (SUBLANES, lt)).astype(BF16), kt[u]) for u in idx]
    for u, (b, h) in enumerate(units):
        decay = w_int[u][lt - 1:lt]
        c_scr[M_HEADS * b + h] = decay * c_old[u] + c_upd[u]
        n_scr[SUBLANES * b + h:SUBLANES * b + h + 1, :] = decay * n_old[u] + n_upd[u][0:1]
        m_scr[SUBLANES * b + h:SUBLANES * b + h + 1, :] = jnp.broadcast_to(
            m2_row[u][lt - 1:lt] * (1.0 / LOG2E), (1, m_scr.shape[1]))

    o_pre = _dot_nt(xn_ref[...].reshape(nseq * lt, D_MODEL), wo_ref[0])
    for u, (b, h) in enumerate(units):
        sl = slice(h * M_DV, (h + 1) * M_DV)
        h_ref[b, :, sl] = _mlstm_finish(num[u], floor[u], o_pre[b * lt:(b + 1) * lt, sl], nw_ref[:, sl])

    @pl.when(c == pl.num_programs(0) - 1)
    def _():
        c_out[...] = c_scr[...].reshape(c_out.shape)
        for b in range(nseq):
            n_out[b] = n_scr[SUBLANES * b:SUBLANES * b + M_HEADS]
        m_out[...] = m_scr[...].reshape(m_out.shape)


def _mlstm_prompt(xn, w_all, mv, mq, kt, gt, nw, layer, batch, seq):
    lt = _pick(seq, (MLSTM_CHUNK, 128))
    nc = seq // lt
    tok = lambda width: pl.BlockSpec((batch, lt, width), lambda c: (0, c, 0))
    const = lambda shape: pl.BlockSpec(shape, lambda c: (0,) * len(shape))
    lanes = lambda rows: [pl.BlockSpec((rows, lt), lambda c, b=b: (0, b * nc + c)) for b in range(batch)]
    per_seq = lambda a: a.reshape(batch, seq, a.shape[-1])
    w_mo = pl.BlockSpec((1, D_MODEL, D_MODEL), lambda c: (layer, ROW_MO // D_MODEL, 0))
    ba, pc, pn, pm = pl.pallas_call(
        functools.partial(_mlstm_prompt_kernel, nseq=batch),
        grid=(nc,),
        in_specs=[tok(D_MODEL), w_mo, tok(MQ_W), tok(D_MODEL), const((1, D_MODEL))]
        + lanes(MQ_W) + lanes(GATE_ROWS),
        out_specs=[
            tok(D_MODEL),
            const((batch, M_HEADS, M_DQK, M_DV)),
            const((batch, M_HEADS, M_DQK)),
            const((batch, SUBLANES, LANES)),
        ],
        out_shape=[
            jax.ShapeDtypeStruct((batch, seq, D_MODEL), F32),
            jax.ShapeDtypeStruct((batch, M_HEADS, M_DQK, M_DV), F32),
            jax.ShapeDtypeStruct((batch, M_HEADS, M_DQK), F32),
            jax.ShapeDtypeStruct((batch, SUBLANES, LANES), F32),
        ],
        scratch_shapes=[pltpu.VMEM((batch * M_HEADS, M_DQK, M_DV), F32), pltpu.VMEM((batch * SUBLANES, M_DQK), F32),
                        pltpu.VMEM((batch * SUBLANES, LANES), F32)],
        compiler_params=_params("arbitrary"),
        name="mlstm_prompt",
    )(per_seq(xn), w_all, per_seq(mq), per_seq(mv), nw, *([kt] * batch), *([gt] * batch))
    return ba.reshape(batch * seq, D_MODEL), pc, pn, pm


def _mlstm_sample_kernel(*refs, seg):
    xn_ref, wo_ref, q_ref, kt_ref, v_ref, gt_ref, nw_ref, m_ref, c_in, n_in = refs[:10]
    h_ref, c_out, n_out, m_out = refs[-4:]
    hd = pl.program_id(1)
    lt = q_ref.shape[0]
    nseg = lt // seg
    gt = gt_ref[...]
    lf = _log_sigmoid(gt)
    bcs = _seg_cumsum(lf, seg)
    row = lax.broadcasted_iota(jnp.int32, (GATE_ROWS, lt), 0)

    def pick(x, r):
        return jnp.sum(jnp.where(row == r, x, 0.0), axis=0, keepdims=True)

    i_row, lf_row, b_row = pick(gt, hd), pick(lf, M_HEADS + hd), pick(bcs, M_HEADS + hd)
    t_idx = lax.broadcasted_iota(jnp.int32, (lt, lt), 0)
    s_idx = lax.broadcasted_iota(jnp.int32, (lt, lt), 1)
    causal = (s_idx <= t_idx) & ((s_idx | (seg - 1)) == (t_idx | (seg - 1)))
    last = t_idx == (s_idx | (seg - 1))

    q = q_ref[...]
    kt = kt_ref[...]
    v = v_ref[...]
    m_col = m_ref[0, 0]
    n_old = [n_in[0, i, pl.ds(hd, 1), :] for i in range(nseg)]
    n_tok = jnp.concatenate([jnp.broadcast_to(n, (seg, M_DQK)) for n in n_old], axis=0)

    a_mat = jnp.where(causal, i_row - b_row, -jnp.inf)
    g_col = jnp.maximum(jnp.max(a_mat, axis=1, keepdims=True), m_col)
    b_col = jnp.sum(jnp.where(causal, lf_row, 0.0), axis=1, keepdims=True)
    w = jnp.exp(a_mat - g_col)
    w_int = jnp.exp(m_col - g_col)
    s = _dot(q, kt) * w
    inter = jnp.concatenate(
        [_dot(q[i * seg:(i + 1) * seg], c_in[0, i, 0].astype(BF16)) for i in range(nseg)], axis=0)
    qn = jnp.sum(q.astype(F32) * n_tok, axis=1, keepdims=True)
    num = w_int * inter + _dot(s.astype(BF16), v)
    den = w_int * qn + jnp.sum(s, axis=1, keepdims=True)
    m_row = b_col + g_col
    floor = jnp.maximum(jnp.abs(den), jnp.exp(-m_row))

    wk_row = jnp.sum(jnp.where(last, w, 0.0), axis=0, keepdims=True)
    ktw = kt.astype(F32) * wk_row
    lane_seg = lax.broadcasted_iota(jnp.int32, (1, lt), 1) | (seg - 1)
    lhs = jnp.concatenate(
        [jnp.where(lane_seg == i * seg + seg - 1, ktw, 0.0).astype(BF16) for i in range(nseg)], axis=0)
    c_upd = _dot(lhs, v)
    seg_row = lax.broadcasted_iota(jnp.int32, (lt, 1), 0) * seg + (seg - 1)
    n_upd = _dot_nt(jnp.where(lane_seg == seg_row, wk_row, 0.0).astype(BF16), kt)
    new_m = []
    for i in range(nseg):
        r = i * seg + seg - 1
        decay = w_int[r:r + 1]
        c_out[0, i, 0] = decay * c_in[0, i, 0] + c_upd[i * M_DQK:(i + 1) * M_DQK]
        n_out[0, i, pl.ds(hd, 1), :] = decay * n_old[i] + n_upd[i:i + 1]
        new_m.append(m_row[r:r + 1])
    m_out[0, 0] = jnp.concatenate(new_m, axis=0)
    h_ref[...] = _mlstm_finish(num, floor, _dot_nt(xn_ref[...], wo_ref[0]), nw_ref[...])


def _mlstm_sample(xn, w_all, mv, mq, kt, gt, nw, m_tok, state_c, state_n, carried, layer, dec_batch, seg):
    lt = SAMPLE_TILE
    nseg = lt // seg
    depth = state_c.shape[0]
    any_spec = pl.BlockSpec(memory_space=pl.ANY)
    c_spec = pl.BlockSpec((1, nseg, 1, M_DQK, M_DV), lambda t, h: (layer, t, h, 0, 0))
    n_spec = pl.BlockSpec((1, nseg, M_HEADS, M_DQK), lambda t, h: (layer, t, 0, 0))
    operands = [xn, w_all, mq, kt, mv, gt, nw, m_tok, state_c, state_n]
    in_specs = [
        pl.BlockSpec((lt, D_MODEL), lambda t, h: (t, 0)),
        pl.BlockSpec((1, M_DV, D_MODEL), lambda t, h: (layer, ROW_MO // M_DV + h, 0)),
        pl.BlockSpec((lt, M_DQK), lambda t, h: (t, h)),
        pl.BlockSpec((M_DQK, lt), lambda t, h: (h, t)),
        pl.BlockSpec((lt, M_DV), lambda t, h: (t, h)),
        pl.BlockSpec((GATE_ROWS, lt), lambda t, h: (0, t)),
        pl.BlockSpec((1, M_DV), lambda t, h: (0, h)),
        pl.BlockSpec((1, 1, lt, 1), lambda t, h: (layer, h, t, 0)),
        c_spec, n_spec,
    ]
    aliases = {}
    if carried is not None:
        aliases = {len(operands) + k: 1 + k for k in range(3)}
        operands += list(carried)
        in_specs += [any_spec] * 3
    return pl.pallas_call(
        functools.partial(_mlstm_sample_kernel, seg=seg),
        grid=(dec_batch // nseg, M_HEADS),
        in_specs=in_specs,
        out_specs=[
            pl.BlockSpec((lt, M_DV), lambda t, h: (t, h)),
            c_spec, n_spec,
            pl.BlockSpec((1, 1, nseg, 1), lambda t, h: (layer, h, t, 0)),
        ],
        out_shape=[
            jax.ShapeDtypeStruct((dec_batch * seg, D_MODEL), F32),
            jax.ShapeDtypeStruct(state_c.shape, F32),
            jax.ShapeDtypeStruct(state_n.shape, F32),
            jax.ShapeDtypeStruct((depth, M_HEADS, dec_batch, 1), F32),
        ],
        input_output_aliases=aliases,
        compiler_params=_params("parallel", "arbitrary"),
        name="mlstm_sample",
    )(*operands)


def _swa_merge_prompt_kernel(sink_ref, bias_ref, qt_ref, kp_ref, kc_ref, vp_ref, vc_ref,
                             x_ref, xn_ref, ba_ref, wg_ref, w_ref,
                             o_ref, pk_ref, pv_ref, kb_scr, vt_scr, bb_scr):
    i = pl.program_id(1)
    nsub = qt_ref.shape[1] // WINDOW
    lanes = A_GROUP * WINDOW
    all_lanes = A_HEADS * WINDOW
    gates = _dot_nt(xn_ref[...], wg_ref[0])
    gated_a = _sigmoid(gates[:, :D_MODEL]) * ba_ref[...]
    gate_b = _sigmoid(gates[:, D_MODEL:])

    kb_scr[0:WINDOW] = kp_ref[...].astype(BF16)
    kb_scr[WINDOW:] = kc_ref[...].astype(BF16)
    vt_scr[:, 0:WINDOW] = vp_ref[...].T.astype(BF16)
    vt_scr[:, WINDOW:] = vc_ref[...].T.astype(BF16)

    lane_head = lax.broadcasted_iota(jnp.int32, (1, all_lanes), 1) >> (WINDOW.bit_length() - 1)
    sink = jnp.zeros((1, all_lanes), F32)
    for h in range(A_HEADS):
        sink = jnp.where(lane_head == h, sink_ref[h] * LOG2E, sink)
    has_prev = (lax.broadcasted_iota(jnp.int32, (2 * WINDOW, 1), 0) >= WINDOW) | (i > 0)
    zeros = jnp.zeros((A_HD, lanes), BF16)
    ones = jnp.ones((BF16_SUBLANES, 2 * WINDOW), BF16)

    for j in range(nsub):
        qt = qt_ref[:, j * WINDOW:(j + 1) * WINDOW]
        qt4 = [jnp.concatenate([qt[(kv * A_GROUP + g) * A_HD:(kv * A_GROUP + g + 1) * A_HD]
                                for g in range(A_GROUP)], axis=1) for kv in range(A_KV)]
        st = []
        for pair in range(A_KV // 2):
            rhs = jnp.concatenate([jnp.concatenate([qt4[2 * pair], zeros], axis=1),
                                   jnp.concatenate([zeros, qt4[2 * pair + 1]], axis=1)], axis=0)
            kpair = kb_scr[j * WINDOW:(j + 2) * WINDOW, pair * LANES:(pair + 1) * LANES]
            st.append(_dot(kpair, rhs))
        st = jnp.concatenate(st, axis=1) + bias_ref[...]
        if j == 0:
            st = jnp.where(has_prev, st, -jnp.inf)
        mx = jnp.maximum(jnp.max(st, axis=0, keepdims=True), sink)
        p = jnp.exp2(st - mx).astype(BF16)
        sink_p = jnp.exp2(sink - mx)
        pieces = []
        for kv in range(A_KV):
            vt = jnp.concatenate([vt_scr[kv * A_HD:(kv + 1) * A_HD, j * WINDOW:(j + 2) * WINDOW], ones], axis=0)
            ot = _dot(vt, p[:, kv * lanes:(kv + 1) * lanes])
            den = ot[A_HD:A_HD + 1] + sink_p[:, kv * lanes:(kv + 1) * lanes]
            ot = ot[:A_HD] * (1.0 / den)
            pieces.extend(ot[:, g * WINDOW:(g + 1) * WINDOW] for g in range(A_GROUP))
        bb_scr[j * WINDOW:(j + 1) * WINDOW, :] = jnp.concatenate(pieces, axis=0).T

    merged = gated_a + gate_b * bb_scr[...]
    o_ref[...] = x_ref[...] + _dot(merged.astype(BF16), w_ref[0])

    @pl.when(i == pl.num_programs(1) - 1)
    def _():
        rows = kc_ref.shape[0]
        pk_ref[0] = kc_ref[rows - WINDOW:rows, :].T
        pv_ref[0] = vc_ref[rows - WINDOW:rows, :].T


def _band_bias():
    c = jnp.arange(2 * WINDOW, dtype=jnp.int32)[:, None]
    t = jnp.arange(A_HEADS * WINDOW, dtype=jnp.int32)[None, :] % WINDOW
    return jnp.where((c > t) & (c <= t + WINDOW), 0.0, -jnp.inf).astype(F32)


def _swa_merge_prompt(aq_t, pf, sinks, bias, x, xn, ba, w_gates, w_out, layer, batch, seq):
    tq = _pick(seq, (SWA_STEP, 256, 128))
    ns = seq // tq
    per = tq // WINDOW
    nb = seq // WINDOW
    cur = lambda col: (lambda b, i: (b * ns + i, col))
    prev = lambda col: (lambda b, i: (b * nb + jnp.maximum(i * per - 1, 0), col))
    const = lambda b, i: (0, 0)
    row = pl.BlockSpec((tq, D_MODEL), cur(0))
    return pl.pallas_call(
        _swa_merge_prompt_kernel,
        grid=(batch, ns),
        in_specs=[
            pl.BlockSpec(memory_space=pltpu.SMEM),
            pl.BlockSpec(bias.shape, const),
            pl.BlockSpec((D_MODEL, tq), lambda b, i: (0, b * ns + i)),
            pl.BlockSpec((WINDOW, KV_W), prev(0)),
            pl.BlockSpec((tq, KV_W), cur(0)),
            pl.BlockSpec((WINDOW, KV_W), prev(1)),
            pl.BlockSpec((tq, KV_W), cur(1)),
            row, row, row,
            pl.BlockSpec((1, 2 * D_MODEL, D_MODEL), lambda b, i: (layer, 0, 0)),
            pl.BlockSpec((1, D_MODEL, D_MODEL), lambda b, i: (layer, 0, 0)),
        ],
        out_specs=[
            pl.BlockSpec((tq, D_MODEL), lambda b, i: (b * ns + i, 0)),
            pl.BlockSpec((1, KV_W, WINDOW), lambda b, i: (b, 0, 0)),
            pl.BlockSpec((1, KV_W, WINDOW), lambda b, i: (b, 0, 0)),
        ],
        out_shape=[
            jax.ShapeDtypeStruct((batch * seq, D_MODEL), F32),
            jax.ShapeDtypeStruct((batch, KV_W, WINDOW), F32),
            jax.ShapeDtypeStruct((batch, KV_W, WINDOW), F32),
        ],
        scratch_shapes=[pltpu.VMEM((WINDOW + tq, KV_W), BF16), pltpu.VMEM((KV_W, WINDOW + tq), BF16),
                        pltpu.VMEM((tq, D_MODEL), F32)],
        compiler_params=_params("parallel", "arbitrary"),
        name="swa_merge_prompt",
    )(sinks, bias, aq_t, pf, pf, pf, pf, x, xn, ba, w_gates, w_out)


def _bdot(a, b, contract_b):
    return lax.dot_general(a, b, (((2,), (contract_b,)), ((0,), (0,))), preferred_element_type=F32)


def _swa_sample_kernel(*refs, tq):
    sink_ref, q_ref, kn_ref, vn_ref, ck_ref, cv_ref = refs[:6]
    o_ref, sk_ref, sv_ref = refs[-3:]
    nb = ck_ref.shape[1]
    wc = ck_ref.shape[3]
    keep = wc - tq
    q3 = q_ref[...].astype(F32).reshape(nb, tq, D_MODEL)
    ck = ck_ref[0]
    cv = cv_ref[0]

    lane = lax.broadcasted_iota(jnp.int32, (nb, KV_W, wc), 2)

    def appended(cache, new_rows):
        new_t = new_rows.T
        placed = jnp.stack([pltpu.roll(new_t, (keep - b * tq) % wc, axis=1) for b in range(nb)])
        return jnp.where(lane >= keep, placed, pltpu.roll(cache, keep, axis=2))

    sk = appended(ck, kn_ref[...])
    sv = appended(cv, vn_ref[...])
    sk_ref[0] = sk
    sv_ref[0] = sv

    rows = A_GROUP * tq
    t_idx = lax.broadcasted_iota(jnp.int32, (1, rows, 1), 1) & (tq - 1)
    c_idx = lax.broadcasted_iota(jnp.int32, (1, 1, wc), 2)
    mask_old = c_idx > t_idx
    mask_new = (c_idx >= keep) & (c_idx - keep <= t_idx)
    g_idx = lax.broadcasted_iota(jnp.int32, (1, rows, 1), 1) >> (tq.bit_length() - 1)
    ones = jnp.ones((nb, BF16_SUBLANES, wc), BF16)
    pieces = []
    for kv in range(A_KV):
        heads = [kv * A_GROUP + g for g in range(A_GROUP)]
        q4 = jnp.concatenate([q3[:, :, h * A_HD:(h + 1) * A_HD] for h in heads], axis=1).astype(BF16)
        sink = jnp.zeros((1, rows, 1), F32)
        for g in range(A_GROUP):
            sink = jnp.where(g_idx == g, sink_ref[heads[g]] * LOG2E, sink)
        sl = slice(kv * A_HD, (kv + 1) * A_HD)
        s1 = jnp.where(mask_old, _bdot(q4, ck[:, sl, :].astype(BF16), 1), -jnp.inf)
        s2 = jnp.where(mask_new, _bdot(q4, sk[:, sl, :].astype(BF16), 1), -jnp.inf)
        mx = jnp.maximum(jnp.maximum(jnp.max(s1, axis=2, keepdims=True),
                                     jnp.max(s2, axis=2, keepdims=True)), sink)
        p1 = jnp.exp2(s1 - mx).astype(BF16)
        p2 = jnp.exp2(s2 - mx).astype(BF16)
        v1 = jnp.concatenate([cv[:, sl, :].astype(BF16), ones], axis=1)
        v2 = jnp.concatenate([sv[:, sl, :].astype(BF16), ones], axis=1)
        o = _bdot(p1, v1, 2) + _bdot(p2, v2, 2)
        den = o[:, :, A_HD:A_HD + 1] + jnp.exp2(sink - mx)
        o = o[:, :, :A_HD] * (1.0 / den)
        pieces.extend(o[:, g * tq:(g + 1) * tq, :] for g in range(A_GROUP))
    o_ref[...] = jnp.concatenate(pieces, axis=2).reshape(nb * tq, D_MODEL)


def _swa_sample(aq, pf, sinks, cache_kt, cache_vt, carried, layer, dec_batch, tq):
    lt = SAMPLE_TILE
    nb = lt // tq
    wc = cache_kt.shape[3]
    assert wc == lt, "the appended keys are placed with lane rolls over one cache row"
    any_spec = pl.BlockSpec(memory_space=pl.ANY)
    cache_spec = pl.BlockSpec((1, nb, KV_W, wc), lambda i: (layer, i, 0, 0))
    operands = [sinks, aq, pf, pf, cache_kt, cache_vt]
    in_specs = [
        pl.BlockSpec(memory_space=pltpu.SMEM),
        pl.BlockSpec((lt, D_MODEL), lambda i: (i, 0)),
        pl.BlockSpec((lt, KV_W), lambda i: (i, 0)),
        pl.BlockSpec((lt, KV_W), lambda i: (i, 1)),
        cache_spec, cache_spec,
    ]
    aliases = {}
    if carried is not None:
        aliases = {len(operands) + k: 1 + k for k in range(2)}
        operands += list(carried)
        in_specs += [any_spec] * 2
    return pl.pallas_call(
        functools.partial(_swa_sample_kernel, tq=tq),
        grid=(dec_batch // nb,),
        in_specs=in_specs,
        out_specs=[
            pl.BlockSpec((lt, D_MODEL), lambda i: (i, 0)),
            cache_spec, cache_spec,
        ],
        out_shape=[
            jax.ShapeDtypeStruct((dec_batch * tq, D_MODEL), F32),
            jax.ShapeDtypeStruct(cache_kt.shape, F32),
            jax.ShapeDtypeStruct(cache_vt.shape, F32),
        ],
        input_output_aliases=aliases,
        compiler_params=_params("parallel"),
        name="swa_sample",
    )(*operands)


def _merge_kernel(x_ref, xn_ref, wg_ref, ba_ref, bb_ref, w_ref, o_ref):
    gates = _dot_nt(xn_ref[...], wg_ref[0])
    merged = _sigmoid(gates[:, :D_MODEL]) * ba_ref[...] + _sigmoid(gates[:, D_MODEL:]) * bb_ref[...]
    o_ref[...] = x_ref[...] + _dot(merged.astype(BF16), w_ref[0])


def _merge(x, xn, w_gates, ba, bb, w_out, layer):
    n = x.shape[0]
    tm = _pick(n, (512, 256, 128))
    row = pl.BlockSpec((tm, D_MODEL), lambda i: (i, 0))
    weight = lambda rows: pl.BlockSpec((1, rows, D_MODEL), lambda i: (layer, 0, 0))
    return pl.pallas_call(
        _merge_kernel,
        grid=(n // tm,),
        in_specs=[row, row, weight(2 * D_MODEL), row, row, weight(D_MODEL)],
        out_specs=row,
        out_shape=jax.ShapeDtypeStruct((n, D_MODEL), F32),
        compiler_params=_params("parallel"),
        name="merge_out_proj",
    )(x, xn, w_gates, ba, bb, w_out)


def _mlp_kernel(x_ref, g_ref, wu_ref, wd_ref, gf_ref, o_ref, xn_ref, *, final_norm):
    j = pl.program_id(1)

    @pl.when(j == 0)
    def _():
        x = x_ref[...]
        xn_ref[...] = _rms(x, g_ref[...]).astype(BF16)
        o_ref[...] = x

    h = jnp.square(jnp.maximum(_dot(xn_ref[...], wu_ref[0]), 0.0))
    o_ref[...] += _dot(h.astype(BF16), wd_ref[0])

    if final_norm:
        @pl.when(j == pl.num_programs(1) - 1)
        def _():
            o_ref[...] = _rms(o_ref[...], gf_ref[...])


def _mlp(x, g, w_up, w_down, g_final, layer, final_norm):
    n = x.shape[0]
    tm = _pick(n, (1024, 512, 256, 128))
    tf = 2048
    return pl.pallas_call(
        functools.partial(_mlp_kernel, final_norm=final_norm),
        grid=(n // tm, D_FF // tf),
        in_specs=[
            pl.BlockSpec((tm, D_MODEL), lambda i, j: (i, 0)),
            pl.BlockSpec((1, D_MODEL), lambda i, j: (0, 0)),
            pl.BlockSpec((1, D_MODEL, tf), lambda i, j: (layer, 0, j)),
            pl.BlockSpec((1, tf, D_MODEL), lambda i, j: (layer, j, 0)),
            pl.BlockSpec((1, D_MODEL), lambda i, j: (0, 0)),
        ],
        out_specs=pl.BlockSpec((tm, D_MODEL), lambda i, j: (i, 0)),
        out_shape=jax.ShapeDtypeStruct((n, D_MODEL), F32),
        scratch_shapes=[pltpu.VMEM((tm, D_MODEL), BF16)],
        compiler_params=_params("parallel", "arbitrary"),
        name="mlp",
    )(x, g, w_up, w_down, g_final)


def _prep_in_proj_weights(w_in):
    assert w_in.shape[2] == D_IN
    w_all = jnp.swapaxes(w_in, 1, 2).astype(BF16)
    return w_all, w_all[:, ROW_AQ:ROW_MERGE], w_all[:, ROW_MERGE:D_IN]


def kernel(x_prompt, x_sample, state_C, state_n, state_m, cache_k, cache_v, norm_attn, w_in, b_gate,
           mlstm_norm, sinks, w_out, norm_mlp, w_up, w_down, norm_final):
    batch, seq, _ = x_prompt.shape
    dec_batch, dec_seq, _ = x_sample.shape
    depth = w_in.shape[0]
    wc = cache_k.shape[2]

    w_all, w_att, w_merge = _prep_in_proj_weights(w_in)
    b_gate_col = jnp.zeros((depth, GATE_ROWS, 1), F32).at[:, :2 * M_HEADS, 0].set(b_gate.astype(F32))
    w_out_b = w_out.astype(BF16)
    w_up_b = w_up.astype(BF16)
    w_down_b = w_down.astype(BF16)
    bias = _band_bias()

    xp = x_prompt.reshape(batch * seq, D_MODEL)
    xs = x_sample.reshape(dec_batch * dec_seq, D_MODEL)
    m_tok = jnp.repeat(jnp.swapaxes(state_m, 1, 2), dec_seq, axis=2)[..., None]
    cache_kt = jnp.transpose(cache_k, (0, 1, 3, 4, 2)).reshape(depth, dec_batch, KV_W, wc)
    cache_vt = jnp.transpose(cache_v, (0, 1, 3, 4, 2)).reshape(depth, dec_batch, KV_W, wc)

    states = None
    caches = None
    p_c, p_n, p_m, p_k, p_v = [], [], [], [], []
    g_final = norm_final[None]

    for l in range(depth):
        g_attn = norm_attn[l][None]
        nw = mlstm_norm[l][None]
        g_mlp = norm_mlp[l][None]
        last = l == depth - 1
        proj_w = (w_all, w_att, b_gate_col[l], l)

        xn, mv, mq, pf, kt, gt, aq_t = _in_proj(xp, g_attn, *proj_w, q_transposed=True)
        ba, pc, pn, pm = _mlstm_prompt(xn, w_all, mv, mq, kt, gt, nw, l, batch, seq)
        xp, pk, pv = _swa_merge_prompt(aq_t, pf, sinks[l], bias, xp, xn, ba, w_merge, w_out_b, l, batch, seq)
        xp = _mlp(xp, g_mlp, w_up_b, w_down_b, g_final, l, last)
        p_c.append(pc)
        p_n.append(pn)
        p_m.append(pm[:, :M_HEADS, 0])
        p_k.append(pk)
        p_v.append(pv)

        xn, mv, mq, pf, kt, gt, aq = _in_proj(xs, g_attn, *proj_w, q_transposed=False)
        ba, *states = _mlstm_sample(xn, w_all, mv, mq, kt, gt, nw, m_tok, state_C, state_n, states,
                                    l, dec_batch, dec_seq)
        bb, *caches = _swa_sample(aq, pf, sinks[l], cache_kt, cache_vt, caches, l, dec_batch, dec_seq)
        xs = _merge(xs, xn, w_merge, ba, bb, w_out_b, l)
        xs = _mlp(xs, g_mlp, w_up_b, w_down_b, g_final, l, last)

    s_c, s_n, s_m = states
    s_k, s_v = caches

    def positions_major(t, lead):
        t = t.reshape(*lead, A_KV, A_HD, wc)
        return jnp.moveaxis(t, -1, -3)

    return (xp.reshape(batch, seq, D_MODEL), xs.reshape(dec_batch, dec_seq, D_MODEL),
            jnp.stack(p_c), jnp.stack(p_n), jnp.stack(p_m),
            positions_major(jnp.stack(p_k), (depth, batch)), positions_major(jnp.stack(p_v), (depth, batch)),
            s_c, s_n, jnp.swapaxes(s_m[..., 0], 1, 2),
            positions_major(s_k, (depth, dec_batch)), positions_major(s_v, (depth, dec_batch)))
```
